```python
import math
import jax
import jax.numpy as jnp
from jax import lax
import numpy as np

D_MODEL = 1024
BATCH = 4
SEQ = 8192
DEPTH = 4

CHUNK = 64
NORM_EPS = 1e-6
A_HEAD_DIM = 64
A_WIDTH = D_MODEL // 2
A_HEADS = A_WIDTH // A_HEAD_DIM
A_DECAY_LORA = 64
A_ICLR_LORA = 64
A_VRES_LORA = 32
A_GATE_LORA = 128
A_GN_EPS = 64e-5
B_HEAD_DIM = 128
B_WIDTH = D_MODEL // 2
B_HEADS = B_WIDTH // B_HEAD_DIM
B_CONV = 4
EVEN_IN = 3 * A_WIDTH + 4 * B_WIDTH + 2 * B_HEADS
MIX_WIDTH = A_WIDTH + B_WIDTH
C_HEAD_DIM = 64
C_HEADS = D_MODEL // C_HEAD_DIM
C_WIDTH = C_HEADS * C_HEAD_DIM
C_WINDOW_CHUNKS = 8
C_MAX_REL = 256
D_FF = 4 * D_MODEL
PLE_DIM = 256
N_EVEN = (DEPTH + 1) // 2
N_ODD = DEPTH // 2

kernel_name = 'hybrid_rwkv7_gdn_chunkattn_trunk'


def rmsnorm(x, g, eps=NORM_EPS):
    xf = x.astype(jnp.float32)
    y = xf * lax.rsqrt(jnp.mean(xf * xf, -1, keepdims=True) + eps)
    return (y * g.astype(jnp.float32)).astype(x.dtype)


def l2norm(x, eps=1e-6):
    xf = x.astype(jnp.float32)
    return xf * lax.rsqrt(jnp.sum(xf * xf, -1, keepdims=True) + eps)


def token_shift(z):
    return jnp.pad(z, ((0, 0), (1, 0), (0, 0)))[:, :-1]


def causal_depthwise_conv(z, w):
    kw = w.shape[0]
    zp = jnp.pad(z, ((0, 0), (kw - 1, 0), (0, 0)))
    return lax.conv_general_dilated(zp, w[:, None, :].astype(z.dtype), window_strides=(1,), padding='VALID',
                                    dimension_numbers=('NWC', 'WIO', 'NWC'), feature_group_count=z.shape[-1])


def wkv7_scan(r, w, k, v, a, b):
    bsz, seq, nh, n = r.shape
    xs = tuple(jnp.swapaxes(z, 0, 1) for z in (r, w, k, v, a, b))

    def step(state, inp):
        r_t, w_t, k_t, v_t, a_t, b_t = inp
        sa = jnp.einsum('bhij,bhj->bhi', state, a_t)
        state = state * w_t[:, :, None, :] + sa[..., None] * b_t[:, :, None, :] + v_t[..., None] * k_t[:, :, None, :]
        return state, jnp.einsum('bhij,bhj->bhi', state, r_t)

    s0 = jnp.zeros((bsz, nh, n, n), jnp.float32)
    _, ys = lax.scan(step, s0, xs)
    return jnp.swapaxes(ys, 0, 1)


def rwkv7_group(h, dh, r, k, v, v_first, mu_proj, mu_lora, w0, w1, w2, a0, a1, a2, g1, g2,
                k_k, k_a, r_k, ln_g, ln_b, vres):
    bsz, seq, _ = h.shape
    heads = lambda z: z.reshape(bsz, seq, A_HEADS, A_HEAD_DIM)
    r = r + (token_shift(r) - r) * mu_proj[0]
    k = k + (token_shift(k) - k) * mu_proj[1]
    v = v + (token_shift(v) - v) * mu_proj[2]
    xw = h + dh * mu_lora[0]
    xa = h + dh * mu_lora[1]
    xg = h + dh * mu_lora[2]
    w_log = -jax.nn.softplus(-(w0 + jnp.tanh(xw @ w1) @ w2).astype(jnp.float32)) - 0.5
    decay = jnp.exp(-jnp.exp(w_log))
    a = jax.nn.sigmoid(a0 + (xa @ a1) @ a2)
    g = jax.nn.sigmoid(xg @ g1) @ g2
    kk = l2norm(heads(k * k_k))
    k = k * (1.0 + (a - 1.0) * k_a)
    if vres is None:
        v_first = v
    else:
        v_mu, v0, v1, v2 = vres
        xv = h + dh * v_mu
        v = v + (v_first - v) * jax.nn.sigmoid(v0 + (xv @ v1) @ v2)
    rh, kh, vh, ah = [heads(z).astype(jnp.float32) for z in (r, k, v, a)]
    y = wkv7_scan(rh, heads(decay), kh, vh, -kk, kk * ah)
    mean = jnp.mean(y, -1, keepdims=True)
    var = jnp.mean(jnp.square(y - mean), -1, keepdims=True)
    gn_g = ln_g.astype(jnp.float32).reshape(A_HEADS, A_HEAD_DIM)
    gn_b = ln_b.astype(jnp.float32).reshape(A_HEADS, A_HEAD_DIM)
    y = (y - mean) * lax.rsqrt(var + A_GN_EPS) * gn_g + gn_b
    y = y + jnp.sum(rh * kh * r_k.astype(jnp.float32), -1, keepdims=True) * vh
    y = y.reshape(bsz, seq, A_WIDTH) * g.astype(jnp.float32)
    return y.astype(h.dtype), v_first


def chunk_gated_delta_rule(q, k, v, beta, g):
    bsz, seq, nh, dk = q.shape
    dv = v.shape[-1]
    n = seq // CHUNK

    def to_chunks(z):
        z = z.astype(jnp.float32).reshape((bsz, n, CHUNK) + z.shape[2:])
        return jnp.moveaxis(z, 3, 1)

    q, k, v, beta, g = [to_chunks(z) for z in (q, k, v, beta, g)]
    gc = jnp.cumsum(g, -1)
    causal = jnp.tril(jnp.ones((CHUNK, CHUNK), bool))
    strict = jnp.tril(jnp.ones((CHUNK, CHUNK), bool), -1)
    decay = jnp.exp(jnp.where(causal, gc[..., :, None] - gc[..., None, :], -jnp.inf))
    kb = k * beta[..., None]
    amat = jnp.where(strict, jnp.einsum('bhncd,bhnsd->bhncs', kb, k) * decay, 0.0)
    rhs = jnp.concatenate([v * beta[..., None], kb * jnp.exp(gc)[..., None]], -1)
    sol = lax.linalg.triangular_solve(amat, rhs, left_side=True, lower=True, unit_diagonal=True)
    u, w = sol[..., :dv], sol[..., dv:]
    qk = jnp.einsum('bhncd,bhnsd->bhncs', q, k) * decay
    q_dec = q * jnp.exp(gc)[..., None]
    k_dec = k * jnp.exp(gc[..., -1:] - gc)[..., None]
    g_last = jnp.exp(gc[..., -1])
    xs = tuple(jnp.moveaxis(z, 2, 0) for z in (q_dec, k_dec, w, u, qk, g_last))

    def step(state, inp):
        qd, kd, wc, uc, qkc, gl = inp
        v_new = uc - jnp.einsum('bhcd,bhde->bhce', wc, state)
        o = jnp.einsum('bhcd,bhde->bhce', qd, state) + jnp.einsum('bhcs,bhse->bhce', qkc, v_new)
        state = state * gl[..., None, None] + jnp.einsum('bhcd,bhce->bhde', kd, v_new)
        return state, o

    s0 = jnp.zeros((bsz, nh, dk, dv), jnp.float32)
    _, o = lax.scan(step, s0, xs)
    return o.transpose(1, 0, 3, 2, 4).reshape(bsz, seq, nh, dv)


def gdn_group(qkv, gate, beta_logit, alpha_logit, conv_w, a_log, dt_bias, norm_g):
    bsz, seq, _ = qkv.shape
    qkv = jax.nn.silu(causal_depthwise_conv(qkv, conv_w))
    heads = lambda z: z.reshape(bsz, seq, B_HEADS, B_HEAD_DIM)
    q = l2norm(heads(qkv[..., :B_WIDTH])) * (B_HEAD_DIM ** -0.5)
    k = l2norm(heads(qkv[..., B_WIDTH:2 * B_WIDTH]))
    v = heads(qkv[..., 2 * B_WIDTH:])
    beta = jax.nn.sigmoid(beta_logit.astype(jnp.float32))
    g = -jnp.exp(a_log.astype(jnp.float32)) * jax.nn.softplus((alpha_logit + dt_bias).astype(jnp.float32))
    o = chunk_gated_delta_rule(q, k, v, beta, g)
    o = rmsnorm(o, norm_g) * jax.nn.silu(heads(gate).astype(jnp.float32))
    return o.reshape(bsz, seq, B_WIDTH).astype(qkv.dtype)


def chunk_band_attention(h, w_qkv, q_g, k_g, rel_bias):
    bsz, seq, _ = h.shape
    qkv = (h @ w_qkv).reshape(bsz, seq, 3, C_HEADS, C_HEAD_DIM)
    q = rmsnorm(qkv[:, :, 0], q_g) * (C_HEAD_DIM ** -0.5)
    k = rmsnorm(qkv[:, :, 1], k_g)
    v = qkv[:, :, 2]
    pad = C_WINDOW_CHUNKS * CHUNK
    band = pad + CHUNK
    kp = jnp.pad(k, ((0, 0), (pad, 0), (0, 0), (0, 0)))
    vp = jnp.pad(v, ((0, 0), (pad, 0), (0, 0), (0, 0)))
    rel = jnp.arange(CHUNK)[:, None] + pad - jnp.arange(band)[None, :]
    bias = rel_bias[:, jnp.clip(rel, -C_MAX_REL, C_MAX_REL) + C_MAX_REL].astype(jnp.float32)

    def one_chunk(c):
        start = c * CHUNK
        q_c = lax.dynamic_slice_in_dim(q, start, CHUNK, axis=1)
        k_b = lax.dynamic_slice_in_dim(kp, start, band, axis=1)
        v_b = lax.dynamic_slice_in_dim(vp, start, band, axis=1)
        s = jnp.einsum('bqhd,bkhd->bhqk', q_c, k_b).astype(jnp.float32) + bias
        valid = (start - pad + jnp.arange(band)) >= 0
        s = jnp.where(valid, s, -jnp.inf)
        prob = jax.nn.softmax(s, -1).astype(v.dtype)
        return jnp.einsum('bhqk,bkhd->bqhd', prob, v_b)

    o = lax.map(one_chunk, jnp.arange(seq // CHUNK))
    return jnp.moveaxis(o, 0, 1).reshape(bsz, seq, C_WIDTH)


def setup_inputs(seed: int = 0) -> dict:
    key = jax.random.key(seed)
    keys = jax.random.split(key, 48)
    ctr = [0]

    def nk():
        kk = keys[ctr[0]]
        ctr[0] += 1
        return kk

    def nrm(shape, scale=1.0):
        return scale * jax.random.normal(nk(), shape, jnp.float32)

    def unif(shape, lo, hi):
        return jax.random.uniform(nk(), shape, jnp.float32, lo, hi)

    D, E, O, EV = D_MODEL, N_EVEN, N_ODD, N_EVEN - 1
    dt = jnp.exp(unif((E, B_HEADS), math.log(1e-3), math.log(1e-1)))
    return {
        'x': nrm((BATCH, SEQ, D)),
        'p': nrm((DEPTH, BATCH, SEQ, PLE_DIM)),
        'norm_mix_g': 1.0 + nrm((DEPTH, D), 0.05),
        'norm_ffn_g': 1.0 + nrm((DEPTH, D), 0.05),
        'even_w_in': nrm((E, D, EVEN_IN), D ** -0.5),
        'rwkv_mu_proj': unif((E, 3, A_WIDTH), 0.0, 1.0),
        'rwkv_mu_lora': unif((E, 3, D), 0.0, 1.0),
        'rwkv_w0': unif((E, A_WIDTH), -6.0, 0.0),
        'rwkv_w1': nrm((E, D, A_DECAY_LORA), D ** -0.5),
        'rwkv_w2': nrm((E, A_DECAY_LORA, A_WIDTH), 0.5 * A_DECAY_LORA ** -0.5),
        'rwkv_a0': nrm((E, A_WIDTH), 0.5),
        'rwkv_a1': nrm((E, D, A_ICLR_LORA), D ** -0.5),
        'rwkv_a2': nrm((E, A_ICLR_LORA, A_WIDTH), 0.5 * A_ICLR_LORA ** -0.5),
        'rwkv_g1': nrm((E, D, A_GATE_LORA), D ** -0.5),
        'rwkv_g2': nrm((E, A_GATE_LORA, A_WIDTH), A_GATE_LORA ** -0.5),
        'rwkv_k_k': 0.85 + nrm((E, A_WIDTH), 0.05),
        'rwkv_k_a': 1.0 + nrm((E, A_WIDTH), 0.05),
        'rwkv_r_k': nrm((E, A_HEADS, A_HEAD_DIM), 0.1),
        'rwkv_ln_g': 1.0 + nrm((E, A_WIDTH), 0.05),
        'rwkv_ln_b': nrm((E, A_WIDTH), 0.02),
        'rwkv_v_mu': unif((EV, D), 0.0, 1.0),
        'rwkv_v0': nrm((EV, A_WIDTH), 0.5),
        'rwkv_v1': nrm((EV, D, A_VRES_LORA), D ** -0.5),
        'rwkv_v2': nrm((EV, A_VRES_LORA, A_WIDTH), 0.5 * A_VRES_LORA ** -0.5),
        'gdn_conv_w': nrm((E, B_CONV, 3 * B_WIDTH), B_CONV ** -0.5),
        'gdn_a_log': jnp.log(unif((E, B_HEADS), 1.0, 16.0)),
        'gdn_dt_bias': dt + jnp.log(-jnp.expm1(-dt)),
        'gdn_norm_g': 1.0 + nrm((E, B_HEAD_DIM), 0.05),
        'even_w_out': nrm((E, MIX_WIDTH, D), MIX_WIDTH ** -0.5),
        'attn_w_qkv': nrm((O, D, 3 * C_WIDTH), D ** -0.5),
        'attn_q_g': 1.0 + nrm((O, C_HEAD_DIM), 0.05),
        'attn_k_g': 1.0 + nrm((O, C_HEAD_DIM), 0.05),
        'attn_rel_bias': nrm((O, C_HEADS, 2 * C_MAX_REL + 1), 0.5),
        'attn_w_out': nrm((O, C_WIDTH, D), C_WIDTH ** -0.5),
        'mlp_w1': nrm((DEPTH, D, D_FF), D ** -0.5),
        'mlp_w2': nrm((DEPTH, D_FF, D), D_FF ** -0.5),
        'ple_w_proj': nrm((DEPTH, PLE_DIM, D), PLE_DIM ** -0.5),
        'ple_norm_g': 1.0 + nrm((DEPTH, D), 0.05),
        'ple_w_gate': nrm((DEPTH, D, D), D ** -0.5),
    }


def reference(x, p, norm_mix_g, norm_ffn_g, even_w_in, rwkv_mu_proj, rwkv_mu_lora, rwkv_w0, rwkv_w1, rwkv_w2,
              rwkv_a0, rwkv_a1, rwkv_a2, rwkv_g1, rwkv_g2, rwkv_k_k, rwkv_k_a, rwkv_r_k, rwkv_ln_g, rwkv_ln_b,
              rwkv_v_mu, rwkv_v0, rwkv_v1, rwkv_v2, gdn_conv_w, gdn_a_log, gdn_dt_bias, gdn_norm_g, even_w_out,
              attn_w_qkv, attn_q_g, attn_k_g, attn_rel_bias, attn_w_out, mlp_w1, mlp_w2,
              ple_w_proj, ple_norm_g, ple_w_gate):
    o_ak, o_av, o_bq = A_WIDTH, 2 * A_WIDTH, 3 * A_WIDTH
    o_bg = o_bq + 3 * B_WIDTH
    o_bb = o_bg + B_WIDTH
    o_ba = o_bb + B_HEADS
    v_first = None
    for i in range(DEPTH):
        h = rmsnorm(x, norm_mix_g[i])
        if i % 2 == 0:
            e = i // 2
            proj = h @ even_w_in[e]
            dh = token_shift(h) - h
            vres = None if e == 0 else (rwkv_v_mu[e - 1], rwkv_v0[e - 1], rwkv_v1[e - 1], rwkv_v2[e - 1])
            y_a, v_first = rwkv7_group(h, dh, proj[..., :o_ak], proj[..., o_ak:o_av], proj[..., o_av:o_bq], v_first,
                                       rwkv_mu_proj[e], rwkv_mu_lora[e], rwkv_w0[e], rwkv_w1[e], rwkv_w2[e],
                                       rwkv_a0[e], rwkv_a1[e], rwkv_a2[e], rwkv_g1[e], rwkv_g2[e],
                                       rwkv_k_k[e], rwkv_k_a[e], rwkv_r_k[e], rwkv_ln_g[e], rwkv_ln_b[e], vres)
            y_b = gdn_group(proj[..., o_bq:o_bg], proj[..., o_bg:o_bb], proj[..., o_bb:o_ba], proj[..., o_ba:],
                            gdn_conv_w[e], gdn_a_log[e], gdn_dt_bias[e], gdn_norm_g[e])
            mix = jnp.concatenate([y_a, y_b], -1) @ even_w_out[e]
        else:
            o = i // 2
            mix = chunk_band_attention(h, attn_w_qkv[o], attn_q_g[o], attn_k_g[o], attn_rel_bias[o]) @ attn_w_out[o]
        x = x + mix
        hf = rmsnorm(x, norm_ffn_g[i])
        x = x + jnp.square(jax.nn.relu(hf @ mlp_w1[i])) @ mlp_w2[i]
        x = x + rmsnorm(p[i] @ ple_w_proj[i], ple_norm_g[i]) * jax.nn.sigmoid(x @ ple_w_gate[i])
    return x
```

```python
import functools

import jax
import jax.numpy as jnp
from jax import lax
from jax.experimental import pallas as pl
from jax.experimental.pallas import tpu as pltpu

F32 = jnp.float32
BF16 = jnp.bfloat16
HIGHEST = lax.Precision.HIGHEST

D_MODEL = 1024
CHUNK = 64
NORM_EPS = 1e-6
L2_EPS = 1e-6
A_WIDTH = 512
A_HEAD = 64
A_GN_EPS = 64e-5
A_LORA_PAD = 128
B_WIDTH = 512
B_HEADS = 4
B_HEAD = 128
B_CONV = 4
C_HEADS = 16
C_HEAD = 64
C_WINDOW = 8 * CHUNK
C_BAND = C_WINDOW + CHUNK
C_MAX_REL = 256
D_FF = 4096
FF_CHUNK = 1024
LANES = 128
SUBLANES = 8
INV_BLOCK = 16
MASK_VALUE = -1e30
VMEM_LIMIT = 56 * 1024 * 1024


def _dot(a, b):
    return jnp.dot(a.astype(BF16), b.astype(BF16), preferred_element_type=F32)


def _dot_nt(a, b):
    return lax.dot_general(a.astype(BF16), b.astype(BF16), (((1,), (1,)), ((), ())),
                           preferred_element_type=F32)


def _hdot(a, b):
    return jnp.dot(a, b, precision=HIGHEST, preferred_element_type=F32)


def _hdot_nt(a, b):
    return lax.dot_general(a, b, (((1,), (1,)), ((), ())), precision=HIGHEST,
                           preferred_element_type=F32)


def _hdot_tn(a, b):
    return jnp.dot(a.T, b, precision=HIGHEST, preferred_element_type=F32)


def _rms(x, g, eps=NORM_EPS):
    return x * lax.rsqrt(jnp.mean(x * x, axis=-1, keepdims=True) + eps) * g


def _sigmoid(z):
    return 1.0 / (1.0 + jnp.exp(-z))


def _softplus(z):
    return jnp.maximum(z, 0.0) + jnp.log(1.0 + jnp.exp(-jnp.abs(z)))


def _shift_rows(x, prev_row):
    row = lax.broadcasted_iota(jnp.int32, x.shape, 0)
    return jnp.where(row == 0, prev_row, pltpu.roll(x, 1, 0))


def _tri_masks(n):
    r = lax.broadcasted_iota(jnp.int32, (n, n), 0)
    c = lax.broadcasted_iota(jnp.int32, (n, n), 1)
    shift = INV_BLOCK.bit_length() - 1
    return r >= c, r > c, (r >> shift) == (c >> shift)


def _unit_lower_inverse(n_mat, same_block):
    nd = jnp.where(same_block, n_mat, 0.0)
    no = n_mat - nd
    td = nd
    pw = nd
    for _ in range(3):
        pw = _hdot(pw, pw)
        td = td + pw + _hdot(td, pw)
    e = no + _hdot(td, no)
    e2 = _hdot(e, e)
    f = e + e2 + _hdot(e, e2)
    return f + td + _hdot(f, td)


def _apply_inverse(minv_minus_i, x):
    return x + _hdot(minv_minus_i, x)


def _seg_sum_pair(x, lane_lo):
    s_lo = jnp.sum(jnp.where(lane_lo, x, 0.0), axis=-1, keepdims=True)
    s_hi = jnp.sum(jnp.where(lane_lo, 0.0, x), axis=-1, keepdims=True)
    return jnp.where(lane_lo, s_lo, s_hi)


def _even_in_kernel(x_ref, halo_ref, g_ref, win_ref, wtail_ref, mu_ref, l1_ref, l2_ref, lb_ref,
                    rkv_ref, qkvb_ref, gate_ref, ba_ref, lw_ref, a_ref, gg_ref, *vg_ref,
                    tiles_per_seq):
    i = pl.program_id(0)
    g = g_ref[...]
    h = _rms(x_ref[...], g)
    prev = _rms(halo_ref[...], g)[SUBLANES - 1:SUBLANES, :]
    prev = jnp.where(i % tiles_per_seq == 0, 0.0, prev)
    dh = _shift_rows(h, prev) - h
    hb = h.astype(BF16)
    rkv_ref[...] = _dot(hb, win_ref[:, 0:3 * A_WIDTH])
    qkvb_ref[...] = _dot(hb, win_ref[:, 3 * A_WIDTH:3 * A_WIDTH + 3 * B_WIDTH])
    gate_ref[...] = _dot(hb, win_ref[:, 3 * A_WIDTH + 3 * B_WIDTH:])
    ba_ref[...] = _dot(hb, wtail_ref[...])

    def lora_in(j):
        return _dot(h + dh * mu_ref[j:j + 1, :], l1_ref[j])

    w_raw = lb_ref[0:1, :] + _dot(jnp.tanh(lora_in(0)), l2_ref[0])
    w_log = -_softplus(-w_raw) - 0.5
    lw_ref[...] = -jnp.exp(w_log)
    a_ref[...] = _sigmoid(lb_ref[1:2, :] + _dot(lora_in(1), l2_ref[1]))
    gg_ref[...] = _dot(_sigmoid(lora_in(2)), l2_ref[2])
    if vg_ref:
        vg_ref[0][...] = _sigmoid(lb_ref[2:3, :] + _dot(lora_in(3), l2_ref[3]))


def _even_in(x2d, seq, g, win, wtail, mu, l1, l2, lb, has_vres, tm):
    m = x2d.shape[0]
    n_out = 8 if has_vres else 7
    widths = [3 * A_WIDTH, 3 * B_WIDTH, B_WIDTH, LANES, A_WIDTH, A_WIDTH, A_WIDTH, A_WIDTH][:n_out]
    const = lambda a: pl.BlockSpec(a.shape, lambda i: (0,) * a.ndim)
    return pl.pallas_call(
        functools.partial(_even_in_kernel, tiles_per_seq=seq // tm),
        grid=(m // tm,),
        in_specs=[pl.BlockSpec((tm, D_MODEL), lambda i: (i, 0)),
                  pl.BlockSpec((SUBLANES, D_MODEL), lambda i: (jnp.maximum(i * (tm // SUBLANES) - 1, 0), 0)),
                  const(g), const(win), const(wtail), const(mu), const(l1), const(l2), const(lb)],
        out_specs=[pl.BlockSpec((tm, w), lambda i: (i, 0)) for w in widths],
        out_shape=[jax.ShapeDtypeStruct((m, w), F32) for w in widths],
        compiler_params=pltpu.CompilerParams(dimension_semantics=("parallel",), vmem_limit_bytes=VMEM_LIMIT),
        name="even_in",
    )(x2d, x2d, g, win, wtail, mu, l1, l2, lb)


def _rwkv_kernel(rkv_ref, lw_ref, a_ref, gg_ref, *rest, has_vres):
    if has_vres:
        vg_ref, vfirst_ref, par_ref, mu_ref, y_ref, state_ref, tail_ref = rest
    else:
        par_ref, mu_ref, y_ref, vfirst_out_ref, state_ref, tail_ref = rest
    c = pl.program_id(1)

    @pl.when(c == 0)
    def _():
        state_ref[...] = jnp.zeros_like(state_ref)
        tail_ref[...] = jnp.zeros_like(tail_ref)

    rkv = rkv_ref[...]
    rkv_prev = _shift_rows(rkv, tail_ref[SUBLANES - 1:SUBLANES, :])
    tail_ref[...] = rkv[CHUNK - SUBLANES:, :]
    rkv = rkv + (rkv_prev - rkv) * mu_ref[...]
    r = rkv[:, 0:A_WIDTH]
    k = rkv[:, A_WIDTH:2 * A_WIDTH]
    v = rkv[:, 2 * A_WIDTH:]
    k_k, k_a, r_k, ln_g, ln_b = (par_ref[j:j + 1, :] for j in range(5))
    a = a_ref[...]
    if has_vres:
        v = v + (vfirst_ref[...] - v) * vg_ref[...]
    else:
        vfirst_out_ref[...] = v

    incl, strict, same_block = _tri_masks(CHUNK)
    lw = lw_ref[...]
    cw = _hdot(incl.astype(F32), lw)
    cw_last = cw[CHUNK - 1:CHUNK, :]
    kk_raw = k * k_k
    k2 = k * (1.0 + (a - 1.0) * k_a)
    e_neg = jnp.exp(-cw)
    e_last = jnp.exp(cw_last - cw)
    r_t = r * jnp.exp(cw)
    rk_bonus = r * k2 * r_k

    lane = lax.broadcasted_iota(jnp.int32, (CHUNK, LANES), 1)
    lane_lo = lane < A_HEAD
    row2 = lax.broadcasted_iota(jnp.int32, (CHUNK, LANES), 0)
    col2 = jnp.where(lane_lo, lane, lane - CHUNK)
    incl2 = row2 >= col2
    strict2 = row2 > col2
    blk_r =lax.broadcasted_iota(jnp.int32, (LANES, LANES), 0) < A_HEAD
    blk_c = lax.broadcasted_iota(jnp.int32, (LANES, LANES), 1) < A_HEAD
    block_diag = blk_r == blk_c

    for p in range(A_WIDTH // LANES):
        sl = slice(p * LANES, (p + 1) * LANES)
        kk_p = kk_raw[:, sl]
        kk_p = kk_p * lax.rsqrt(_seg_sum_pair(kk_p * kk_p, lane_lo) + L2_EPS)
        b_p = kk_p * a[:, sl]
        a_t = -kk_p * jnp.exp(cw[:, sl] - lw[:, sl])
        b_t = b_p * e_neg[:, sl]
        k_t = k2[:, sl] * e_neg[:, sl]
        v_p = v[:, sl]
        rt_p = r_t[:, sl]
        s_vk = state_ref[p]
        x_state = _hdot_nt(a_t, s_vk)
        y_state = _hdot_nt(rt_p, s_vk)
        bk_t = jnp.concatenate([b_t, k_t], axis=0)
        vv = jnp.concatenate([v_p, v_p], axis=0)
        u_heads, y_heads = [], []
        for head in range(2):
            keep = lane_lo if head == 0 else jnp.logical_not(lane_lo)
            a_gram = _hdot_nt(jnp.where(keep, a_t, 0.0), bk_t)
            r_gram = _hdot_nt(jnp.where(keep, rt_p, 0.0), bk_t)
            a_ab = jnp.where(strict, a_gram[:, :CHUNK], 0.0)
            minv = _unit_lower_inverse(a_ab, same_block)
            a_ak = jnp.where(jnp.logical_and(strict2, jnp.logical_not(lane_lo)), a_gram, 0.0)
            u_h = _apply_inverse(minv, x_state + _hdot(a_ak, vv))
            g_cat = jnp.where(incl2, r_gram, 0.0)
            y_heads.append(y_state + _hdot(g_cat, jnp.concatenate([u_h, v_p], axis=0)))
            u_heads.append(u_h)
        u = jnp.where(lane_lo, u_heads[0], u_heads[1])
        y = jnp.where(lane_lo, y_heads[0], y_heads[1])
        upd = _hdot_tn(jnp.concatenate([u, v_p], axis=0),
                       jnp.concatenate([b_p * e_last[:, sl], k2[:, sl] * e_last[:, sl]], axis=0))
        state_ref[p] = s_vk * jnp.exp(cw_last[:, sl]) + jnp.where(block_diag, upd, 0.0)

        mean = _seg_sum_pair(y, lane_lo) * (1.0 / A_HEAD)
        yc = y - mean
        var = _seg_sum_pair(yc * yc, lane_lo) * (1.0 / A_HEAD)
        yn = yc * lax.rsqrt(var + A_GN_EPS) * ln_g[:, sl] + ln_b[:, sl]
        yn = yn + _seg_sum_pair(rk_bonus[:, sl], lane_lo) * v_p
        y_ref[:, sl] = yn * gg_ref[:, sl]


def _rwkv(rkv, lw, a, gg, vg, vfirst, par, mu, batch, seq):
    m = rkv.shape[0]
    nc = seq // CHUNK
    has_vres = vg is not None
    row = lambda w: pl.BlockSpec((CHUNK, w), lambda b, c: (b * nc + c, 0))
    const = lambda arr: pl.BlockSpec(arr.shape, lambda b, c: (0,) * arr.ndim)
    ins = [rkv, lw, a, gg] + ([vg, vfirst] if has_vres else []) + [par, mu]
    in_specs = [row(3 * A_WIDTH)] + [row(A_WIDTH)] * (5 if has_vres else 3) + [const(par), const(mu)]
    n_out = 1 if has_vres else 2
    outs = pl.pallas_call(
        functools.partial(_rwkv_kernel, has_vres=has_vres),
        grid=(batch, nc),
        in_specs=in_specs,
        out_specs=[row(A_WIDTH)] * n_out,
        out_shape=[jax.ShapeDtypeStruct((m, A_WIDTH), F32)] * n_out,
        scratch_shapes=[pltpu.VMEM((A_WIDTH // LANES, LANES, LANES), F32),
                        pltpu.VMEM((SUBLANES, 3 * A_WIDTH), F32)],
        compiler_params=pltpu.CompilerParams(dimension_semantics=("arbitrary", "arbitrary"),
                                             vmem_limit_bytes=VMEM_LIMIT),
        name="rwkv7_scan",
    )(*ins)
    return outs if not has_vres else (outs[0], vfirst)


def _gdn_kernel(qkv_ref, gate_ref, ba_ref, conv_ref, hp_ref, ng_ref, y_ref, state_ref, zz_ref):
    c = pl.program_id(1)

    @pl.when(c == 0)
    def _():
        state_ref[...] = jnp.zeros_like(state_ref)
        zz_ref[0:SUBLANES, :] = jnp.zeros((SUBLANES, 3 * B_WIDTH), F32)

    z = qkv_ref[...]
    zz_ref[SUBLANES:, :] = z
    zz = zz_ref[...]
    zz_ref[0:SUBLANES, :] = z[CHUNK - SUBLANES:, :]
    conv = zz[SUBLANES:, :] * conv_ref[B_CONV - 1:B_CONV, :]
    for j in range(B_CONV - 1):
        conv = conv + pltpu.roll(zz, B_CONV - 1 - j, 0)[SUBLANES:, :] * conv_ref[j:j + 1, :]
    qkv = conv * _sigmoid(conv)

    incl, strict, same_block = _tri_masks(CHUNK)
    ba = ba_ref[...]
    beta = _sigmoid(ba)
    g_step = -jnp.exp(hp_ref[0:1, :]) * _softplus(ba + hp_ref[1:2, :])
    gc = _hdot(incl.astype(F32), g_step)
    gc_t = gc.T

    for hd in range(B_HEADS):
        sl = slice(hd * B_HEAD, (hd + 1) * B_HEAD)
        q = qkv[:, sl]
        k = qkv[:, B_WIDTH + hd * B_HEAD:B_WIDTH + (hd + 1) * B_HEAD]
        v = qkv[:, 2 * B_WIDTH + hd * B_HEAD:2 * B_WIDTH + (hd + 1) * B_HEAD]
        q = q * lax.rsqrt(jnp.sum(q * q, axis=-1, keepdims=True) + L2_EPS) * (B_HEAD ** -0.5)
        k = k * lax.rsqrt(jnp.sum(k * k, axis=-1, keepdims=True) + L2_EPS)
        beta_c = beta[:, hd:hd + 1]
        g_col = gc[:, B_HEADS + hd:B_HEADS + hd + 1]
        g_row = gc_t[B_HEADS + hd:B_HEADS + hd + 1, :]
        g_last = g_col[CHUNK - 1:CHUNK, :]
        decay = jnp.where(incl, jnp.exp(jnp.where(incl, g_col - g_row, 0.0)), 0.0)
        kb = k * beta_c
        amat = jnp.where(strict, _hdot_nt(kb, k) * decay, 0.0)
        tinv = _unit_lower_inverse(-amat, same_block)
        e_g = jnp.exp(g_col)
        u = _apply_inverse(tinv, v * beta_c)
        w = _apply_inverse(tinv, kb * e_g)
        qk = _hdot_nt(q, k) * decay
        s_kv = state_ref[hd]
        v_new = u - _hdot(w, s_kv)
        o = _hdot(q * e_g, s_kv) + _hdot(qk, v_new)
        state_ref[hd] = s_kv * jnp.exp(g_last) + _hdot_tn(k * jnp.exp(g_last - g_col), v_new)
        gate = gate_ref[:, sl]
        y_ref[:, sl] = _rms(o, ng_ref[...]) * (gate * _sigmoid(gate))


def _gdn(qkvb, gate, ba, conv_w, hp, norm_g, batch, seq):
    m = qkvb.shape[0]
    nc = seq // CHUNK
    row = lambda w: pl.BlockSpec((CHUNK, w), lambda b, c: (b * nc + c, 0))
    const = lambda arr: pl.BlockSpec(arr.shape, lambda b, c: (0,) * arr.ndim)
    return pl.pallas_call(
        _gdn_kernel,
        grid=(batch, nc),
        in_specs=[row(3 * B_WIDTH), row(B_WIDTH), row(LANES), const(conv_w), const(hp), const(norm_g)],
        out_specs=row(B_WIDTH),
        out_shape=jax.ShapeDtypeStruct((m, B_WIDTH), F32),
        scratch_shapes=[pltpu.VMEM((B_HEADS, B_HEAD, B_HEAD), F32),
                        pltpu.VMEM((CHUNK + SUBLANES, 3 * B_WIDTH), F32)],
        compiler_params=pltpu.CompilerParams(dimension_semantics=("arbitrary", "arbitrary"),
                                             vmem_limit_bytes=VMEM_LIMIT),
        name="gdn_scan",
    )(qkvb, gate, ba, conv_w, hp, norm_g)


def _head_rms(x, ones_bd, g):
    sq = x * x
    hi = sq.astype(BF16)
    lo = (sq - hi.astype(F32)).astype(BF16)
    ssum = (jnp.dot(hi, ones_bd, preferred_element_type=F32)
            + jnp.dot(lo, ones_bd, preferred_element_type=F32))
    return x * lax.rsqrt(ssum * (1.0 / C_HEAD) + NORM_EPS) * g


def _attn_qkv_kernel(x_ref, g_ref, w_ref, qg_ref, kg_ref, ones_ref, o_ref):
    t = pl.program_id(1)

    @pl.when(t == 0)
    def _():
        o_ref[...] = jnp.zeros_like(o_ref)

    @pl.when(t > 0)
    def _():
        hb = _rms(x_ref[...], g_ref[...]).astype(BF16)
        ones_bd = ones_ref[...]
        blk = ones_bd.shape[0]
        for j in range(D_MODEL // blk):
            sl = slice(j * blk, (j + 1) * blk)
            q = _dot(hb, w_ref[:, sl])
            o_ref[:, sl] = _head_rms(q, ones_bd, qg_ref[...]) * (C_HEAD ** -0.5)
            ksl = slice(D_MODEL + j * blk, D_MODEL + (j + 1) * blk)
            k = _dot(hb, w_ref[:, ksl])
            o_ref[:, ksl] = _head_rms(k, ones_bd, kg_ref[...])
        o_ref[:, 2 * D_MODEL:] = _dot(hb, w_ref[:, 2 * D_MODEL:])


def _attn_qkv(x2d, g, w, qg, kg, ones_bd, batch, seq, tm):
    tiles = seq // tm
    pad_tiles = C_WINDOW // tm
    const = lambda arr: pl.BlockSpec(arr.shape, lambda b, t: (0,) * arr.ndim)
    return pl.pallas_call(
        _attn_qkv_kernel,
        grid=(batch, tiles + pad_tiles),
        in_specs=[pl.BlockSpec((tm, D_MODEL), lambda b, t: (b * tiles + jnp.maximum(t - pad_tiles, 0), 0)),
                  const(g), const(w), const(qg), const(kg), const(ones_bd)],
        out_specs=pl.BlockSpec((tm, 3 * D_MODEL), lambda b, t: (b * (tiles + pad_tiles) + t, 0)),
        out_shape=jax.ShapeDtypeStruct((batch * (seq + C_WINDOW), 3 * D_MODEL), F32),
        compiler_params=pltpu.CompilerParams(dimension_semantics=("parallel", "arbitrary"),
                                             vmem_limit_bytes=VMEM_LIMIT),
        name="attn_qkv",
    )(x2d, g, w, qg, kg, ones_bd)


def _attn_kernel(q_ref, k_ref, v_ref, bias_ref, o_ref):
    c = pl.program_id(2)
    start = pl.multiple_of(c * CHUNK, CHUNK)
    kw = k_ref[pl.ds(start, C_BAND), :].astype(BF16)
    vw = v_ref[pl.ds(start, C_BAND), :].astype(BF16)
    q = q_ref[...]
    lane_lo = lax.broadcasted_iota(jnp.int32, (CHUNK, LANES), 1) < C_HEAD
    key_idx = lax.broadcasted_iota(jnp.int32, (CHUNK, C_BAND), 1)
    valid = key_idx >= C_WINDOW - c * CHUNK
    outs = []
    for head in range(2):
        keep = lane_lo if head == 0 else jnp.logical_not(lane_lo)
        s = _dot_nt(jnp.where(keep, q, 0.0), kw) + bias_ref[head]
        s = jnp.where(valid, s, MASK_VALUE)
        p = jnp.exp(s - jnp.max(s, axis=-1, keepdims=True))
        denom = jnp.sum(p, axis=-1, keepdims=True)
        outs.append(_dot(p, vw) / denom)
    o_ref[...] = jnp.where(lane_lo, outs[0], outs[1])


def _attn(qkv_pad, bias, batch, seq):
    nc = seq // CHUNK
    pad_rows = seq + C_WINDOW
    n_pairs = D_MODEL // LANES
    q_off = C_WINDOW // CHUNK
    return pl.pallas_call(
        _attn_kernel,
        grid=(batch, n_pairs, nc),
        in_specs=[pl.BlockSpec((CHUNK, LANES), lambda b, h, c: (b * (pad_rows // CHUNK) + q_off + c, h)),
                  pl.BlockSpec((pad_rows, LANES), lambda b, h, c: (b, n_pairs + h)),
                  pl.BlockSpec((pad_rows, LANES), lambda b, h, c: (b, 2 * n_pairs + h)),
                  pl.BlockSpec((2, CHUNK, C_BAND), lambda b, h, c: (h, 0, 0))],
        out_specs=pl.BlockSpec((CHUNK, LANES), lambda b, h, c: (b * nc + c, h)),
        out_shape=jax.ShapeDtypeStruct((batch * seq, D_MODEL), F32),
        compiler_params=pltpu.CompilerParams(dimension_semantics=("parallel", "parallel", "arbitrary"),
                                             vmem_limit_bytes=VMEM_LIMIT),
        name="band_attn",
    )(qkv_pad, qkv_pad, qkv_pad, bias)


def _post_kernel(x_ref, *rest, n_mix):
    mix_refs = rest[:n_mix]
    wo_ref, g_ref, w1_ref, w2_ref, o_ref = rest[n_mix:]
    mix = jnp.concatenate([mref[...].astype(BF16) for mref in mix_refs], axis=1)
    x = x_ref[...] + _dot(mix, wo_ref[...])
    hb = _rms(x, g_ref[...]).astype(BF16)
    acc = None
    for j in range(D_FF // FF_CHUNK):
        sl = slice(j * FF_CHUNK, (j + 1) * FF_CHUNK)
        hid = jnp.maximum(_dot(hb, w1_ref[:, sl]), 0.0)
        part = _dot(hid * hid, w2_ref[sl, :])
        acc = part if acc is None else acc + part
    o_ref[...] = x + acc


def _post(x2d, mixes, wo, g, w1, w2, tm):
    m = x2d.shape[0]
    const = lambda arr: pl.BlockSpec(arr.shape, lambda i: (0,) * arr.ndim)
    row = lambda w: pl.BlockSpec((tm, w), lambda i: (i, 0))
    return pl.pallas_call(
        functools.partial(_post_kernel, n_mix=len(mixes)),
        grid=(m // tm,),
        in_specs=[row(D_MODEL)] + [row(mx.shape[1]) for mx in mixes] + [const(wo), const(g), const(w1), const(w2)],
        out_specs=row(D_MODEL),
        out_shape=jax.ShapeDtypeStruct((m, D_MODEL), F32),
        compiler_params=pltpu.CompilerParams(dimension_semantics=("parallel",), vmem_limit_bytes=VMEM_LIMIT),
        name="mix_out_mlp",
    )(x2d, *mixes, wo, g, w1, w2)


def _ple_kernel(x_ref, p_ref, wp_ref, g_ref, wg_ref, o_ref):
    x = x_ref[...]
    emb = _rms(_dot(p_ref[...], wp_ref[...]), g_ref[...])
    o_ref[...] = x + emb * _sigmoid(_dot(x, wg_ref[...]))


def _ple(x2d, p2d, wp, g, wg, tm):
    m = x2d.shape[0]
    const = lambda arr: pl.BlockSpec(arr.shape, lambda i: (0,) * arr.ndim)
    row = lambda w: pl.BlockSpec((tm, w), lambda i: (i, 0))
    return pl.pallas_call(
        _ple_kernel,
        grid=(m // tm,),
        in_specs=[row(D_MODEL), row(p2d.shape[1]), const(wp), const(g), const(wg)],
        out_specs=row(D_MODEL),
        out_shape=jax.ShapeDtypeStruct((m, D_MODEL), F32),
        compiler_params=pltpu.CompilerParams(dimension_semantics=("parallel",), vmem_limit_bytes=VMEM_LIMIT),
        name="ple_gate",
    )(x2d, p2d, wp, g, wg)


def _pad_to(w, rows=None, cols=None):
    r = (rows or w.shape[0]) - w.shape[0]
    c = (cols or w.shape[1]) - w.shape[1]
    return jnp.pad(w, ((0, r), (0, c)))


def kernel(x, p, norm_mix_g, norm_ffn_g, even_w_in, rwkv_mu_proj, rwkv_mu_lora, rwkv_w0, rwkv_w1, rwkv_w2, rwkv_a0, rwkv_a1, rwkv_a2, rwkv_g1, rwkv_g2, rwkv_k_k, rwkv_k_a, rwkv_r_k, rwkv_ln_g, rwkv_ln_b, rwkv_v_mu, rwkv_v0, rwkv_v1, rwkv_v2, gdn_conv_w, gdn_a_log, gdn_dt_bias, gdn_norm_g, even_w_out, attn_w_qkv, attn_q_g, attn_k_g, attn_rel_bias, attn_w_out, mlp_w1, mlp_w2, ple_w_proj, ple_norm_g, ple_w_gate):
    batch, seq, _ = x.shape
    depth = p.shape[0]
    assert seq % 512 == 0 and x.shape[2] == D_MODEL
    tm_in, tm_post, tm_ple, tm_qkv = 256, 256, 512, 512
    xs = x.reshape(batch * seq, D_MODEL)
    row1 = lambda vec: vec.reshape(1, -1)
    main_cols = 3 * A_WIDTH + 4 * B_WIDTH

    rel = jnp.arange(CHUNK)[:, None] + C_WINDOW - jnp.arange(C_BAND)[None, :]
    rel_idx = jnp.clip(rel, -C_MAX_REL, C_MAX_REL) + C_MAX_REL
    blk = 4 * C_HEAD
    ones_bd = (jnp.arange(blk)[:, None] // C_HEAD == jnp.arange(blk)[None, :] // C_HEAD).astype(BF16)

    v_first = None
    for i in range(depth):
        if i % 2 == 0:
            e = i // 2
            has_vres = e > 0
            win = even_w_in[e]
            mus = [rwkv_mu_lora[e, 0], rwkv_mu_lora[e, 1], rwkv_mu_lora[e, 2]]
            l1s = [rwkv_w1[e], rwkv_a1[e], rwkv_g1[e]]
            l2s = [rwkv_w2[e], rwkv_a2[e], rwkv_g2[e]]
            lbs = [rwkv_w0[e], rwkv_a0[e]]
            if has_vres:
                mus.append(rwkv_v_mu[e - 1])
                l1s.append(rwkv_v1[e - 1])
                l2s.append(rwkv_v2[e - 1])
                lbs.append(rwkv_v0[e - 1])
            l1 = jnp.stack([_pad_to(w, cols=A_LORA_PAD) for w in l1s]).astype(BF16)
            l2 = jnp.stack([_pad_to(w, rows=A_LORA_PAD) for w in l2s]).astype(BF16)
            outs = _even_in(xs, seq, row1(norm_mix_g[i]), win[:, :main_cols].astype(BF16),
                            _pad_to(win[:, main_cols:], cols=LANES).astype(BF16),
                            jnp.stack(mus), l1, l2, jnp.stack(lbs), has_vres, tm_in)
            rkv, qkvb, gate, ba, lw, a_lr, gg = outs[:7]
            par = jnp.stack([rwkv_k_k[e], rwkv_k_a[e], rwkv_r_k[e].reshape(-1), rwkv_ln_g[e], rwkv_ln_b[e]])
            y_a, v_first = _rwkv(rkv, lw, a_lr, gg, outs[7] if has_vres else None, v_first, par,
                                 rwkv_mu_proj[e].reshape(1, -1), batch, seq)
            hp = jnp.stack([_pad_to(jnp.pad(row1(gdn_a_log[e]), ((0, 0), (B_HEADS, 0))), cols=LANES)[0],
                            _pad_to(jnp.pad(row1(gdn_dt_bias[e]), ((0, 0), (B_HEADS, 0))), cols=LANES)[0]])
            y_b = _gdn(qkvb, gate, ba, gdn_conv_w[e], hp, row1(gdn_norm_g[e]), batch, seq)
            mixes, wo = [y_a, y_b], even_w_out[e]
        else:
            o = i // 2
            tile4 = lambda gvec: jnp.tile(gvec, blk // C_HEAD).reshape(1, blk)
            qkv_pad = _attn_qkv(xs, row1(norm_mix_g[i]), attn_w_qkv[o].astype(BF16), tile4(attn_q_g[o]),
                                tile4(attn_k_g[o]), ones_bd, batch, seq, tm_qkv)
            bias = attn_rel_bias[o][:, rel_idx]
            mixes, wo = [_attn(qkv_pad, bias, batch, seq)], attn_w_out[o]
        xs = _post(xs, mixes, wo.astype(BF16), row1(norm_ffn_g[i]), mlp_w1[i].astype(BF16),
                   mlp_w2[i].astype(BF16), tm_post)
        xs = _ple(xs, p[i].reshape(batch * seq, -1), ple_w_proj[i].astype(BF16), row1(ple_norm_g[i]),
                  ple_w_gate[i].astype(BF16), tm_ple)
    return xs.reshape(batch, seq, D_MODEL)
```

```python
import functools

import jax
import jax.numpy as jnp
from jax import lax
from jax.experimental import pallas as pl
from jax.experimental.pallas import tpu as pltpu

F32 = jnp.float32
BF16 = jnp.bfloat16

D_MODEL = 1024
CHUNK = 64
NORM_EPS = 1e-6
L2_EPS = 1e-6
A_WIDTH = 512
A_HEAD = 64
A_GN_EPS = 64e-5
A_LORA_PAD = 128
A_GROUP = 256
B_WIDTH = 512
B_HEADS = 4
B_HEAD = 128
B_CONV = 4
C_HEADS = 16
C_HEAD = 64
C_WINDOW = 8 * CHUNK
C_BAND = C_WINDOW + CHUNK
C_MAX_REL = 256
D_FF = 4096
FF_CHUNK = 1024
LANES = 128
SUBLANES = 8
INV_BLOCK = 16
MASK_VALUE = -1e30
VMEM_LIMIT = 56 * 1024 * 1024


def _dot(a, b):
    return jnp.dot(a.astype(BF16), b.astype(BF16), preferred_element_type=F32)


def _dot_nt(a, b):
    return lax.dot_general(a.astype(BF16), b.astype(BF16), (((1,), (1,)), ((), ())),
                           preferred_element_type=F32)


def _dot_tn(a, b):
    return jnp.dot(a.astype(BF16).T, b.astype(BF16), preferred_element_type=F32)


def _rms(x, g, eps=NORM_EPS):
    return x * lax.rsqrt(jnp.mean(x * x, axis=-1, keepdims=True) + eps) * g


def _sigmoid(z):
    return 1.0 / (1.0 + jnp.exp(-z))


def _softplus(z):
    return jnp.maximum(z, 0.0) + jnp.log(1.0 + jnp.exp(-jnp.abs(z)))


def _shift_rows(x, prev_row):
    row = lax.broadcasted_iota(jnp.int32, x.shape, 0)
    return jnp.where(row == 0, prev_row, pltpu.roll(x, 1, 0))


def _cumsum_rows(x):
    row = lax.broadcasted_iota(jnp.int32, x.shape, 0)
    step = 1
    while step < x.shape[0]:
        x = x + jnp.where(row >= step, pltpu.roll(x, step, 0), 0.0)
        step *= 2
    return x


def _stack_heads(x, n_heads):
    head_w = x.shape[1] // n_heads
    lane_head = lax.broadcasted_iota(jnp.int32, x.shape, 1) >> (head_w.bit_length() - 1)
    zero = jnp.zeros_like(x)
    return jnp.concatenate([jnp.where(lane_head == h, x, zero) for h in range(n_heads)], axis=0)


def _unstack_heads(xs, n_heads):
    out = xs[0:CHUNK]
    for h in range(1, n_heads):
        out = out + xs[h * CHUNK:(h + 1) * CHUNK]
    return out


def _block_masks(n_heads):
    n = n_heads * CHUNK
    r = lax.broadcasted_iota(jnp.int32, (n, n), 0)
    c = lax.broadcasted_iota(jnp.int32, (n, n), 1)
    same_head = (r ^ c) < CHUNK
    delta = jnp.where(same_head, r - c, -1)
    return same_head, delta >= 0, delta > 0, (r ^ c) < INV_BLOCK


def _unit_lower_inverse(n_mat, same_block):
    nd = jnp.where(same_block, n_mat, 0.0)
    no = n_mat - nd
    td = nd
    pw = nd
    for _ in range(3):
        pw = _dot(pw, pw)
        td = td + pw + _dot(td, pw)
    e = no + _dot(td, no)
    e2 = _dot(e, e)
    f = e + e2 + _dot(e, e2)
    return f + td + _dot(f, td)


def _seg_sum(x, seg_lo):
    outs = []
    for j in range(x.shape[1] // LANES):
        blk = x[:, j * LANES:(j + 1) * LANES]
        s_lo = jnp.sum(jnp.where(seg_lo, blk, 0.0), axis=-1, keepdims=True)
        s_hi = jnp.sum(jnp.where(seg_lo, 0.0, blk), axis=-1, keepdims=True)
        outs.append(jnp.where(seg_lo, s_lo, s_hi))
    return jnp.concatenate(outs, axis=1)


def _even_in_kernel(x_ref, halo_ref, g_ref, win_ref, wtail_ref, mu_ref, l1_ref, l2_ref, lb_ref,
                    rkv_ref, qkvb_ref, gate_ref, ba_ref, lw_ref, a_ref, gg_ref, *vg_ref,
                    tiles_per_seq):
    i = pl.program_id(0)
    g = g_ref[...]
    h = _rms(x_ref[...], g)
    prev = _rms(halo_ref[...], g)[SUBLANES - 1:SUBLANES, :]
    prev = jnp.where(i % tiles_per_seq == 0, 0.0, prev)
    dh = _shift_rows(h, prev) - h
    hb = h.astype(BF16)
    rkv_ref[...] = _dot(hb, win_ref[:, 0:3 * A_WIDTH])
    qkvb_ref[...] = _dot(hb, win_ref[:, 3 * A_WIDTH:3 * A_WIDTH + 3 * B_WIDTH])
    gate_ref[...] = _dot(hb, win_ref[:, 3 * A_WIDTH + 3 * B_WIDTH:])
    ba_ref[...] = _dot(hb, wtail_ref[...])

    def lora_in(j):
        return _dot(h + dh * mu_ref[j:j + 1, :], l1_ref[j])

    w_raw = lb_ref[0:1, :] + _dot(jnp.tanh(lora_in(0)), l2_ref[0])
    w_log = -_softplus(-w_raw) - 0.5
    lw_ref[...] = -jnp.exp(w_log)
    a_ref[...] = _sigmoid(lb_ref[1:2, :] + _dot(lora_in(1), l2_ref[1]))
    gg_ref[...] = _dot(_sigmoid(lora_in(2)), l2_ref[2])
    if vg_ref:
        vg_ref[0][...] = _sigmoid(lb_ref[2:3, :] + _dot(lora_in(3), l2_ref[3]))


def _even_in(x2d, seq, g, win, wtail, mu, l1, l2, lb, has_vres, tm):
    m = x2d.shape[0]
    n_out = 8 if has_vres else 7
    widths = [3 * A_WIDTH, 3 * B_WIDTH, B_WIDTH, LANES, A_WIDTH, A_WIDTH, A_WIDTH, A_WIDTH][:n_out]
    const = lambda a: pl.BlockSpec(a.shape, lambda i: (0,) * a.ndim)
    return pl.pallas_call(
        functools.partial(_even_in_kernel, tiles_per_seq=seq // tm),
        grid=(m // tm,),
        in_specs=[pl.BlockSpec((tm, D_MODEL), lambda i: (i, 0)),
                  pl.BlockSpec((SUBLANES, D_MODEL), lambda i: (jnp.maximum(i * (tm // SUBLANES) - 1, 0), 0)),
                  const(g), const(win), const(wtail), const(mu), const(l1), const(l2), const(lb)],
        out_specs=[pl.BlockSpec((tm, w), lambda i: (i, 0)) for w in widths],
        out_shape=[jax.ShapeDtypeStruct((m, w), F32) for w in widths],
        compiler_params=pltpu.CompilerParams(dimension_semantics=("parallel",), vmem_limit_bytes=VMEM_LIMIT),
        name="even_in",
    )(x2d, x2d, g, win, wtail, mu, l1, l2, lb)


def _rwkv_kernel(rkv_ref, lw_ref, a_ref, gg_ref, *rest, has_vres):
    if has_vres:
        vg_ref, vfirst_ref, par_ref, mu_ref, y_ref, state_ref, tail_ref = rest
    else:
        par_ref, mu_ref, y_ref, vfirst_out_ref, state_ref, tail_ref = rest
    c = pl.program_id(1)

    @pl.when(c == 0)
    def _():
        state_ref[...] = jnp.zeros_like(state_ref)
        tail_ref[...] = jnp.zeros_like(tail_ref)

    rkv = rkv_ref[...]
    rkv_prev = _shift_rows(rkv, tail_ref[SUBLANES - 1:SUBLANES, :])
    tail_ref[...] = rkv[CHUNK - SUBLANES:, :]
    rkv = rkv + (rkv_prev - rkv) * mu_ref[...]
    r = rkv[:, 0:A_WIDTH]
    k = rkv[:, A_WIDTH:2 * A_WIDTH]
    v = rkv[:, 2 * A_WIDTH:]
    k_k, k_a, r_k, ln_g, ln_b = (par_ref[j:j + 1, :] for j in range(5))
    a = a_ref[...]
    if has_vres:
        v = v + (vfirst_ref[...] - v) * vg_ref[...]
    else:
        vfirst_out_ref[...] = v

    seg_lo = lax.broadcasted_iota(jnp.int32, (CHUNK, LANES), 1) < A_HEAD
    lw = lw_ref[...]
    cw = _cumsum_rows(lw)
    cw_last = cw[CHUNK - 1:CHUNK, :]
    kk = k * k_k
    kk = kk * lax.rsqrt(_seg_sum(kk * kk, seg_lo) + L2_EPS)
    k2 = k * (1.0 + (a - 1.0) * k_a)
    b = kk * a
    e_neg = jnp.exp(-cw)
    e_last = jnp.exp(cw_last - cw)
    r_t = (r * jnp.exp(cw)).astype(BF16)
    a_t = (-kk * jnp.exp(cw - lw)).astype(BF16)
    b_t = (b * e_neg).astype(BF16)
    k_t = (k2 * e_neg).astype(BF16)
    b_hat = (b * e_last).astype(BF16)
    k_hat = (k2 * e_last).astype(BF16)
    bonus = _seg_sum(r * k2 * r_k, seg_lo) * v
    v_bf = v.astype(BF16)

    heads = A_GROUP // A_HEAD
    n_rows = heads * CHUNK
    same_head, incl, strict, same_block = _block_masks(heads)
    incl2 = jnp.concatenate([incl, incl], axis=1)
    for g in range(A_WIDTH // A_GROUP):
        sl = slice(g * A_GROUP, (g + 1) * A_GROUP)
        s_vk = state_ref[g]
        lhs = jnp.concatenate([_stack_heads(a_t[:, sl], heads), _stack_heads(r_t[:, sl], heads)], axis=0)
        rhs = jnp.concatenate([b_t[:, sl]] * heads + [k_t[:, sl]] * heads, axis=0)
        gram = _dot_nt(lhs, rhs)
        minv = _unit_lower_inverse(jnp.where(strict, gram[0:n_rows, 0:n_rows], 0.0), same_block)
        a_ak = jnp.where(strict, gram[0:n_rows, n_rows:], 0.0)
        g_cat = jnp.where(incl2, gram[n_rows:, :], 0.0)
        from_state = _dot_nt(jnp.concatenate([a_t[:, sl], r_t[:, sl]], axis=0), s_vk)
        v_s = _stack_heads(v_bf[:, sl], heads)
        x = from_state[0:CHUNK] + _unstack_heads(_dot(a_ak, v_s), heads)
        u = x + _unstack_heads(_dot(minv, _stack_heads(x.astype(BF16), heads)), heads)
        u_bf = u.astype(BF16)
        y_s = _dot(g_cat, jnp.concatenate([_stack_heads(u_bf, heads), v_s], axis=0))
        y = from_state[CHUNK:] + _unstack_heads(y_s, heads)
        upd = _dot_tn(jnp.concatenate([u_bf, v_bf[:, sl]], axis=0),
                      jnp.concatenate([b_hat[:, sl], k_hat[:, sl]], axis=0))
        state_ref[g] = s_vk * jnp.exp(cw_last[:, sl]) + jnp.where(same_head, upd, 0.0)

        mean = _seg_sum(y, seg_lo) * (1.0 / A_HEAD)
        yc = y - mean
        var = _seg_sum(yc * yc, seg_lo) * (1.0 / A_HEAD)
        yn = yc * lax.rsqrt(var + A_GN_EPS) * ln_g[:, sl] + ln_b[:, sl]
        y_ref[:, sl] = (yn + bonus[:, sl]) * gg_ref[:, sl]


def _rwkv(rkv, lw, a, gg, vg, vfirst, par, mu, batch, seq):
    m = rkv.shape[0]
    nc = seq // CHUNK
    has_vres = vg is not None
    row = lambda w: pl.BlockSpec((CHUNK, w), lambda b, c: (b * nc + c, 0))
    const = lambda arr: pl.BlockSpec(arr.shape, lambda b, c: (0,) * arr.ndim)
    ins = [rkv, lw, a, gg] + ([vg, vfirst] if has_vres else []) + [par, mu]
    in_specs = [row(3 * A_WIDTH)] + [row(A_WIDTH)] * (5 if has_vres else 3) + [const(par), const(mu)]
    n_out = 1 if has_vres else 2
    outs = pl.pallas_call(
        functools.partial(_rwkv_kernel, has_vres=has_vres),
        grid=(batch, nc),
        in_specs=in_specs,
        out_specs=[row(A_WIDTH)] * n_out,
        out_shape=[jax.ShapeDtypeStruct((m, A_WIDTH), F32)] * n_out,
        scratch_shapes=[pltpu.VMEM((A_WIDTH // A_GROUP, A_GROUP, A_GROUP), F32),
                        pltpu.VMEM((SUBLANES, 3 * A_WIDTH), F32)],
        compiler_params=pltpu.CompilerParams(dimension_semantics=("arbitrary", "arbitrary"),
                                             vmem_limit_bytes=VMEM_LIMIT),
        name="rwkv7_scan",
    )(*ins)
    return outs if not has_vres else (outs[0], vfirst)


def _gdn_kernel(qkv_ref, gate_ref, ba_ref, conv_ref, hp_ref, ng_ref, y_ref, state_ref, zz_ref):
    c = pl.program_id(1)

    @pl.when(c == 0)
    def _():
        state_ref[...] = jnp.zeros_like(state_ref)
        zz_ref[0:SUBLANES, :] = jnp.zeros((SUBLANES, 3 * B_WIDTH), F32)

    z = qkv_ref[...]
    zz_ref[SUBLANES:, :] = z
    zz = zz_ref[...]
    zz_ref[0:SUBLANES, :] = z[CHUNK - SUBLANES:, :]
    conv = zz[SUBLANES:, :] * conv_ref[B_CONV - 1:B_CONV, :]
    for j in range(B_CONV - 1):
        conv = conv + pltpu.roll(zz, B_CONV - 1 - j, 0)[SUBLANES:, :] * conv_ref[j:j + 1, :]
    qkv = conv * _sigmoid(conv)

    ba = ba_ref[...]
    beta = _sigmoid(ba)
    g_step = -jnp.exp(hp_ref[0:1, :]) * _softplus(ba + hp_ref[1:2, :])
    gc = _cumsum_rows(g_step)
    gc_t = gc.T

    def per_head(tile, lane0):
        return jnp.concatenate([jnp.broadcast_to(tile[:, lane0 + h:lane0 + h + 1], (CHUNK, B_HEAD))
                                for h in range(B_HEADS)], axis=1)

    def l2n(z):
        return jnp.concatenate(
            [z[:, h * B_HEAD:(h + 1) * B_HEAD]
             * lax.rsqrt(jnp.sum(jnp.square(z[:, h * B_HEAD:(h + 1) * B_HEAD]), axis=-1, keepdims=True) + L2_EPS)
             for h in range(B_HEADS)], axis=1)

    q = l2n(qkv[:, 0:B_WIDTH]) * (B_HEAD ** -0.5)
    k = l2n(qkv[:, B_WIDTH:2 * B_WIDTH])
    v = qkv[:, 2 * B_WIDTH:]
    beta_f = per_head(beta, 0)
    g_col = per_head(gc, B_HEADS)
    g_last = g_col[CHUNK - 1:CHUNK, :]
    e_g = jnp.exp(g_col)
    kb = k * beta_f
    k_bf = k.astype(BF16)

    same_head, incl, strict, same_block = _block_masks(B_HEADS)
    n_rows = B_HEADS * CHUNK
    g_col_s = jnp.concatenate([gc[:, B_HEADS + h:B_HEADS + h + 1] for h in range(B_HEADS)], axis=0)
    g_row_s = jnp.concatenate([gc_t[B_HEADS + h:B_HEADS + h + 1, :] for h in range(B_HEADS)], axis=1)
    decay = jnp.where(incl, jnp.exp(jnp.where(incl, g_col_s - g_row_s, 0.0)), 0.0)
    lhs = jnp.concatenate([_stack_heads(kb.astype(BF16), B_HEADS), _stack_heads(q.astype(BF16), B_HEADS)], axis=0)
    gram = _dot_nt(lhs, jnp.concatenate([k_bf] * B_HEADS, axis=0))
    amat = jnp.where(strict, gram[0:n_rows] * decay, 0.0)
    qk = gram[n_rows:] * decay
    tinv = _unit_lower_inverse(-amat, same_block)
    vb = v * beta_f
    kbg = kb * e_g
    solved = _dot(tinv, jnp.concatenate([_stack_heads(vb.astype(BF16), B_HEADS),
                                         _stack_heads(kbg.astype(BF16), B_HEADS)], axis=1))
    u = vb + _unstack_heads(solved[:, 0:B_WIDTH], B_HEADS)
    w = kbg + _unstack_heads(solved[:, B_WIDTH:], B_HEADS)
    wq = jnp.concatenate([w, q * e_g], axis=0).astype(BF16)
    kd = (k * jnp.exp(g_last - g_col)).astype(BF16)
    v_new, q_state = [], []
    for hd in range(B_HEADS):
        sl = slice(hd * B_HEAD, (hd + 1) * B_HEAD)
        s_kv = state_ref[hd]
        from_state = _dot(wq[:, sl], s_kv)
        vn = u[:, sl] - from_state[0:CHUNK]
        state_ref[hd] = s_kv * jnp.exp(g_last[:, sl]) + _dot_tn(kd[:, sl], vn)
        v_new.append(vn)
        q_state.append(from_state[CHUNK:])
    v_new = jnp.concatenate(v_new, axis=1)
    o = jnp.concatenate(q_state, axis=1) + _unstack_heads(_dot(qk, _stack_heads(v_new.astype(BF16), B_HEADS)), B_HEADS)
    for hd in range(B_HEADS):
        sl = slice(hd * B_HEAD, (hd + 1) * B_HEAD)
        gate = gate_ref[:, sl]
        y_ref[:, sl] = _rms(o[:, sl], ng_ref[...]) * (gate * _sigmoid(gate))


def _gdn(qkvb, gate, ba, conv_w, hp, norm_g, batch, seq):
    m = qkvb.shape[0]
    nc = seq // CHUNK
    row = lambda w: pl.BlockSpec((CHUNK, w), lambda b, c: (b * nc + c, 0))
    const = lambda arr: pl.BlockSpec(arr.shape, lambda b, c: (0,) * arr.ndim)
    return pl.pallas_call(
        _gdn_kernel,
        grid=(batch, nc),
        in_specs=[row(3 * B_WIDTH), row(B_WIDTH), row(LANES), const(conv_w), const(hp), const(norm_g)],
        out_specs=row(B_WIDTH),
        out_shape=jax.ShapeDtypeStruct((m, B_WIDTH), F32),
        scratch_shapes=[pltpu.VMEM((B_HEADS, B_HEAD, B_HEAD), F32),
                        pltpu.VMEM((CHUNK + SUBLANES, 3 * B_WIDTH), F32)],
        compiler_params=pltpu.CompilerParams(dimension_semantics=("arbitrary", "arbitrary"),
                                             vmem_limit_bytes=VMEM_LIMIT),
        name="gdn_scan",
    )(qkvb, gate, ba, conv_w, hp, norm_g)


def _head_rms(x, ones_bd, g):
    sq = x * x
    hi = sq.astype(BF16)
    lo = (sq - hi.astype(F32)).astype(BF16)
    ssum = (jnp.dot(hi, ones_bd, preferred_element_type=F32)
            + jnp.dot(lo, ones_bd, preferred_element_type=F32))
    return x * lax.rsqrt(ssum * (1.0 / C_HEAD) + NORM_EPS) * g


def _attn_qkv_kernel(x_ref, g_ref, w_ref, qg_ref, kg_ref, ones_ref, o_ref):
    t = pl.program_id(1)

    @pl.when(t == 0)
    def _():
        o_ref[...] = jnp.zeros_like(o_ref)

    @pl.when(t > 0)
    def _():
        hb = _rms(x_ref[...], g_ref[...]).astype(BF16)
        ones_bd = ones_ref[...]
        blk = ones_bd.shape[0]
        for j in range(D_MODEL // blk):
            sl = slice(j * blk, (j + 1) * blk)
            q = _dot(hb, w_ref[:, sl])
            o_ref[:, sl] = _head_rms(q, ones_bd, qg_ref[...]) * (C_HEAD ** -0.5)
            ksl = slice(D_MODEL + j * blk, D_MODEL + (j + 1) * blk)
            k = _dot(hb, w_ref[:, ksl])
            o_ref[:, ksl] = _head_rms(k, ones_bd, kg_ref[...])
        o_ref[:, 2 * D_MODEL:] = _dot(hb, w_ref[:, 2 * D_MODEL:])


def _attn_qkv(x2d, g, w, qg, kg, ones_bd, batch, seq, tm):
    tiles = seq // tm
    pad_tiles = C_WINDOW // tm
    const = lambda arr: pl.BlockSpec(arr.shape, lambda b, t: (0,) * arr.ndim)
    return pl.pallas_call(
        _attn_qkv_kernel,
        grid=(batch, tiles + pad_tiles),
        in_specs=[pl.BlockSpec((tm, D_MODEL), lambda b, t: (b * tiles + jnp.maximum(t - pad_tiles, 0), 0)),
                  const(g), const(w), const(qg), const(kg), const(ones_bd)],
        out_specs=pl.BlockSpec((tm, 3 * D_MODEL), lambda b, t: (b * (tiles + pad_tiles) + t, 0)),
        out_shape=jax.ShapeDtypeStruct((batch * (seq + C_WINDOW), 3 * D_MODEL), F32),
        compiler_params=pltpu.CompilerParams(dimension_semantics=("parallel", "arbitrary"),
                                             vmem_limit_bytes=VMEM_LIMIT),
        name="attn_qkv",
    )(x2d, g, w, qg, kg, ones_bd)


def _attn_kernel(q_ref, k_ref, v_ref, bias_ref, o_ref):
    c = pl.program_id(2)
    start = pl.multiple_of(c * CHUNK, CHUNK)
    kw = k_ref[pl.ds(start, C_BAND), :].astype(BF16)
    vw = v_ref[pl.ds(start, C_BAND), :].astype(BF16)
    q = q_ref[...]
    lane_lo = lax.broadcasted_iota(jnp.int32, (CHUNK, LANES), 1) < C_HEAD
    key_idx = lax.broadcasted_iota(jnp.int32, (CHUNK, C_BAND), 1)
    valid = key_idx >= C_WINDOW - c * CHUNK
    outs = []
    for head in range(2):
        keep = lane_lo if head == 0 else jnp.logical_not(lane_lo)
        s = _dot_nt(jnp.where(keep, q, 0.0), kw) + bias_ref[head]
        s = jnp.where(valid, s, MASK_VALUE)
        p = jnp.exp(s - jnp.max(s, axis=-1, keepdims=True))
        denom = jnp.sum(p, axis=-1, keepdims=True)
        outs.append(_dot(p, vw) / denom)
    o_ref[...] = jnp.where(lane_lo, outs[0], outs[1])


def _attn(qkv_pad, bias, batch, seq):
    nc = seq // CHUNK
    pad_rows = seq + C_WINDOW
    n_pairs = D_MODEL // LANES
    q_off = C_WINDOW // CHUNK
    return pl.pallas_call(
        _attn_kernel,
        grid=(batch, n_pairs, nc),
        in_specs=[pl.BlockSpec((CHUNK, LANES), lambda b, h, c: (b * (pad_rows // CHUNK) + q_off + c, h)),
                  pl.BlockSpec((pad_rows, LANES), lambda b, h, c: (b, n_pairs + h)),
                  pl.BlockSpec((pad_rows, LANES), lambda b, h, c: (b, 2 * n_pairs + h)),
                  pl.BlockSpec((2, CHUNK, C_BAND), lambda b, h, c: (h, 0, 0))],
        out_specs=pl.BlockSpec((CHUNK, LANES), lambda b, h, c: (b * nc + c, h)),
        out_shape=jax.ShapeDtypeStruct((batch * seq, D_MODEL), F32),
        compiler_params=pltpu.CompilerParams(dimension_semantics=("parallel", "parallel", "arbitrary"),
                                             vmem_limit_bytes=VMEM_LIMIT),
        name="band_attn",
    )(qkv_pad, qkv_pad, qkv_pad, bias)


def _post_kernel(x_ref, *rest, n_mix):
    mix_refs = rest[:n_mix]
    wo_ref, g_ref, w1_ref, w2_ref, o_ref = rest[n_mix:]
    mix = jnp.concatenate([mref[...].astype(BF16) for mref in mix_refs], axis=1)
    x = x_ref[...] + _dot(mix, wo_ref[...])
    hb = _rms(x, g_ref[...]).astype(BF16)
    acc = None
    for j in range(D_FF // FF_CHUNK):
        sl = slice(j * FF_CHUNK, (j + 1) * FF_CHUNK)
        hid = jnp.maximum(_dot(hb, w1_ref[:, sl]), 0.0)
        part = _dot(hid * hid, w2_ref[sl, :])
        acc = part if acc is None else acc + part
    o_ref[...] = x + acc


def _post(x2d, mixes, wo, g, w1, w2, tm):
    m = x2d.shape[0]
    const = lambda arr: pl.BlockSpec(arr.shape, lambda i: (0,) * arr.ndim)
    row = lambda w: pl.BlockSpec((tm, w), lambda i: (i, 0))
    return pl.pallas_call(
        functools.partial(_post_kernel, n_mix=len(mixes)),
        grid=(m // tm,),
        in_specs=[row(D_MODEL)] + [row(mx.shape[1]) for mx in mixes] + [const(wo), const(g), const(w1), const(w2)],
        out_specs=row(D_MODEL),
        out_shape=jax.ShapeDtypeStruct((m, D_MODEL), F32),
        compiler_params=pltpu.CompilerParams(dimension_semantics=("parallel",), vmem_limit_bytes=VMEM_LIMIT),
        name="mix_out_mlp",
    )(x2d, *mixes, wo, g, w1, w2)


def _ple_kernel(x_ref, p_ref, wp_ref, g_ref, wg_ref, o_ref):
    x = x_ref[...]
    emb = _rms(_dot(p_ref[...], wp_ref[...]), g_ref[...])
    o_ref[...] = x + emb * _sigmoid(_dot(x, wg_ref[...]))


def _ple(x2d, p2d, wp, g, wg, tm):
    m = x2d.shape[0]
    const = lambda arr: pl.BlockSpec(arr.shape, lambda i: (0,) * arr.ndim)
    row = lambda w: pl.BlockSpec((tm, w), lambda i: (i, 0))
    return pl.pallas_call(
        _ple_kernel,
        grid=(m // tm,),
        in_specs=[row(D_MODEL), row(p2d.shape[1]), const(wp), const(g), const(wg)],
        out_specs=row(D_MODEL),
        out_shape=jax.ShapeDtypeStruct((m, D_MODEL), F32),
        compiler_params=pltpu.CompilerParams(dimension_semantics=("parallel",), vmem_limit_bytes=VMEM_LIMIT),
        name="ple_gate",
    )(x2d, p2d, wp, g, wg)


def _pad_to(w, rows=None, cols=None):
    r = (rows or w.shape[0]) - w.shape[0]
    c = (cols or w.shape[1]) - w.shape[1]
    return jnp.pad(w, ((0, r), (0, c)))


def kernel(x, p, norm_mix_g, norm_ffn_g, even_w_in, rwkv_mu_proj, rwkv_mu_lora, rwkv_w0, rwkv_w1, rwkv_w2, rwkv_a0, rwkv_a1, rwkv_a2, rwkv_g1, rwkv_g2, rwkv_k_k, rwkv_k_a, rwkv_r_k, rwkv_ln_g, rwkv_ln_b, rwkv_v_mu, rwkv_v0, rwkv_v1, rwkv_v2, gdn_conv_w, gdn_a_log, gdn_dt_bias, gdn_norm_g, even_w_out, attn_w_qkv, attn_q_g, attn_k_g, attn_rel_bias, attn_w_out, mlp_w1, mlp_w2, ple_w_proj, ple_norm_g, ple_w_gate):
    batch, seq, _ = x.shape
    depth = p.shape[0]
    assert seq % 512 == 0 and x.shape[2] == D_MODEL
    tm_in, tm_post, tm_ple, tm_qkv = 256, 256, 512, 512
    xs = x.reshape(batch * seq, D_MODEL)
    row1 = lambda vec: vec.reshape(1, -1)
    main_cols = 3 * A_WIDTH + 4 * B_WIDTH

    rel = jnp.arange(CHUNK)[:, None] + C_WINDOW - jnp.arange(C_BAND)[None, :]
    rel_idx = jnp.clip(rel, -C_MAX_REL, C_MAX_REL) + C_MAX_REL
    blk = 4 * C_HEAD
    ones_bd = (jnp.arange(blk)[:, None] // C_HEAD == jnp.arange(blk)[None, :] // C_HEAD).astype(BF16)

    v_first = None
    for i in range(depth):
        if i % 2 == 0:
            e = i // 2
            has_vres = e > 0
            win = even_w_in[e]
            mus = [rwkv_mu_lora[e, 0], rwkv_mu_lora[e, 1], rwkv_mu_lora[e, 2]]
            l1s = [rwkv_w1[e], rwkv_a1[e], rwkv_g1[e]]
            l2s = [rwkv_w2[e], rwkv_a2[e], rwkv_g2[e]]
            lbs = [rwkv_w0[e], rwkv_a0[e]]
            if has_vres:
                mus.append(rwkv_v_mu[e - 1])
                l1s.append(rwkv_v1[e - 1])
                l2s.append(rwkv_v2[e - 1])
                lbs.append(rwkv_v0[e - 1])
            l1 = jnp.stack([_pad_to(w, cols=A_LORA_PAD) for w in l1s]).astype(BF16)
            l2 = jnp.stack([_pad_to(w, rows=A_LORA_PAD) for w in l2s]).astype(BF16)
            outs = _even_in(xs, seq, row1(norm_mix_g[i]), win[:, :main_cols].astype(BF16),
                            _pad_to(win[:, main_cols:], cols=LANES).astype(BF16),
                            jnp.stack(mus), l1, l2, jnp.stack(lbs), has_vres, tm_in)
            rkv, qkvb, gate, ba, lw, a_lr, gg = outs[:7]
            par = jnp.stack([rwkv_k_k[e], rwkv_k_a[e], rwkv_r_k[e].reshape(-1), rwkv_ln_g[e], rwkv_ln_b[e]])
            y_a, v_first = _rwkv(rkv, lw, a_lr, gg, outs[7] if has_vres else None, v_first, par,
                                 rwkv_mu_proj[e].reshape(1, -1), batch, seq)
            hp = jnp.stack([_pad_to(jnp.pad(row1(gdn_a_log[e]), ((0, 0), (B_HEADS, 0))), cols=LANES)[0],
                            _pad_to(jnp.pad(row1(gdn_dt_bias[e]), ((0, 0), (B_HEADS, 0))), cols=LANES)[0]])
            y_b = _gdn(qkvb, gate, ba, gdn_conv_w[e], hp, row1(gdn_norm_g[e]), batch, seq)
            mixes, wo = [y_a, y_b], even_w_out[e]
        else:
            o = i // 2
            tile4 = lambda gvec: jnp.tile(gvec, blk // C_HEAD).reshape(1, blk)
            qkv_pad = _attn_qkv(xs, row1(norm_mix_g[i]), attn_w_qkv[o].astype(BF16), tile4(attn_q_g[o]),
                                tile4(attn_k_g[o]), ones_bd, batch, seq, tm_qkv)
            bias = attn_rel_bias[o][:, rel_idx]
            mixes, wo = [_attn(qkv_pad, bias, batch, seq)], attn_w_out[o]
        xs = _post(xs, mixes, wo.astype(BF16), row1(norm_ffn_g[i]), mlp_w1[i].astype(BF16),
                   mlp_w2[i].astype(BF16), tm_post)
        xs = _ple(xs, p[i].reshape(batch * seq, -1), ple_w_proj[i].astype(BF16), row1(ple_norm_g[i]),
                  ple_w_gate[i].astype(BF16), tm_ple)
    return xs.reshape(batch, seq, D_MODEL)
```

```python
import functools

import jax
import jax.numpy as jnp
from jax import lax
from jax.experimental import pallas as pl
from jax.experimental.pallas import tpu as pltpu

F32 = jnp.float32
BF16 = jnp.bfloat16

D_MODEL = 1024
CHUNK = 64
NORM_EPS = 1e-6
L2_EPS = 1e-6
A_WIDTH = 512
A_HEAD = 64
A_GN_EPS = 64e-5
A_LORA_PAD = 128
A_GROUP = 256
B_WIDTH = 512
B_HEADS = 4
B_HEAD = 128
B_CONV = 4
C_HEADS = 16
C_HEAD = 64
C_WINDOW = 8 * CHUNK
C_BAND = C_WINDOW + CHUNK
C_QTILE = 4 * CHUNK
C_MAX_REL = 256
D_FF = 4096
FF_CHUNK = 1024
LANES = 128
SUBLANES = 8
INV_BLOCK = 16
MASK_VALUE = -1e30
VMEM_LIMIT = 56 * 1024 * 1024


def _dot(a, b):
    return jnp.dot(a.astype(BF16), b.astype(BF16), preferred_element_type=F32)


def _dot_nt(a, b):
    return lax.dot_general(a.astype(BF16), b.astype(BF16), (((1,), (1,)), ((), ())),
                           preferred_element_type=F32)


def _dot_tn(a, b):
    return jnp.dot(a.astype(BF16).T, b.astype(BF16), preferred_element_type=F32)


def _rms(x, g, eps=NORM_EPS):
    return x * lax.rsqrt(jnp.mean(x * x, axis=-1, keepdims=True) + eps) * g


def _sigmoid(z):
    return 1.0 / (1.0 + jnp.exp(-z))


def _softplus(z):
    return jnp.maximum(z, 0.0) + jnp.log(1.0 + jnp.exp(-jnp.abs(z)))


def _shift_rows(x, prev_row):
    row = lax.broadcasted_iota(jnp.int32, x.shape, 0)
    return jnp.where(row == 0, prev_row, pltpu.roll(x, 1, 0))


def _cumsum_rows(x):
    row = lax.broadcasted_iota(jnp.int32, x.shape, 0)
    step = 1
    while step < x.shape[0]:
        x = x + jnp.where(row >= step, pltpu.roll(x, step, 0), 0.0)
        step *= 2
    return x


def _stack_heads(x, n_heads):
    head_w = x.shape[1] // n_heads
    lane_head = lax.broadcasted_iota(jnp.int32, x.shape, 1) >> (head_w.bit_length() - 1)
    zero = jnp.zeros_like(x)
    return jnp.concatenate([jnp.where(lane_head == h, x, zero) for h in range(n_heads)], axis=0)


def _unstack_heads(xs, n_heads):
    out = xs[0:CHUNK]
    for h in range(1, n_heads):
        out = out + xs[h * CHUNK:(h + 1) * CHUNK]
    return out


def _block_masks(n_heads):
    n = n_heads * CHUNK
    r = lax.broadcasted_iota(jnp.int32, (n, n), 0)
    c = lax.broadcasted_iota(jnp.int32, (n, n), 1)
    same_head = (r ^ c) < CHUNK
    delta = jnp.where(same_head, r - c, -1)
    return same_head, delta >= 0, delta > 0, (r ^ c) < INV_BLOCK


def _unit_lower_inverse(n_mat, same_block):
    nd = jnp.where(same_block, n_mat, 0.0)
    no = n_mat - nd
    td = nd
    pw = nd
    for _ in range(3):
        pw = _dot(pw, pw)
        td = td + pw + _dot(td, pw)
    e = no + _dot(td, no)
    e2 = _dot(e, e)
    f = e + e2 + _dot(e, e2)
    return f + td + _dot(f, td)


def _seg_sum(x, seg_lo):
    outs = []
    for j in range(x.shape[1] // LANES):
        blk = x[:, j * LANES:(j + 1) * LANES]
        s_lo = jnp.sum(jnp.where(seg_lo, blk, 0.0), axis=-1, keepdims=True)
        s_hi = jnp.sum(jnp.where(seg_lo, 0.0, blk), axis=-1, keepdims=True)
        outs.append(jnp.where(seg_lo, s_lo, s_hi))
    return jnp.concatenate(outs, axis=1)


def _even_in_kernel(x_ref, halo_ref, g_ref, win_ref, wtail_ref, mu_ref, l1_ref, l2_ref, lb_ref,
                    rkv_ref, qkvb_ref, gate_ref, ba_ref, lw_ref, a_ref, gg_ref, *vg_ref,
                    tiles_per_seq):
    i = pl.program_id(0)
    g = g_ref[...]
    h = _rms(x_ref[...], g)
    prev = _rms(halo_ref[...], g)[SUBLANES - 1:SUBLANES, :]
    prev = jnp.where(i % tiles_per_seq == 0, 0.0, prev)
    dh = _shift_rows(h, prev) - h
    hb = h.astype(BF16)
    rkv_ref[...] = _dot(hb, win_ref[:, 0:3 * A_WIDTH])
    qkvb_ref[...] = _dot(hb, win_ref[:, 3 * A_WIDTH:3 * A_WIDTH + 3 * B_WIDTH])
    gate_ref[...] = _dot(hb, win_ref[:, 3 * A_WIDTH + 3 * B_WIDTH:])
    ba_ref[...] = _dot(hb, wtail_ref[...])

    def lora_in(j):
        return _dot(h + dh * mu_ref[j:j + 1, :], l1_ref[j])

    w_raw = lb_ref[0:1, :] + _dot(jnp.tanh(lora_in(0)), l2_ref[0])
    w_log = -_softplus(-w_raw) - 0.5
    lw_ref[...] = -jnp.exp(w_log)
    a_ref[...] = _sigmoid(lb_ref[1:2, :] + _dot(lora_in(1), l2_ref[1]))
    gg_ref[...] = _dot(_sigmoid(lora_in(2)), l2_ref[2])
    if vg_ref:
        vg_ref[0][...] = _sigmoid(lb_ref[2:3, :] + _dot(lora_in(3), l2_ref[3]))


def _even_in(x2d, seq, g, win, wtail, mu, l1, l2, lb, has_vres, tm):
    m = x2d.shape[0]
    n_out = 8 if has_vres else 7
    widths = [3 * A_WIDTH, 3 * B_WIDTH, B_WIDTH, LANES, A_WIDTH, A_WIDTH, A_WIDTH, A_WIDTH][:n_out]
    const = lambda a: pl.BlockSpec(a.shape, lambda i: (0,) * a.ndim)
    return pl.pallas_call(
        functools.partial(_even_in_kernel, tiles_per_seq=seq // tm),
        grid=(m // tm,),
        in_specs=[pl.BlockSpec((tm, D_MODEL), lambda i: (i, 0)),
                  pl.BlockSpec((SUBLANES, D_MODEL), lambda i: (jnp.maximum(i * (tm // SUBLANES) - 1, 0), 0)),
                  const(g), const(win), const(wtail), const(mu), const(l1), const(l2), const(lb)],
        out_specs=[pl.BlockSpec((tm, w), lambda i: (i, 0)) for w in widths],
        out_shape=[jax.ShapeDtypeStruct((m, w), F32) for w in widths],
        compiler_params=pltpu.CompilerParams(dimension_semantics=("parallel",), vmem_limit_bytes=VMEM_LIMIT),
        name="even_in",
    )(x2d, x2d, g, win, wtail, mu, l1, l2, lb)


def _rwkv_kernel(rkv_ref, lw_ref, a_ref, gg_ref, *rest, has_vres):
    if has_vres:
        vg_ref, vfirst_ref, par_ref, mu_ref, y_ref, state_ref, tail_ref = rest
    else:
        par_ref, mu_ref, y_ref, vfirst_out_ref, state_ref, tail_ref = rest
    c = pl.program_id(1)

    @pl.when(c == 0)
    def _():
        state_ref[...] = jnp.zeros_like(state_ref)
        tail_ref[...] = jnp.zeros_like(tail_ref)

    rkv = rkv_ref[...]
    rkv_prev = _shift_rows(rkv, tail_ref[SUBLANES - 1:SUBLANES, :])
    tail_ref[...] = rkv[CHUNK - SUBLANES:, :]
    rkv = rkv + (rkv_prev - rkv) * mu_ref[...]
    r = rkv[:, 0:A_WIDTH]
    k = rkv[:, A_WIDTH:2 * A_WIDTH]
    v = rkv[:, 2 * A_WIDTH:]
    k_k, k_a, r_k, ln_g, ln_b = (par_ref[j:j + 1, :] for j in range(5))
    a = a_ref[...]
    if has_vres:
        v = v + (vfirst_ref[...] - v) * vg_ref[...]
    else:
        vfirst_out_ref[...] = v

    seg_lo = lax.broadcasted_iota(jnp.int32, (CHUNK, LANES), 1) < A_HEAD
    lw = lw_ref[...]
    cw = _cumsum_rows(lw)
    cw_last = cw[CHUNK - 1:CHUNK, :]
    kk = k * k_k
    kk = kk * lax.rsqrt(_seg_sum(kk * kk, seg_lo) + L2_EPS)
    k2 = k * (1.0 + (a - 1.0) * k_a)
    b = kk * a
    e_neg = jnp.exp(-cw)
    e_last = jnp.exp(cw_last - cw)
    r_t = (r * jnp.exp(cw)).astype(BF16)
    a_t = (-kk * jnp.exp(cw - lw)).astype(BF16)
    b_t = (b * e_neg).astype(BF16)
    k_t = (k2 * e_neg).astype(BF16)
    b_hat = (b * e_last).astype(BF16)
    k_hat = (k2 * e_last).astype(BF16)
    bonus = _seg_sum(r * k2 * r_k, seg_lo) * v
    v_bf = v.astype(BF16)

    heads = A_GROUP // A_HEAD
    n_rows = heads * CHUNK
    same_head, incl, strict, same_block = _block_masks(heads)
    incl2 = jnp.concatenate([incl, incl], axis=1)
    for g in range(A_WIDTH // A_GROUP):
        sl = slice(g * A_GROUP, (g + 1) * A_GROUP)
        s_vk = state_ref[g]
        lhs = jnp.concatenate([_stack_heads(a_t[:, sl], heads), _stack_heads(r_t[:, sl], heads)], axis=0)
        rhs = jnp.concatenate([b_t[:, sl]] * heads + [k_t[:, sl]] * heads, axis=0)
        gram = _dot_nt(lhs, rhs)
        minv = _unit_lower_inverse(jnp.where(strict, gram[0:n_rows, 0:n_rows], 0.0), same_block)
        a_ak = jnp.where(strict, gram[0:n_rows, n_rows:], 0.0)
        g_cat = jnp.where(incl2, gram[n_rows:, :], 0.0)
        from_state = _dot_nt(jnp.concatenate([a_t[:, sl], r_t[:, sl]], axis=0), s_vk)
        v_s = _stack_heads(v_bf[:, sl], heads)
        x = from_state[0:CHUNK] + _unstack_heads(_dot(a_ak, v_s), heads)
        u = x + _unstack_heads(_dot(minv, _stack_heads(x.astype(BF16), heads)), heads)
        u_bf = u.astype(BF16)
        y_s = _dot(g_cat, jnp.concatenate([_stack_heads(u_bf, heads), v_s], axis=0))
        y = from_state[CHUNK:] + _unstack_heads(y_s, heads)
        upd = _dot_tn(jnp.concatenate([u_bf, v_bf[:, sl]], axis=0),
                      jnp.concatenate([b_hat[:, sl], k_hat[:, sl]], axis=0))
        state_ref[g] = s_vk * jnp.exp(cw_last[:, sl]) + jnp.where(same_head, upd, 0.0)

        mean = _seg_sum(y, seg_lo) * (1.0 / A_HEAD)
        yc = y - mean
        var = _seg_sum(yc * yc, seg_lo) * (1.0 / A_HEAD)
        yn = yc * lax.rsqrt(var + A_GN_EPS) * ln_g[:, sl] + ln_b[:, sl]
        y_ref[:, sl] = ((yn + bonus[:, sl]) * gg_ref[:, sl]).astype(y_ref.dtype)


def _rwkv(rkv, lw, a, gg, vg, vfirst, par, mu, batch, seq):
    m = rkv.shape[0]
    nc = seq // CHUNK
    has_vres = vg is not None
    row = lambda w: pl.BlockSpec((CHUNK, w), lambda b, c: (b * nc + c, 0))
    const = lambda arr: pl.BlockSpec(arr.shape, lambda b, c: (0,) * arr.ndim)
    ins = [rkv, lw, a, gg] + ([vg, vfirst] if has_vres else []) + [par, mu]
    in_specs = [row(3 * A_WIDTH)] + [row(A_WIDTH)] * (5 if has_vres else 3) + [const(par), const(mu)]
    n_out = 1 if has_vres else 2
    outs = pl.pallas_call(
        functools.partial(_rwkv_kernel, has_vres=has_vres),
        grid=(batch, nc),
        in_specs=in_specs,
        out_specs=[row(A_WIDTH)] * n_out,
        out_shape=[jax.ShapeDtypeStruct((m, A_WIDTH), dt) for dt in (BF16, F32)[:n_out]],
        scratch_shapes=[pltpu.VMEM((A_WIDTH // A_GROUP, A_GROUP, A_GROUP), F32),
                        pltpu.VMEM((SUBLANES, 3 * A_WIDTH), F32)],
        compiler_params=pltpu.CompilerParams(dimension_semantics=("arbitrary", "arbitrary"),
                                             vmem_limit_bytes=VMEM_LIMIT),
        name="rwkv7_scan",
    )(*ins)
    return outs if not has_vres else (outs[0], vfirst)


def _gdn_kernel(qkv_ref, gate_ref, ba_ref, conv_ref, hp_ref, ng_ref, y_ref, state_ref, zz_ref):
    c = pl.program_id(1)

    @pl.when(c == 0)
    def _():
        state_ref[...] = jnp.zeros_like(state_ref)
        zz_ref[0:SUBLANES, :] = jnp.zeros((SUBLANES, 3 * B_WIDTH), F32)

    z = qkv_ref[...]
    zz_ref[SUBLANES:, :] = z
    zz = zz_ref[...]
    zz_ref[0:SUBLANES, :] = z[CHUNK - SUBLANES:, :]
    conv = zz[SUBLANES:, :] * conv_ref[B_CONV - 1:B_CONV, :]
    for j in range(B_CONV - 1):
        conv = conv + pltpu.roll(zz, B_CONV - 1 - j, 0)[SUBLANES:, :] * conv_ref[j:j + 1, :]
    qkv = conv * _sigmoid(conv)

    ba = ba_ref[...]
    beta = _sigmoid(ba)
    g_step = -jnp.exp(hp_ref[0:1, :]) * _softplus(ba + hp_ref[1:2, :])
    gc = _cumsum_rows(g_step)
    gc_t = gc.T

    def per_head(tile, lane0):
        return jnp.concatenate([jnp.broadcast_to(tile[:, lane0 + h:lane0 + h + 1], (CHUNK, B_HEAD))
                                for h in range(B_HEADS)], axis=1)

    def l2n(z):
        return jnp.concatenate(
            [z[:, h * B_HEAD:(h + 1) * B_HEAD]
             * lax.rsqrt(jnp.sum(jnp.square(z[:, h * B_HEAD:(h + 1) * B_HEAD]), axis=-1, keepdims=True) + L2_EPS)
             for h in range(B_HEADS)], axis=1)

    q = l2n(qkv[:, 0:B_WIDTH]) * (B_HEAD ** -0.5)
    k = l2n(qkv[:, B_WIDTH:2 * B_WIDTH])
    v = qkv[:, 2 * B_WIDTH:]
    beta_f = per_head(beta, 0)
    g_col = per_head(gc, B_HEADS)
    g_last = g_col[CHUNK - 1:CHUNK, :]
    e_g = jnp.exp(g_col)
    kb = k * beta_f
    k_bf = k.astype(BF16)

    same_head, incl, strict, same_block = _block_masks(B_HEADS)
    n_rows = B_HEADS * CHUNK
    g_col_s = jnp.concatenate([gc[:, B_HEADS + h:B_HEADS + h + 1] for h in range(B_HEADS)], axis=0)
    g_row_s = jnp.concatenate([gc_t[B_HEADS + h:B_HEADS + h + 1, :] for h in range(B_HEADS)], axis=1)
    decay = jnp.where(incl, jnp.exp(jnp.where(incl, g_col_s - g_row_s, 0.0)), 0.0)
    lhs = jnp.concatenate([_stack_heads(kb.astype(BF16), B_HEADS), _stack_heads(q.astype(BF16), B_HEADS)], axis=0)
    gram = _dot_nt(lhs, jnp.concatenate([k_bf] * B_HEADS, axis=0))
    amat = jnp.where(strict, gram[0:n_rows] * decay, 0.0)
    qk = gram[n_rows:] * decay
    tinv = _unit_lower_inverse(-amat, same_block)
    vb = v * beta_f
    kbg = kb * e_g
    solved = _dot(tinv, jnp.concatenate([_stack_heads(vb.astype(BF16), B_HEADS),
                                         _stack_heads(kbg.astype(BF16), B_HEADS)], axis=1))
    u = vb + _unstack_heads(solved[:, 0:B_WIDTH], B_HEADS)
    w = kbg + _unstack_heads(solved[:, B_WIDTH:], B_HEADS)
    wq = jnp.concatenate([w, q * e_g], axis=0).astype(BF16)
    kd = (k * jnp.exp(g_last - g_col)).astype(BF16)
    v_new, q_state = [], []
    for hd in range(B_HEADS):
        sl = slice(hd * B_HEAD, (hd + 1) * B_HEAD)
        s_kv = state_ref[hd]
        from_state = _dot(wq[:, sl], s_kv)
        vn = u[:, sl] - from_state[0:CHUNK]
        state_ref[hd] = s_kv * jnp.exp(g_last[:, sl]) + _dot_tn(kd[:, sl], vn)
        v_new.append(vn)
        q_state.append(from_state[CHUNK:])
    v_new = jnp.concatenate(v_new, axis=1)
    o = jnp.concatenate(q_state, axis=1) + _unstack_heads(_dot(qk, _stack_heads(v_new.astype(BF16), B_HEADS)), B_HEADS)
    for hd in range(B_HEADS):
        sl = slice(hd * B_HEAD, (hd + 1) * B_HEAD)
        gate = gate_ref[:, sl]
        y_ref[:, sl] = (_rms(o[:, sl], ng_ref[...]) * (gate * _sigmoid(gate))).astype(y_ref.dtype)


def _gdn(qkvb, gate, ba, conv_w, hp, norm_g, batch, seq):
    m = qkvb.shape[0]
    nc = seq // CHUNK
    row = lambda w: pl.BlockSpec((CHUNK, w), lambda b, c: (b * nc + c, 0))
    const = lambda arr: pl.BlockSpec(arr.shape, lambda b, c: (0,) * arr.ndim)
    return pl.pallas_call(
        _gdn_kernel,
        grid=(batch, nc),
        in_specs=[row(3 * B_WIDTH), row(B_WIDTH), row(LANES), const(conv_w), const(hp), const(norm_g)],
        out_specs=row(B_WIDTH),
        out_shape=jax.ShapeDtypeStruct((m, B_WIDTH), BF16),
        scratch_shapes=[pltpu.VMEM((B_HEADS, B_HEAD, B_HEAD), F32),
                        pltpu.VMEM((CHUNK + SUBLANES, 3 * B_WIDTH), F32)],
        compiler_params=pltpu.CompilerParams(dimension_semantics=("arbitrary", "arbitrary"),
                                             vmem_limit_bytes=VMEM_LIMIT),
        name="gdn_scan",
    )(qkvb, gate, ba, conv_w, hp, norm_g)


def _head_rms(x, ones_bd, g):
    sq = x * x
    hi = sq.astype(BF16)
    lo = (sq - hi.astype(F32)).astype(BF16)
    ssum = (jnp.dot(hi, ones_bd, preferred_element_type=F32)
            + jnp.dot(lo, ones_bd, preferred_element_type=F32))
    return x * lax.rsqrt(ssum * (1.0 / C_HEAD) + NORM_EPS) * g


def _attn_qkv_kernel(x_ref, g_ref, w_ref, qg_ref, kg_ref, ones_ref, o_ref):
    t = pl.program_id(1)

    @pl.when(t == 0)
    def _():
        o_ref[...] = jnp.zeros_like(o_ref)

    @pl.when(t > 0)
    def _():
        hb = _rms(x_ref[...], g_ref[...]).astype(BF16)
        ones_bd = ones_ref[...]
        blk = ones_bd.shape[0]
        for j in range(D_MODEL // blk):
            sl = slice(j * blk, (j + 1) * blk)
            q = _dot(hb, w_ref[:, sl])
            o_ref[:, sl] = (_head_rms(q, ones_bd, qg_ref[...]) * (C_HEAD ** -0.5)).astype(o_ref.dtype)
            ksl = slice(D_MODEL + j * blk, D_MODEL + (j + 1) * blk)
            k = _dot(hb, w_ref[:, ksl])
            o_ref[:, ksl] = _head_rms(k, ones_bd, kg_ref[...]).astype(o_ref.dtype)
        o_ref[:, 2 * D_MODEL:] = _dot(hb, w_ref[:, 2 * D_MODEL:]).astype(o_ref.dtype)


def _attn_qkv(x2d, g, w, qg, kg, ones_bd, batch, seq, tm):
    tiles = seq // tm
    pad_tiles = C_WINDOW // tm
    const = lambda arr: pl.BlockSpec(arr.shape, lambda b, t: (0,) * arr.ndim)
    return pl.pallas_call(
        _attn_qkv_kernel,
        grid=(batch, tiles + pad_tiles),
        in_specs=[pl.BlockSpec((tm, D_MODEL), lambda b, t: (b * tiles + jnp.maximum(t - pad_tiles, 0), 0)),
                  const(g), const(w), const(qg), const(kg), const(ones_bd)],
        out_specs=pl.BlockSpec((tm, 3 * D_MODEL), lambda b, t: (b * (tiles + pad_tiles) + t, 0)),
        out_shape=jax.ShapeDtypeStruct((batch * (seq + C_WINDOW), 3 * D_MODEL), BF16),
        compiler_params=pltpu.CompilerParams(dimension_semantics=("parallel", "arbitrary"),
                                             vmem_limit_bytes=VMEM_LIMIT),
        name="attn_qkv",
    )(x2d, g, w, qg, kg, ones_bd)


def _attn_kernel(q_ref, k_ref, v_ref, bias_ref, o_ref):
    t = pl.program_id(2)
    start = pl.multiple_of(t * C_QTILE, C_QTILE)
    kw = k_ref[pl.ds(start, C_WINDOW + C_QTILE), :]
    vw = v_ref[pl.ds(start, C_WINDOW + C_QTILE), :]
    q = q_ref[...]
    lane_lo = lax.broadcasted_iota(jnp.int32, (C_QTILE, LANES), 1) < C_HEAD
    key_idx = lax.broadcasted_iota(jnp.int32, (C_QTILE, C_WINDOW + C_QTILE), 1)
    valid = key_idx >= C_WINDOW - t * C_QTILE
    outs = []
    for head in range(2):
        keep = lane_lo if head == 0 else jnp.logical_not(lane_lo)
        s = _dot_nt(jnp.where(keep, q, jnp.zeros_like(q)), kw) + bias_ref[head]
        s = jnp.where(valid, s, MASK_VALUE)
        p = jnp.exp(s - jnp.max(s, axis=-1, keepdims=True))
        denom = jnp.sum(p, axis=-1, keepdims=True)
        outs.append(_dot(p, vw) / denom)
    o_ref[...] = jnp.where(lane_lo, outs[0], outs[1]).astype(o_ref.dtype)


def _attn(qkv_pad, bias, batch, seq):
    nt = seq // C_QTILE
    pad_rows = seq + C_WINDOW
    n_pairs = D_MODEL // LANES
    q_off = C_WINDOW // C_QTILE
    return pl.pallas_call(
        _attn_kernel,
        grid=(batch, n_pairs, nt),
        in_specs=[pl.BlockSpec((C_QTILE, LANES), lambda b, h, t: (b * (pad_rows // C_QTILE) + q_off + t, h)),
                  pl.BlockSpec((pad_rows, LANES), lambda b, h, t: (b, n_pairs + h)),
                  pl.BlockSpec((pad_rows, LANES), lambda b, h, t: (b, 2 * n_pairs + h)),
                  pl.BlockSpec((2, C_QTILE, C_WINDOW + C_QTILE), lambda b, h, t: (h, 0, 0))],
        out_specs=pl.BlockSpec((C_QTILE, LANES), lambda b, h, t: (b * nt + t, h)),
        out_shape=jax.ShapeDtypeStruct((batch * seq, D_MODEL), BF16),
        compiler_params=pltpu.CompilerParams(dimension_semantics=("parallel", "parallel", "arbitrary"),
                                             vmem_limit_bytes=VMEM_LIMIT),
        name="band_attn",
    )(qkv_pad, qkv_pad, qkv_pad, bias)


def _post_kernel(x_ref, *rest, n_mix):
    mix_refs = rest[:n_mix]
    wo_ref, g_ref, w1_ref, w2_ref, o_ref = rest[n_mix:]
    mix = jnp.concatenate([mref[...].astype(BF16) for mref in mix_refs], axis=1)
    x = x_ref[...] + _dot(mix, wo_ref[...])
    hb = _rms(x, g_ref[...]).astype(BF16)
    acc = None
    for j in range(D_FF // FF_CHUNK):
        sl = slice(j * FF_CHUNK, (j + 1) * FF_CHUNK)
        hid = jnp.maximum(_dot(hb, w1_ref[:, sl]), 0.0)
        part = _dot(hid * hid, w2_ref[sl, :])
        acc = part if acc is None else acc + part
    o_ref[...] = x + acc


def _post(x2d, mixes, wo, g, w1, w2, tm):
    m = x2d.shape[0]
    const = lambda arr: pl.BlockSpec(arr.shape, lambda i: (0,) * arr.ndim)
    row = lambda w: pl.BlockSpec((tm, w), lambda i: (i, 0))
    return pl.pallas_call(
        functools.partial(_post_kernel, n_mix=len(mixes)),
        grid=(m // tm,),
        in_specs=[row(D_MODEL)] + [row(mx.shape[1]) for mx in mixes] + [const(wo), const(g), const(w1), const(w2)],
        out_specs=row(D_MODEL),
        out_shape=jax.ShapeDtypeStruct((m, D_MODEL), F32),
        compiler_params=pltpu.CompilerParams(dimension_semantics=("parallel",), vmem_limit_bytes=VMEM_LIMIT),
        name="mix_out_mlp",
    )(x2d, *mixes, wo, g, w1, w2)


def _ple_kernel(x_ref, p_ref, wp_ref, g_ref, wg_ref, o_ref):
    x = x_ref[...]
    emb = _rms(_dot(p_ref[...], wp_ref[...]), g_ref[...])
    o_ref[...] = x + emb * _sigmoid(_dot(x, wg_ref[...]))


def _ple(x2d, p2d, wp, g, wg, tm):
    m = x2d.shape[0]
    const = lambda arr: pl.BlockSpec(arr.shape, lambda i: (0,) * arr.ndim)
    row = lambda w: pl.BlockSpec((tm, w), lambda i: (i, 0))
    return pl.pallas_call(
        _ple_kernel,
        grid=(m // tm,),
        in_specs=[row(D_MODEL), row(p2d.shape[1]), const(wp), const(g), const(wg)],
        out_specs=row(D_MODEL),
        out_shape=jax.ShapeDtypeStruct((m, D_MODEL), F32),
        compiler_params=pltpu.CompilerParams(dimension_semantics=("parallel",), vmem_limit_bytes=VMEM_LIMIT),
        name="ple_gate",
    )(x2d, p2d, wp, g, wg)


def _pad_to(w, rows=None, cols=None):
    r = (rows or w.shape[0]) - w.shape[0]
    c = (cols or w.shape[1]) - w.shape[1]
    return jnp.pad(w, ((0, r), (0, c)))


def _rel_bias_table(rel_bias):
    n_heads = rel_bias.shape[0]
    span = CHUNK + C_BAND - 1
    rel = (C_BAND - 1) - jnp.arange(span)
    f = rel_bias[:, jnp.clip(rel, -C_MAX_REL, C_MAX_REL) + C_MAX_REL]
    g = jnp.tile(jnp.pad(f, ((0, 0), (0, 1))), (1, CHUNK))[:, :CHUNK * span].reshape(n_heads, CHUNK, span)
    band = g[:, :, CHUNK - 1:CHUNK - 1 + C_BAND]
    rows = [jnp.pad(band, ((0, 0), (0, 0), (cc * CHUNK, C_QTILE - CHUNK - cc * CHUNK)), constant_values=MASK_VALUE)
            for cc in range(C_QTILE // CHUNK)]
    return jnp.concatenate(rows, axis=1)


def kernel(x, p, norm_mix_g, norm_ffn_g, even_w_in, rwkv_mu_proj, rwkv_mu_lora, rwkv_w0, rwkv_w1, rwkv_w2, rwkv_a0, rwkv_a1, rwkv_a2, rwkv_g1, rwkv_g2, rwkv_k_k, rwkv_k_a, rwkv_r_k, rwkv_ln_g, rwkv_ln_b, rwkv_v_mu, rwkv_v0, rwkv_v1, rwkv_v2, gdn_conv_w, gdn_a_log, gdn_dt_bias, gdn_norm_g, even_w_out, attn_w_qkv, attn_q_g, attn_k_g, attn_rel_bias, attn_w_out, mlp_w1, mlp_w2, ple_w_proj, ple_norm_g, ple_w_gate):
    batch, seq, _ = x.shape
    depth = p.shape[0]
    assert seq % 512 == 0 and x.shape[2] == D_MODEL
    tm_in, tm_post, tm_ple, tm_qkv = 256, 256, 512, 512
    xs = x.reshape(batch * seq, D_MODEL)
    row1 = lambda vec: vec.reshape(1, -1)
    main_cols = 3 * A_WIDTH + 4 * B_WIDTH

    blk = 4 * C_HEAD
    ones_bd = (jnp.arange(blk)[:, None] // C_HEAD == jnp.arange(blk)[None, :] // C_HEAD).astype(BF16)

    v_first = None
    for i in range(depth):
        if i % 2 == 0:
            e = i // 2
            has_vres = e > 0
            win = even_w_in[e]
            mus = [rwkv_mu_lora[e, 0], rwkv_mu_lora[e, 1], rwkv_mu_lora[e, 2]]
            l1s = [rwkv_w1[e], rwkv_a1[e], rwkv_g1[e]]
            l2s = [rwkv_w2[e], rwkv_a2[e], rwkv_g2[e]]
            lbs = [rwkv_w0[e], rwkv_a0[e]]
            if has_vres:
                mus.append(rwkv_v_mu[e - 1])
                l1s.append(rwkv_v1[e - 1])
                l2s.append(rwkv_v2[e - 1])
                lbs.append(rwkv_v0[e - 1])
            l1 = jnp.stack([_pad_to(w, cols=A_LORA_PAD) for w in l1s]).astype(BF16)
            l2 = jnp.stack([_pad_to(w, rows=A_LORA_PAD) for w in l2s]).astype(BF16)
            outs = _even_in(xs, seq, row1(norm_mix_g[i]), win[:, :main_cols].astype(BF16),
                            _pad_to(win[:, main_cols:], cols=LANES).astype(BF16),
                            jnp.stack(mus), l1, l2, jnp.stack(lbs), has_vres, tm_in)
            rkv, qkvb, gate, ba, lw, a_lr, gg = outs[:7]
            par = jnp.stack([rwkv_k_k[e], rwkv_k_a[e], rwkv_r_k[e].reshape(-1), rwkv_ln_g[e], rwkv_ln_b[e]])
            y_a, v_first = _rwkv(rkv, lw, a_lr, gg, outs[7] if has_vres else None, v_first, par,
                                 rwkv_mu_proj[e].reshape(1, -1), batch, seq)
            hp = jnp.stack([_pad_to(jnp.pad(row1(gdn_a_log[e]), ((0, 0), (B_HEADS, 0))), cols=LANES)[0],
                            _pad_to(jnp.pad(row1(gdn_dt_bias[e]), ((0, 0), (B_HEADS, 0))), cols=LANES)[0]])
            y_b = _gdn(qkvb, gate, ba, gdn_conv_w[e], hp, row1(gdn_norm_g[e]), batch, seq)
            mixes, wo = [y_a, y_b], even_w_out[e]
        else:
            o = i // 2
            tile4 = lambda gvec: jnp.tile(gvec, blk // C_HEAD).reshape(1, blk)
            qkv_pad = _attn_qkv(xs, row1(norm_mix_g[i]), attn_w_qkv[o].astype(BF16), tile4(attn_q_g[o]),
                                tile4(attn_k_g[o]), ones_bd, batch, seq, tm_qkv)
            mixes, wo = [_attn(qkv_pad, _rel_bias_table(attn_rel_bias[o]), batch, seq)], attn_w_out[o]
        xs = _post(xs, mixes, wo.astype(BF16), row1(norm_ffn_g[i]), mlp_w1[i].astype(BF16),
                   mlp_w2[i].astype(BF16), tm_post)
        xs = _ple(xs, p[i].reshape(batch * seq, -1), ple_w_proj[i].astype(BF16), row1(ple_norm_g[i]),
                  ple_w_gate[i].astype(BF16), tm_ple)
    return xs.reshape(batch, seq, D_MODEL)
```

```python
import functools

import jax
import jax.numpy as jnp
from jax import lax
from jax.experimental import pallas as pl
from jax.experimental.pallas import tpu as pltpu

F32 = jnp.float32
BF16 = jnp.bfloat16

D_MODEL = 1024
CHUNK = 64
NORM_EPS = 1e-6
L2_EPS = 1e-6
A_WIDTH = 512
A_HEAD = 64
A_GN_EPS = 64e-5
A_LORA_PAD = 128
A_GROUP = 256
A_SCAN_ROWS = 4 * CHUNK
B_SCAN_ROWS = 4 * CHUNK
B_WIDTH = 512
B_HEADS = 4
B_HEAD = 128
B_CONV = 4
C_HEADS = 16
C_HEAD = 64
C_WINDOW = 8 * CHUNK
C_BAND = C_WINDOW + CHUNK
C_QTILE = 4 * CHUNK
C_MAX_REL = 256
D_FF = 4096
FF_CHUNK = 1024
LANES = 128
SUBLANES = 8
INV_BLOCK = 16
MASK_VALUE = -1e30
VMEM_LIMIT = 56 * 1024 * 1024


def _dot(a, b):
    return jnp.dot(a.astype(BF16), b.astype(BF16), preferred_element_type=F32)


def _dot_nt(a, b):
    return lax.dot_general(a.astype(BF16), b.astype(BF16), (((1,), (1,)), ((), ())),
                           preferred_element_type=F32)


def _dot_tn(a, b):
    return jnp.dot(a.astype(BF16).T, b.astype(BF16), preferred_element_type=F32)


def _rms(x, g, eps=NORM_EPS):
    return x * lax.rsqrt(jnp.mean(x * x, axis=-1, keepdims=True) + eps) * g


def _sigmoid(z):
    return 1.0 / (1.0 + jnp.exp(-z))


def _softplus(z):
    return jnp.maximum(z, 0.0) + jnp.log(1.0 + jnp.exp(-jnp.abs(z)))


def _shift_rows(x, prev_row):
    row = lax.broadcasted_iota(jnp.int32, x.shape, 0)
    return jnp.where(row == 0, prev_row, pltpu.roll(x, 1, 0))


def _cumsum_rows(x):
    row = lax.broadcasted_iota(jnp.int32, x.shape, 0)
    step = 1
    while step < x.shape[0]:
        x = x + jnp.where(row >= step, pltpu.roll(x, step, 0), 0.0)
        step *= 2
    return x


def _stack_heads(x, n_heads):
    head_w = x.shape[1] // n_heads
    lane_head = lax.broadcasted_iota(jnp.int32, x.shape, 1) >> (head_w.bit_length() - 1)
    zero = jnp.zeros_like(x)
    return jnp.concatenate([jnp.where(lane_head == h, x, zero) for h in range(n_heads)], axis=0)


def _unstack_heads(xs, n_heads):
    out = xs[0:CHUNK]
    for h in range(1, n_heads):
        out = out + xs[h * CHUNK:(h + 1) * CHUNK]
    return out


def _block_masks(n_heads):
    n = n_heads * CHUNK
    r = lax.broadcasted_iota(jnp.int32, (n, n), 0)
    c = lax.broadcasted_iota(jnp.int32, (n, n), 1)
    same_head = (r ^ c) < CHUNK
    delta = jnp.where(same_head, r - c, -1)
    return same_head, delta >= 0, delta > 0, (r ^ c) < INV_BLOCK


def _unit_lower_inverses(n_mats, same_block):
    nds = [jnp.where(same_block, n, 0.0) for n in n_mats]
    nos = [n - nd for n, nd in zip(n_mats, nds)]
    tds, pws = nds, nds
    for _ in range(3):
        pws = [_dot(pw, pw) for pw in pws]
        tds = [td + pw + _dot(td, pw) for td, pw in zip(tds, pws)]
    es = [no + _dot(td, no) for td, no in zip(tds, nos)]
    e2s = [_dot(e, e) for e in es]
    fs = [e + e2 + _dot(e, e2) for e, e2 in zip(es, e2s)]
    return [f + td + _dot(f, td) for f, td in zip(fs, tds)]


def _seg_sum(x):
    seg_lo = lax.broadcasted_iota(jnp.int32, (x.shape[0], LANES), 1) < LANES // 2
    outs = []
    for j in range(x.shape[1] // LANES):
        blk = x[:, j * LANES:(j + 1) * LANES]
        s_lo = jnp.sum(jnp.where(seg_lo, blk, 0.0), axis=-1, keepdims=True)
        s_hi = jnp.sum(jnp.where(seg_lo, 0.0, blk), axis=-1, keepdims=True)
        outs.append(jnp.where(seg_lo, s_lo, s_hi))
    return jnp.concatenate(outs, axis=1)


def _even_in_kernel(x_ref, halo_ref, g_ref, win_ref, wtail_ref, mu_ref, l1_ref, l2_ref, lb_ref,
                    rkv_ref, qkvb_ref, gate_ref, ba_ref, lw_ref, a_ref, gg_ref, *vg_ref,
                    tiles_per_seq):
    i = pl.program_id(0)
    g = g_ref[...]
    h = _rms(x_ref[...], g)
    prev = _rms(halo_ref[...], g)[SUBLANES - 1:SUBLANES, :]
    prev = jnp.where(i % tiles_per_seq == 0, 0.0, prev)
    dh = _shift_rows(h, prev) - h
    hb = h.astype(BF16)
    rkv_ref[...] = _dot(hb, win_ref[:, 0:3 * A_WIDTH])
    qkvb_ref[...] = _dot(hb, win_ref[:, 3 * A_WIDTH:3 * A_WIDTH + 3 * B_WIDTH])
    gate_ref[...] = _dot(hb, win_ref[:, 3 * A_WIDTH + 3 * B_WIDTH:])
    ba_ref[...] = _dot(hb, wtail_ref[...])

    def lora_in(j):
        return _dot(h + dh * mu_ref[j:j + 1, :], l1_ref[j])

    w_raw = lb_ref[0:1, :] + _dot(jnp.tanh(lora_in(0)), l2_ref[0])
    w_log = -_softplus(-w_raw) - 0.5
    lw_ref[...] = -jnp.exp(w_log)
    a_ref[...] = _sigmoid(lb_ref[1:2, :] + _dot(lora_in(1), l2_ref[1]))
    gg_ref[...] = _dot(_sigmoid(lora_in(2)), l2_ref[2])
    if vg_ref:
        vg_ref[0][...] = _sigmoid(lb_ref[2:3, :] + _dot(lora_in(3), l2_ref[3]))


def _even_in(x2d, seq, g, win, wtail, mu, l1, l2, lb, has_vres, tm):
    m = x2d.shape[0]
    n_out = 8 if has_vres else 7
    widths = [3 * A_WIDTH, 3 * B_WIDTH, B_WIDTH, LANES, A_WIDTH, A_WIDTH, A_WIDTH, A_WIDTH][:n_out]
    const = lambda a: pl.BlockSpec(a.shape, lambda i: (0,) * a.ndim)
    return pl.pallas_call(
        functools.partial(_even_in_kernel, tiles_per_seq=seq // tm),
        grid=(m // tm,),
        in_specs=[pl.BlockSpec((tm, D_MODEL), lambda i: (i, 0)),
                  pl.BlockSpec((SUBLANES, D_MODEL), lambda i: (jnp.maximum(i * (tm // SUBLANES) - 1, 0), 0)),
                  const(g), const(win), const(wtail), const(mu), const(l1), const(l2), const(lb)],
        out_specs=[pl.BlockSpec((tm, w), lambda i: (i, 0)) for w in widths],
        out_shape=[jax.ShapeDtypeStruct((m, w), F32) for w in widths],
        compiler_params=pltpu.CompilerParams(dimension_semantics=("parallel",), vmem_limit_bytes=VMEM_LIMIT),
        name="even_in",
    )(x2d, x2d, g, win, wtail, mu, l1, l2, lb)


def _rwkv_kernel(rkv_ref, lw_ref, a_ref, gg_ref, *rest, has_vres):
    if has_vres:
        vg_ref, vfirst_ref, par_ref, mu_ref, y_ref, state_ref, tail_ref = rest
    else:
        par_ref, mu_ref, y_ref, vfirst_out_ref, state_ref, tail_ref = rest
    c = pl.program_id(1)

    @pl.when(c == 0)
    def _():
        state_ref[...] = jnp.zeros_like(state_ref)
        tail_ref[...] = jnp.zeros_like(tail_ref)

    rkv = rkv_ref[...]
    rkv_prev = _shift_rows(rkv, tail_ref[SUBLANES - 1:SUBLANES, :])
    tail_ref[...] = rkv[rkv.shape[0] - SUBLANES:, :]
    rkv = rkv + (rkv_prev - rkv) * mu_ref[...]
    r = rkv[:, 0:A_WIDTH]
    k = rkv[:, A_WIDTH:2 * A_WIDTH]
    v = rkv[:, 2 * A_WIDTH:]
    k_k, k_a, r_k, ln_g, ln_b = (par_ref[j:j + 1, :] for j in range(5))
    a = a_ref[...]
    if has_vres:
        v = v + (vfirst_ref[...] - v) * vg_ref[...]
    else:
        vfirst_out_ref[...] = v

    lw_all = lw_ref[...]
    kk_all = k * k_k
    kk_all = kk_all * lax.rsqrt(_seg_sum(kk_all * kk_all) + L2_EPS)
    k2_all = k * (1.0 + (a - 1.0) * k_a)
    b_all = kk_all * a
    bonus = _seg_sum(r * k2_all * r_k) * v
    v_bf_all = v.astype(BF16)

    heads = A_GROUP // A_HEAD
    n_groups = A_WIDTH // A_GROUP
    n_rows = heads * CHUNK
    same_head, incl, strict, same_block = _block_masks(heads)
    incl2 = jnp.concatenate([incl, incl], axis=1)
    n_chunks = rkv.shape[0] // CHUNK
    group_sl = [slice(g * A_GROUP, (g + 1) * A_GROUP) for g in range(n_groups)]

    probs = []
    for cc in range(n_chunks):
        rows = slice(cc * CHUNK, (cc + 1) * CHUNK)
        lw, kk, k2, b = lw_all[rows], kk_all[rows], k2_all[rows], b_all[rows]
        cw = _cumsum_rows(lw)
        cw_last = cw[CHUNK - 1:CHUNK, :]
        e_neg = jnp.exp(-cw)
        e_last = jnp.exp(cw_last - cw)
        r_t = (r[rows] * jnp.exp(cw)).astype(BF16)
        a_t = (-kk * jnp.exp(cw - lw)).astype(BF16)
        b_t = (b * e_neg).astype(BF16)
        k_t = (k2 * e_neg).astype(BF16)
        b_hat = (b * e_last).astype(BF16)
        k_hat = (k2 * e_last).astype(BF16)
        for sl in group_sl:
            lhs = jnp.concatenate([_stack_heads(a_t[:, sl], heads), _stack_heads(r_t[:, sl], heads)], axis=0)
            rhs = jnp.concatenate([b_t[:, sl]] * heads + [k_t[:, sl]] * heads, axis=0)
            probs.append(dict(
                rows=rows, sl=sl, gram=_dot_nt(lhs, rhs),
                ar=jnp.concatenate([a_t[:, sl], r_t[:, sl]], axis=0),
                bk_hat=jnp.concatenate([b_hat[:, sl], k_hat[:, sl]], axis=0),
                v=v_bf_all[rows, sl], w_last=jnp.exp(cw_last[:, sl])))
    minvs = _unit_lower_inverses([jnp.where(strict, pr["gram"][0:n_rows, 0:n_rows], 0.0) for pr in probs],
                                 same_block)

    states = [state_ref[g] for g in range(n_groups)]
    for cc in range(n_chunks):
        prs = probs[cc * n_groups:(cc + 1) * n_groups]
        mis = minvs[cc * n_groups:(cc + 1) * n_groups]
        from_state = [_dot_nt(pr["ar"], st) for pr, st in zip(prs, states)]
        v_s = [_stack_heads(pr["v"], heads) for pr in prs]
        xs = [fs[0:CHUNK] + _unstack_heads(_dot(jnp.where(strict, pr["gram"][0:n_rows, n_rows:], 0.0), vs), heads)
              for fs, pr, vs in zip(from_state, prs, v_s)]
        us = [x + _unstack_heads(_dot(mi, _stack_heads(x.astype(BF16), heads)), heads) for x, mi in zip(xs, mis)]
        u_bf = [u.astype(BF16) for u in us]
        ys = [fs[CHUNK:] + _unstack_heads(_dot(jnp.where(incl2, pr["gram"][n_rows:, :], 0.0),
                                               jnp.concatenate([_stack_heads(ub, heads), vs], axis=0)), heads)
              for fs, pr, ub, vs in zip(from_state, prs, u_bf, v_s)]
        upds = [_dot_tn(jnp.concatenate([ub, pr["v"]], axis=0), pr["bk_hat"]) for ub, pr in zip(u_bf, prs)]
        states = [st * pr["w_last"] + jnp.where(same_head, upd, 0.0) for st, pr, upd in zip(states, prs, upds)]
        for pr, y in zip(prs, ys):
            rows, sl = pr["rows"], pr["sl"]
            mean = _seg_sum(y) * (1.0 / A_HEAD)
            yc = y - mean
            var = _seg_sum(yc * yc) * (1.0 / A_HEAD)
            yn = yc * lax.rsqrt(var + A_GN_EPS) * ln_g[:, sl] + ln_b[:, sl]
            y_ref[rows, sl] = ((yn + bonus[rows, sl]) * gg_ref[rows, sl]).astype(y_ref.dtype)
    for g in range(n_groups):
        state_ref[g] = states[g]


def _rwkv(rkv, lw, a, gg, vg, vfirst, par, mu, batch, seq):
    m = rkv.shape[0]
    nc = seq // A_SCAN_ROWS
    has_vres = vg is not None
    row = lambda w: pl.BlockSpec((A_SCAN_ROWS, w), lambda b, c: (b * nc + c, 0))
    const = lambda arr: pl.BlockSpec(arr.shape, lambda b, c: (0,) * arr.ndim)
    ins = [rkv, lw, a, gg] + ([vg, vfirst] if has_vres else []) + [par, mu]
    in_specs = [row(3 * A_WIDTH)] + [row(A_WIDTH)] * (5 if has_vres else 3) + [const(par), const(mu)]
    n_out = 1 if has_vres else 2
    outs = pl.pallas_call(
        functools.partial(_rwkv_kernel, has_vres=has_vres),
        grid=(batch, nc),
        in_specs=in_specs,
        out_specs=[row(A_WIDTH)] * n_out,
        out_shape=[jax.ShapeDtypeStruct((m, A_WIDTH), dt) for dt in (BF16, F32)[:n_out]],
        scratch_shapes=[pltpu.VMEM((A_WIDTH // A_GROUP, A_GROUP, A_GROUP), F32),
                        pltpu.VMEM((SUBLANES, 3 * A_WIDTH), F32)],
        compiler_params=pltpu.CompilerParams(dimension_semantics=("arbitrary", "arbitrary"),
                                             vmem_limit_bytes=VMEM_LIMIT),
        name="rwkv7_scan",
    )(*ins)
    return outs if not has_vres else (outs[0], vfirst)


def _gdn_kernel(qkv_ref, gate_ref, ba_ref, conv_ref, hp_ref, ng_ref, y_ref, state_ref, zz_ref):
    c = pl.program_id(1)

    @pl.when(c == 0)
    def _():
        state_ref[...] = jnp.zeros_like(state_ref)
        zz_ref[0:SUBLANES, :] = jnp.zeros((SUBLANES, 3 * B_WIDTH), F32)

    z = qkv_ref[...]
    zz_ref[SUBLANES:, :] = z
    zz = zz_ref[...]
    zz_ref[0:SUBLANES, :] = z[z.shape[0] - SUBLANES:, :]
    conv = zz[SUBLANES:, :] * conv_ref[B_CONV - 1:B_CONV, :]
    for j in range(B_CONV - 1):
        conv = conv + pltpu.roll(zz, B_CONV - 1 - j, 0)[SUBLANES:, :] * conv_ref[j:j + 1, :]
    qkv = conv * _sigmoid(conv)

    def per_head(tile, lane0):
        return jnp.concatenate([jnp.broadcast_to(tile[:, lane0 + h:lane0 + h + 1], (CHUNK, B_HEAD))
                                for h in range(B_HEADS)], axis=1)

    def l2n(t):
        return jnp.concatenate(
            [t[:, h * B_HEAD:(h + 1) * B_HEAD]
             * lax.rsqrt(jnp.sum(jnp.square(t[:, h * B_HEAD:(h + 1) * B_HEAD]), axis=-1, keepdims=True) + L2_EPS)
             for h in range(B_HEADS)], axis=1)

    q_all = l2n(qkv[:, 0:B_WIDTH]) * (B_HEAD ** -0.5)
    k_all = l2n(qkv[:, B_WIDTH:2 * B_WIDTH])
    v_all = qkv[:, 2 * B_WIDTH:]
    ba_all = ba_ref[...]
    beta_all = _sigmoid(ba_all)
    g_step_all = -jnp.exp(hp_ref[0:1, :]) * _softplus(ba_all + hp_ref[1:2, :])

    same_head, incl, strict, same_block = _block_masks(B_HEADS)
    n_rows = B_HEADS * CHUNK
    probs = []
    for cc in range(z.shape[0] // CHUNK):
        rows = slice(cc * CHUNK, (cc + 1) * CHUNK)
        q, k, v = q_all[rows], k_all[rows], v_all[rows]
        gc = _cumsum_rows(g_step_all[rows])
        gc_t = gc.T
        beta_f = per_head(beta_all[rows], 0)
        g_col = per_head(gc, B_HEADS)
        g_last = g_col[CHUNK - 1:CHUNK, :]
        e_g = jnp.exp(g_col)
        kb = k * beta_f
        g_col_s = jnp.concatenate([gc[:, B_HEADS + h:B_HEADS + h + 1] for h in range(B_HEADS)], axis=0)
        g_row_s = jnp.concatenate([gc_t[B_HEADS + h:B_HEADS + h + 1, :] for h in range(B_HEADS)], axis=1)
        decay = jnp.where(incl, jnp.exp(jnp.where(incl, g_col_s - g_row_s, 0.0)), 0.0)
        lhs = jnp.concatenate([_stack_heads(kb.astype(BF16), B_HEADS), _stack_heads(q.astype(BF16), B_HEADS)],
                              axis=0)
        gram = _dot_nt(lhs, jnp.concatenate([k.astype(BF16)] * B_HEADS, axis=0))
        probs.append(dict(rows=rows, amat=jnp.where(strict, gram[0:n_rows] * decay, 0.0),
                          qk=gram[n_rows:] * decay, vb=v * beta_f, kbg=kb * e_g, qe=q * e_g,
                          kd=(k * jnp.exp(g_last - g_col)).astype(BF16), s_decay=jnp.exp(g_last)))
    tinvs = _unit_lower_inverses([-pr["amat"] for pr in probs], same_block)
    solved = [_dot(tinv, jnp.concatenate([_stack_heads(pr["vb"].astype(BF16), B_HEADS),
                                          _stack_heads(pr["kbg"].astype(BF16), B_HEADS)], axis=1))
              for tinv, pr in zip(tinvs, probs)]

    states = [state_ref[hd] for hd in range(B_HEADS)]
    for pr, sol in zip(probs, solved):
        rows, qk = pr["rows"], pr["qk"]
        u = pr["vb"] + _unstack_heads(sol[:, 0:B_WIDTH], B_HEADS)
        w = pr["kbg"] + _unstack_heads(sol[:, B_WIDTH:], B_HEADS)
        wq = jnp.concatenate([w, pr["qe"]], axis=0).astype(BF16)
        kd, g_last_e = pr["kd"], pr["s_decay"]
        v_new, q_state = [], []
        for hd in range(B_HEADS):
            sl = slice(hd * B_HEAD, (hd + 1) * B_HEAD)
            from_state = _dot(wq[:, sl], states[hd])
            vn = u[:, sl] - from_state[0:CHUNK]
            states[hd] = states[hd] * g_last_e[:, sl] + _dot_tn(kd[:, sl], vn)
            v_new.append(vn)
            q_state.append(from_state[CHUNK:])
        v_new = jnp.concatenate(v_new, axis=1)
        o = (jnp.concatenate(q_state, axis=1)
             + _unstack_heads(_dot(qk, _stack_heads(v_new.astype(BF16), B_HEADS)), B_HEADS))
        for hd in range(B_HEADS):
            sl = slice(hd * B_HEAD, (hd + 1) * B_HEAD)
            gate = gate_ref[rows, sl]
            y_ref[rows, sl] = (_rms(o[:, sl], ng_ref[...]) * (gate * _sigmoid(gate))).astype(y_ref.dtype)
    for hd in range(B_HEADS):
        state_ref[hd] = states[hd]


def _gdn(qkvb, gate, ba, conv_w, hp, norm_g, batch, seq):
    m = qkvb.shape[0]
    nc = seq // B_SCAN_ROWS
    row = lambda w: pl.BlockSpec((B_SCAN_ROWS, w), lambda b, c: (b * nc + c, 0))
    const = lambda arr: pl.BlockSpec(arr.shape, lambda b, c: (0,) * arr.ndim)
    return pl.pallas_call(
        _gdn_kernel,
        grid=(batch, nc),
        in_specs=[row(3 * B_WIDTH), row(B_WIDTH), row(LANES), const(conv_w), const(hp), const(norm_g)],
        out_specs=row(B_WIDTH),
        out_shape=jax.ShapeDtypeStruct((m, B_WIDTH), BF16),
        scratch_shapes=[pltpu.VMEM((B_HEADS, B_HEAD, B_HEAD), F32),
                        pltpu.VMEM((B_SCAN_ROWS + SUBLANES, 3 * B_WIDTH), F32)],
        compiler_params=pltpu.CompilerParams(dimension_semantics=("arbitrary", "arbitrary"),
                                             vmem_limit_bytes=VMEM_LIMIT),
        name="gdn_scan",
    )(qkvb, gate, ba, conv_w, hp, norm_g)


def _head_rms(x, ones_bd, g):
    sq = x * x
    hi = sq.astype(BF16)
    lo = (sq - hi.astype(F32)).astype(BF16)
    ssum = (jnp.dot(hi, ones_bd, preferred_element_type=F32)
            + jnp.dot(lo, ones_bd, preferred_element_type=F32))
    return x * lax.rsqrt(ssum * (1.0 / C_HEAD) + NORM_EPS) * g


def _attn_qkv_kernel(x_ref, g_ref, w_ref, qg_ref, kg_ref, ones_ref, o_ref):
    t = pl.program_id(1)

    @pl.when(t == 0)
    def _():
        o_ref[...] = jnp.zeros_like(o_ref)

    @pl.when(t > 0)
    def _():
        hb = _rms(x_ref[...], g_ref[...]).astype(BF16)
        ones_bd = ones_ref[...]
        blk = ones_bd.shape[0]
        for j in range(D_MODEL // blk):
            sl = slice(j * blk, (j + 1) * blk)
            q = _dot(hb, w_ref[:, sl])
            o_ref[:, sl] = (_head_rms(q, ones_bd, qg_ref[...]) * (C_HEAD ** -0.5)).astype(o_ref.dtype)
            ksl = slice(D_MODEL + j * blk, D_MODEL + (j + 1) * blk)
            k = _dot(hb, w_ref[:, ksl])
            o_ref[:, ksl] = _head_rms(k, ones_bd, kg_ref[...]).astype(o_ref.dtype)
        o_ref[:, 2 * D_MODEL:] = _dot(hb, w_ref[:, 2 * D_MODEL:]).astype(o_ref.dtype)


def _attn_qkv(x2d, g, w, qg, kg, ones_bd, batch, seq, tm):
    tiles = seq // tm
    pad_tiles = C_WINDOW // tm
    const = lambda arr: pl.BlockSpec(arr.shape, lambda b, t: (0,) * arr.ndim)
    return pl.pallas_call(
        _attn_qkv_kernel,
        grid=(batch, tiles + pad_tiles),
        in_specs=[pl.BlockSpec((tm, D_MODEL), lambda b, t: (b * tiles + jnp.maximum(t - pad_tiles, 0), 0)),
                  const(g), const(w), const(qg), const(kg), const(ones_bd)],
        out_specs=pl.BlockSpec((tm, 3 * D_MODEL), lambda b, t: (b * (tiles + pad_tiles) + t, 0)),
        out_shape=jax.ShapeDtypeStruct((batch * (seq + C_WINDOW), 3 * D_MODEL), BF16),
        compiler_params=pltpu.CompilerParams(dimension_semantics=("parallel", "arbitrary"),
                                             vmem_limit_bytes=VMEM_LIMIT),
        name="attn_qkv",
    )(x2d, g, w, qg, kg, ones_bd)


def _attn_kernel(q_ref, k_ref, v_ref, bias_ref, o_ref):
    t = pl.program_id(2)
    start = pl.multiple_of(t * C_QTILE, C_QTILE)
    kw = k_ref[pl.ds(start, C_WINDOW + C_QTILE), :]
    vw = v_ref[pl.ds(start, C_WINDOW + C_QTILE), :]
    q = q_ref[...]
    lane_lo = lax.broadcasted_iota(jnp.int32, (C_QTILE, LANES), 1) < C_HEAD
    key_idx = lax.broadcasted_iota(jnp.int32, (C_QTILE, C_WINDOW + C_QTILE), 1)
    valid = key_idx >= C_WINDOW - t * C_QTILE
    outs = []
    for head in range(2):
        keep = lane_lo if head == 0 else jnp.logical_not(lane_lo)
        s = _dot_nt(jnp.where(keep, q, jnp.zeros_like(q)), kw) + bias_ref[head]
        s = jnp.where(valid, s, MASK_VALUE)
        p = jnp.exp(s - jnp.max(s, axis=-1, keepdims=True))
        denom = jnp.sum(p, axis=-1, keepdims=True)
        outs.append(_dot(p, vw) / denom)
    o_ref[...] = jnp.where(lane_lo, outs[0], outs[1]).astype(o_ref.dtype)


def _attn(qkv_pad, bias, batch, seq):
    nt = seq // C_QTILE
    pad_rows = seq + C_WINDOW
    n_pairs = D_MODEL // LANES
    q_off = C_WINDOW // C_QTILE
    return pl.pallas_call(
        _attn_kernel,
        grid=(batch, n_pairs, nt),
        in_specs=[pl.BlockSpec((C_QTILE, LANES), lambda b, h, t: (b * (pad_rows // C_QTILE) + q_off + t, h)),
                  pl.BlockSpec((pad_rows, LANES), lambda b, h, t: (b, n_pairs + h)),
                  pl.BlockSpec((pad_rows, LANES), lambda b, h, t: (b, 2 * n_pairs + h)),
                  pl.BlockSpec((2, C_QTILE, C_WINDOW + C_QTILE), lambda b, h, t: (h, 0, 0))],
        out_specs=pl.BlockSpec((C_QTILE, LANES), lambda b, h, t: (b * nt + t, h)),
        out_shape=jax.ShapeDtypeStruct((batch * seq, D_MODEL), BF16),
        compiler_params=pltpu.CompilerParams(dimension_semantics=("parallel", "parallel", "arbitrary"),
                                             vmem_limit_bytes=VMEM_LIMIT),
        name="band_attn",
    )(qkv_pad, qkv_pad, qkv_pad, bias)


def _post_kernel(x_ref, *rest, n_mix):
    mix_refs = rest[:n_mix]
    wo_ref, g_ref, w1_ref, w2_ref, o_ref = rest[n_mix:]
    mix = jnp.concatenate([mref[...].astype(BF16) for mref in mix_refs], axis=1)
    x = x_ref[...] + _dot(mix, wo_ref[...])
    hb = _rms(x, g_ref[...]).astype(BF16)
    acc = None
    for j in range(D_FF // FF_CHUNK):
        sl = slice(j * FF_CHUNK, (j + 1) * FF_CHUNK)
        hid = jnp.maximum(_dot(hb, w1_ref[:, sl]), 0.0)
        part = _dot(hid * hid, w2_ref[sl, :])
        acc = part if acc is None else acc + part
    o_ref[...] = x + acc


def _post(x2d, mixes, wo, g, w1, w2, tm):
    m = x2d.shape[0]
    const = lambda arr: pl.BlockSpec(arr.shape, lambda i: (0,) * arr.ndim)
    row = lambda w: pl.BlockSpec((tm, w), lambda i: (i, 0))
    return pl.pallas_call(
        functools.partial(_post_kernel, n_mix=len(mixes)),
        grid=(m // tm,),
        in_specs=[row(D_MODEL)] + [row(mx.shape[1]) for mx in mixes] + [const(wo), const(g), const(w1), const(w2)],
        out_specs=row(D_MODEL),
        out_shape=jax.ShapeDtypeStruct((m, D_MODEL), F32),
        compiler_params=pltpu.CompilerParams(dimension_semantics=("parallel",), vmem_limit_bytes=VMEM_LIMIT),
        name="mix_out_mlp",
    )(x2d, *mixes, wo, g, w1, w2)


def _ple_kernel(x_ref, p_ref, wp_ref, g_ref, wg_ref, o_ref):
    x = x_ref[...]
    emb = _rms(_dot(p_ref[...], wp_ref[...]), g_ref[...])
    o_ref[...] = x + emb * _sigmoid(_dot(x, wg_ref[...]))


def _ple(x2d, p2d, wp, g, wg, tm):
    m = x2d.shape[0]
    const = lambda arr: pl.BlockSpec(arr.shape, lambda i: (0,) * arr.ndim)
    row = lambda w: pl.BlockSpec((tm, w), lambda i: (i, 0))
    return pl.pallas_call(
        _ple_kernel,
        grid=(m // tm,),
        in_specs=[row(D_MODEL), row(p2d.shape[1]), const(wp), const(g), const(wg)],
        out_specs=row(D_MODEL),
        out_shape=jax.ShapeDtypeStruct((m, D_MODEL), F32),
        compiler_params=pltpu.CompilerParams(dimension_semantics=("parallel",), vmem_limit_bytes=VMEM_LIMIT),
        name="ple_gate",
    )(x2d, p2d, wp, g, wg)


def _pad_to(w, rows=None, cols=None):
    r = (rows or w.shape[0]) - w.shape[0]
    c = (cols or w.shape[1]) - w.shape[1]
    return jnp.pad(w, ((0, r), (0, c)))


def _rel_bias_table(rel_bias):
    n_heads = rel_bias.shape[0]
    span = CHUNK + C_BAND - 1
    rel = (C_BAND - 1) - jnp.arange(span)
    f = rel_bias[:, jnp.clip(rel, -C_MAX_REL, C_MAX_REL) + C_MAX_REL]
    g = jnp.tile(jnp.pad(f, ((0, 0), (0, 1))), (1, CHUNK))[:, :CHUNK * span].reshape(n_heads, CHUNK, span)
    band = g[:, :, CHUNK - 1:CHUNK - 1 + C_BAND]
    rows = [jnp.pad(band, ((0, 0), (0, 0), (cc * CHUNK, C_QTILE - CHUNK - cc * CHUNK)), constant_values=MASK_VALUE)
            for cc in range(C_QTILE // CHUNK)]
    return jnp.concatenate(rows, axis=1)


def kernel(x, p, norm_mix_g, norm_ffn_g, even_w_in, rwkv_mu_proj, rwkv_mu_lora, rwkv_w0, rwkv_w1, rwkv_w2, rwkv_a0, rwkv_a1, rwkv_a2, rwkv_g1, rwkv_g2, rwkv_k_k, rwkv_k_a, rwkv_r_k, rwkv_ln_g, rwkv_ln_b, rwkv_v_mu, rwkv_v0, rwkv_v1, rwkv_v2, gdn_conv_w, gdn_a_log, gdn_dt_bias, gdn_norm_g, even_w_out, attn_w_qkv, attn_q_g, attn_k_g, attn_rel_bias, attn_w_out, mlp_w1, mlp_w2, ple_w_proj, ple_norm_g, ple_w_gate):
    batch, seq, _ = x.shape
    depth = p.shape[0]
    assert seq % 512 == 0 and x.shape[2] == D_MODEL
    tm_in, tm_post, tm_ple, tm_qkv = 256, 256, 512, 512
    xs = x.reshape(batch * seq, D_MODEL)
    row1 = lambda vec: vec.reshape(1, -1)
    main_cols = 3 * A_WIDTH + 4 * B_WIDTH

    blk = 4 * C_HEAD
    ones_bd = (jnp.arange(blk)[:, None] // C_HEAD == jnp.arange(blk)[None, :] // C_HEAD).astype(BF16)

    v_first = None
    for i in range(depth):
        if i % 2 == 0:
            e = i // 2
            has_vres = e > 0
            win = even_w_in[e]
            mus = [rwkv_mu_lora[e, 0], rwkv_mu_lora[e, 1], rwkv_mu_lora[e, 2]]
            l1s = [rwkv_w1[e], rwkv_a1[e], rwkv_g1[e]]
            l2s = [rwkv_w2[e], rwkv_a2[e], rwkv_g2[e]]
            lbs = [rwkv_w0[e], rwkv_a0[e]]
            if has_vres:
                mus.append(rwkv_v_mu[e - 1])
                l1s.append(rwkv_v1[e - 1])
                l2s.append(rwkv_v2[e - 1])
                lbs.append(rwkv_v0[e - 1])
            l1 = jnp.stack([_pad_to(w, cols=A_LORA_PAD) for w in l1s]).astype(BF16)
            l2 = jnp.stack([_pad_to(w, rows=A_LORA_PAD) for w in l2s]).astype(BF16)
            outs = _even_in(xs, seq, row1(norm_mix_g[i]), win[:, :main_cols].astype(BF16),
                            _pad_to(win[:, main_cols:], cols=LANES).astype(BF16),
                            jnp.stack(mus), l1, l2, jnp.stack(lbs), has_vres, tm_in)
            rkv, qkvb, gate, ba, lw, a_lr, gg = outs[:7]
            par = jnp.stack([rwkv_k_k[e], rwkv_k_a[e], rwkv_r_k[e].reshape(-1), rwkv_ln_g[e], rwkv_ln_b[e]])
            y_a, v_first = _rwkv(rkv, lw, a_lr, gg, outs[7] if has_vres else None, v_first, par,
                                 rwkv_mu_proj[e].reshape(1, -1), batch, seq)
            hp = jnp.stack([_pad_to(jnp.pad(row1(gdn_a_log[e]), ((0, 0), (B_HEADS, 0))), cols=LANES)[0],
                            _pad_to(jnp.pad(row1(gdn_dt_bias[e]), ((0, 0), (B_HEADS, 0))), cols=LANES)[0]])
            y_b = _gdn(qkvb, gate, ba, gdn_conv_w[e], hp, row1(gdn_norm_g[e]), batch, seq)
            mixes, wo = [y_a, y_b], even_w_out[e]
        else:
            o = i // 2
            tile4 = lambda gvec: jnp.tile(gvec, blk // C_HEAD).reshape(1, blk)
            qkv_pad = _attn_qkv(xs, row1(norm_mix_g[i]), attn_w_qkv[o].astype(BF16), tile4(attn_q_g[o]),
                                tile4(attn_k_g[o]), ones_bd, batch, seq, tm_qkv)
            mixes, wo = [_attn(qkv_pad, _rel_bias_table(attn_rel_bias[o]), batch, seq)], attn_w_out[o]
        xs = _post(xs, mixes, wo.astype(BF16), row1(norm_ffn_g[i]), mlp_w1[i].astype(BF16),
                   mlp_w2[i].astype(BF16), tm_post)
        xs = _ple(xs, p[i].reshape(batch * seq, -1), ple_w_proj[i].astype(BF16), row1(ple_norm_g[i]),
                  ple_w_gate[i].astype(BF16), tm_ple)
    return xs.reshape(batch, seq, D_MODEL)
```

```python
import functools

import jax
import jax.numpy as jnp
from jax import lax
from jax.experimental import pallas as pl
from jax.experimental.pallas import tpu as pltpu

F32 = jnp.float32
BF16 = jnp.bfloat16

D_MODEL = 1024
CHUNK = 64
NORM_EPS = 1e-6
L2_EPS = 1e-6
A_WIDTH = 512
A_HEAD = 64
A_GN_EPS = 64e-5
A_LORA_PAD = 128
A_GROUP = 256
A_SCAN_ROWS = 4 * CHUNK
B_SCAN_ROWS = 4 * CHUNK
B_WIDTH = 512
B_HEADS = 4
B_HEAD = 128
B_CONV = 4
C_HEADS = 16
C_HEAD = 64
C_WINDOW = 8 * CHUNK
C_BAND = C_WINDOW + CHUNK
C_QTILE = 4 * CHUNK
C_MAX_REL = 256
D_FF = 4096
FF_CHUNK = 1024
LANES = 128
SUBLANES = 8
INV_BLOCK = 16
MASK_VALUE = -1e30
VMEM_LIMIT = 56 * 1024 * 1024


def _dot(a, b):
    return jnp.dot(a.astype(BF16), b.astype(BF16), preferred_element_type=F32)


def _dot_nt(a, b):
    return lax.dot_general(a.astype(BF16), b.astype(BF16), (((1,), (1,)), ((), ())),
                           preferred_element_type=F32)


def _dot_tn(a, b):
    return jnp.dot(a.astype(BF16).T, b.astype(BF16), preferred_element_type=F32)


def _rms(x, g, eps=NORM_EPS):
    return x * lax.rsqrt(jnp.mean(x * x, axis=-1, keepdims=True) + eps) * g


def _sigmoid(z):
    return 1.0 / (1.0 + jnp.exp(-z))


def _softplus(z):
    return jnp.maximum(z, 0.0) + jnp.log(1.0 + jnp.exp(-jnp.abs(z)))


def _shift_rows(x, prev_row):
    row = lax.broadcasted_iota(jnp.int32, x.shape, 0)
    return jnp.where(row == 0, prev_row, pltpu.roll(x, 1, 0))


def _cumsum_rows(x):
    row = lax.broadcasted_iota(jnp.int32, x.shape, 0)
    step = 1
    while step < x.shape[0]:
        x = x + jnp.where(row >= step, pltpu.roll(x, step, 0), 0.0)
        step *= 2
    return x


def _stack_heads(x, n_heads):
    head_w = x.shape[1] // n_heads
    lane_head = lax.broadcasted_iota(jnp.int32, x.shape, 1) >> (head_w.bit_length() - 1)
    zero = jnp.zeros_like(x)
    return jnp.concatenate([jnp.where(lane_head == h, x, zero) for h in range(n_heads)], axis=0)


def _unstack_heads(xs, n_heads):
    out = xs[0:CHUNK]
    for h in range(1, n_heads):
        out = out + xs[h * CHUNK:(h + 1) * CHUNK]
    return out


def _block_masks(n_heads):
    n = n_heads * CHUNK
    r = lax.broadcasted_iota(jnp.int32, (n, n), 0)
    c = lax.broadcasted_iota(jnp.int32, (n, n), 1)
    same_head = (r ^ c) < CHUNK
    delta = jnp.where(same_head, r - c, -1)
    return same_head, delta >= 0, delta > 0, (r ^ c) < INV_BLOCK


def _unit_lower_inverses(n_mats, same_block):
    nds = [jnp.where(same_block, n, 0.0) for n in n_mats]
    nos = [n - nd for n, nd in zip(n_mats, nds)]
    tds, pws = nds, nds
    for _ in range(3):
        pws = [_dot(pw, pw) for pw in pws]
        tds = [td + pw + _dot(td, pw) for td, pw in zip(tds, pws)]
    es = [no + _dot(td, no) for td, no in zip(tds, nos)]
    e2s = [_dot(e, e) for e in es]
    fs = [e + e2 + _dot(e, e2) for e, e2 in zip(es, e2s)]
    return [f + td + _dot(f, td) for f, td in zip(fs, tds)]


def _seg_sum(x):
    seg_lo = lax.broadcasted_iota(jnp.int32, (x.shape[0], LANES), 1) < LANES // 2
    outs = []
    for j in range(x.shape[1] // LANES):
        blk = x[:, j * LANES:(j + 1) * LANES]
        s_lo = jnp.sum(jnp.where(seg_lo, blk, 0.0), axis=-1, keepdims=True)
        s_hi = jnp.sum(jnp.where(seg_lo, 0.0, blk), axis=-1, keepdims=True)
        outs.append(jnp.where(seg_lo, s_lo, s_hi))
    return jnp.concatenate(outs, axis=1)


def _even_in_kernel(x_ref, halo_ref, g_ref, win_ref, wtail_ref, mu_ref, l1_ref, l2_ref, lb_ref,
                    rkv_ref, qkvb_ref, gate_ref, ba_ref, lw_ref, a_ref, gg_ref, *vg_ref,
                    tiles_per_seq):
    i = pl.program_id(0)
    g = g_ref[...]
    h = _rms(x_ref[...], g)
    prev = _rms(halo_ref[...], g)[SUBLANES - 1:SUBLANES, :]
    prev = jnp.where(i % tiles_per_seq == 0, 0.0, prev)
    dh = _shift_rows(h, prev) - h
    hb = h.astype(BF16)
    rkv_ref[...] = _dot(hb, win_ref[:, 0:3 * A_WIDTH])
    qkvb_ref[...] = _dot(hb, win_ref[:, 3 * A_WIDTH:3 * A_WIDTH + 3 * B_WIDTH])
    gate_ref[...] = _dot(hb, win_ref[:, 3 * A_WIDTH + 3 * B_WIDTH:])
    ba_ref[...] = _dot(hb, wtail_ref[...])

    def lora_in(j):
        return _dot(h + dh * mu_ref[j:j + 1, :], l1_ref[j])

    w_raw = lb_ref[0:1, :] + _dot(jnp.tanh(lora_in(0)), l2_ref[0])
    w_log = -_softplus(-w_raw) - 0.5
    lw_ref[...] = -jnp.exp(w_log)
    a_ref[...] = _sigmoid(lb_ref[1:2, :] + _dot(lora_in(1), l2_ref[1]))
    gg_ref[...] = _dot(_sigmoid(lora_in(2)), l2_ref[2])
    if vg_ref:
        vg_ref[0][...] = _sigmoid(lb_ref[2:3, :] + _dot(lora_in(3), l2_ref[3]))


def _even_in(x2d, seq, g, win, wtail, mu, l1, l2, lb, has_vres, tm):
    m = x2d.shape[0]
    n_out = 8 if has_vres else 7
    widths = [3 * A_WIDTH, 3 * B_WIDTH, B_WIDTH, LANES, A_WIDTH, A_WIDTH, A_WIDTH, A_WIDTH][:n_out]
    const = lambda a: pl.BlockSpec(a.shape, lambda i: (0,) * a.ndim)
    return pl.pallas_call(
        functools.partial(_even_in_kernel, tiles_per_seq=seq // tm),
        grid=(m // tm,),
        in_specs=[pl.BlockSpec((tm, D_MODEL), lambda i: (i, 0)),
                  pl.BlockSpec((SUBLANES, D_MODEL), lambda i: (jnp.maximum(i * (tm // SUBLANES) - 1, 0), 0)),
                  const(g), const(win), const(wtail), const(mu), const(l1), const(l2), const(lb)],
        out_specs=[pl.BlockSpec((tm, w), lambda i: (i, 0)) for w in widths],
        out_shape=[jax.ShapeDtypeStruct((m, w), F32) for w in widths],
        compiler_params=pltpu.CompilerParams(dimension_semantics=("parallel",), vmem_limit_bytes=VMEM_LIMIT),
        name="even_in",
    )(x2d, x2d, g, win, wtail, mu, l1, l2, lb)


def _rwkv_kernel(rkv_ref, lw_ref, a_ref, gg_ref, *rest, has_vres):
    if has_vres:
        vg_ref, vfirst_ref, par_ref, mu_ref, y_ref, state_ref, tail_ref = rest
    else:
        par_ref, mu_ref, y_ref, vfirst_out_ref, state_ref, tail_ref = rest
    c = pl.program_id(1)

    @pl.when(c == 0)
    def _():
        state_ref[...] = jnp.zeros_like(state_ref)
        tail_ref[...] = jnp.zeros_like(tail_ref)

    rkv = rkv_ref[...]
    rkv_prev = _shift_rows(rkv, tail_ref[SUBLANES - 1:SUBLANES, :])
    tail_ref[...] = rkv[rkv.shape[0] - SUBLANES:, :]
    rkv = rkv + (rkv_prev - rkv) * mu_ref[...]
    r = rkv[:, 0:A_WIDTH]
    k = rkv[:, A_WIDTH:2 * A_WIDTH]
    v = rkv[:, 2 * A_WIDTH:]
    k_k, k_a, r_k, ln_g, ln_b = (par_ref[j:j + 1, :] for j in range(5))
    a = a_ref[...]
    if has_vres:
        v = v + (vfirst_ref[...] - v) * vg_ref[...]
    else:
        vfirst_out_ref[...] = v

    lw_all = lw_ref[...]
    kk_all = k * k_k
    kk_all = kk_all * lax.rsqrt(_seg_sum(kk_all * kk_all) + L2_EPS)
    k2_all = k * (1.0 + (a - 1.0) * k_a)
    b_all = kk_all * a
    bonus = _seg_sum(r * k2_all * r_k) * v
    v_bf_all = v.astype(BF16)

    heads = A_GROUP // A_HEAD
    n_groups = A_WIDTH // A_GROUP
    n_rows = heads * CHUNK
    same_head, incl, strict, same_block = _block_masks(heads)
    incl2 = jnp.concatenate([incl, incl], axis=1)
    n_chunks = rkv.shape[0] // CHUNK
    group_sl = [slice(g * A_GROUP, (g + 1) * A_GROUP) for g in range(n_groups)]

    probs = []
    for cc in range(n_chunks):
        rows = slice(cc * CHUNK, (cc + 1) * CHUNK)
        lw, kk, k2, b = lw_all[rows], kk_all[rows], k2_all[rows], b_all[rows]
        cw = _cumsum_rows(lw)
        cw_last = cw[CHUNK - 1:CHUNK, :]
        e_neg = jnp.exp(-cw)
        e_last = jnp.exp(cw_last - cw)
        r_t = (r[rows] * jnp.exp(cw)).astype(BF16)
        a_t = (-kk * jnp.exp(cw - lw)).astype(BF16)
        b_t = (b * e_neg).astype(BF16)
        k_t = (k2 * e_neg).astype(BF16)
        b_hat = (b * e_last).astype(BF16)
        k_hat = (k2 * e_last).astype(BF16)
        for sl in group_sl:
            lhs = jnp.concatenate([_stack_heads(a_t[:, sl], heads), _stack_heads(r_t[:, sl], heads)], axis=0)
            rhs = jnp.concatenate([b_t[:, sl]] * heads + [k_t[:, sl]] * heads, axis=0)
            probs.append(dict(
                rows=rows, sl=sl, gram=_dot_nt(lhs, rhs),
                ar=jnp.concatenate([a_t[:, sl], r_t[:, sl]], axis=0),
                bk_hat=jnp.concatenate([b_hat[:, sl], k_hat[:, sl]], axis=0),
                v=v_bf_all[rows, sl], w_last=jnp.exp(cw_last[:, sl])))
    minvs = _unit_lower_inverses([jnp.where(strict, pr["gram"][0:n_rows, 0:n_rows], 0.0) for pr in probs],
                                 same_block)

    states = [state_ref[g] for g in range(n_groups)]
    for cc in range(n_chunks):
        prs = probs[cc * n_groups:(cc + 1) * n_groups]
        mis = minvs[cc * n_groups:(cc + 1) * n_groups]
        from_state = [_dot_nt(pr["ar"], st) for pr, st in zip(prs, states)]
        v_s = [_stack_heads(pr["v"], heads) for pr in prs]
        xs = [fs[0:CHUNK] + _unstack_heads(_dot(jnp.where(strict, pr["gram"][0:n_rows, n_rows:], 0.0), vs), heads)
              for fs, pr, vs in zip(from_state, prs, v_s)]
        us = [x + _unstack_heads(_dot(mi, _stack_heads(x.astype(BF16), heads)), heads) for x, mi in zip(xs, mis)]
        u_bf = [u.astype(BF16) for u in us]
        ys = [fs[CHUNK:] + _unstack_heads(_dot(jnp.where(incl2, pr["gram"][n_rows:, :], 0.0),
                                               jnp.concatenate([_stack_heads(ub, heads), vs], axis=0)), heads)
              for fs, pr, ub, vs in zip(from_state, prs, u_bf, v_s)]
        upds = [_dot_tn(jnp.concatenate([ub, pr["v"]], axis=0), pr["bk_hat"]) for ub, pr in zip(u_bf, prs)]
        states = [st * pr["w_last"] + jnp.where(same_head, upd, 0.0) for st, pr, upd in zip(states, prs, upds)]
        for pr, y in zip(prs, ys):
            rows, sl = pr["rows"], pr["sl"]
            mean = _seg_sum(y) * (1.0 / A_HEAD)
            yc = y - mean
            var = _seg_sum(yc * yc) * (1.0 / A_HEAD)
            yn = yc * lax.rsqrt(var + A_GN_EPS) * ln_g[:, sl] + ln_b[:, sl]
            y_ref[rows, sl] = ((yn + bonus[rows, sl]) * gg_ref[rows, sl]).astype(y_ref.dtype)
    for g in range(n_groups):
        state_ref[g] = states[g]


def _rwkv(rkv, lw, a, gg, vg, vfirst, par, mu, batch, seq):
    m = rkv.shape[0]
    nc = seq // A_SCAN_ROWS
    has_vres = vg is not None
    row = lambda w: pl.BlockSpec((A_SCAN_ROWS, w), lambda b, c: (b * nc + c, 0))
    const = lambda arr: pl.BlockSpec(arr.shape, lambda b, c: (0,) * arr.ndim)
    ins = [rkv, lw, a, gg] + ([vg, vfirst] if has_vres else []) + [par, mu]
    in_specs = [row(3 * A_WIDTH)] + [row(A_WIDTH)] * (5 if has_vres else 3) + [const(par), const(mu)]
    n_out = 1 if has_vres else 2
    outs = pl.pallas_call(
        functools.partial(_rwkv_kernel, has_vres=has_vres),
        grid=(batch, nc),
        in_specs=in_specs,
        out_specs=[row(A_WIDTH)] * n_out,
        out_shape=[jax.ShapeDtypeStruct((m, A_WIDTH), dt) for dt in (BF16, F32)[:n_out]],
        scratch_shapes=[pltpu.VMEM((A_WIDTH // A_GROUP, A_GROUP, A_GROUP), F32),
                        pltpu.VMEM((SUBLANES, 3 * A_WIDTH), F32)],
        compiler_params=pltpu.CompilerParams(dimension_semantics=("arbitrary", "arbitrary"),
                                             vmem_limit_bytes=VMEM_LIMIT),
        name="rwkv7_scan",
    )(*ins)
    return outs if not has_vres else (outs[0], vfirst)


def _gdn_kernel(qkv_ref, gate_ref, ba_ref, conv_ref, hp_ref, ng_ref, y_ref, state_ref, zz_ref):
    c = pl.program_id(1)

    @pl.when(c == 0)
    def _():
        state_ref[...] = jnp.zeros_like(state_ref)
        zz_ref[0:SUBLANES, :] = jnp.zeros((SUBLANES, 3 * B_WIDTH), F32)

    z = qkv_ref[...]
    zz_ref[SUBLANES:, :] = z
    zz = zz_ref[...]
    zz_ref[0:SUBLANES, :] = z[z.shape[0] - SUBLANES:, :]
    conv = zz[SUBLANES:, :] * conv_ref[B_CONV - 1:B_CONV, :]
    for j in range(B_CONV - 1):
        conv = conv + pltpu.roll(zz, B_CONV - 1 - j, 0)[SUBLANES:, :] * conv_ref[j:j + 1, :]
    qkv = conv * _sigmoid(conv)

    def per_head(tile, lane0):
        return jnp.concatenate([jnp.broadcast_to(tile[:, lane0 + h:lane0 + h + 1], (CHUNK, B_HEAD))
                                for h in range(B_HEADS)], axis=1)

    def l2n(t):
        return jnp.concatenate(
            [t[:, h * B_HEAD:(h + 1) * B_HEAD]
             * lax.rsqrt(jnp.sum(jnp.square(t[:, h * B_HEAD:(h + 1) * B_HEAD]), axis=-1, keepdims=True) + L2_EPS)
             for h in range(B_HEADS)], axis=1)

    q_all = l2n(qkv[:, 0:B_WIDTH]) * (B_HEAD ** -0.5)
    k_all = l2n(qkv[:, B_WIDTH:2 * B_WIDTH])
    v_all = qkv[:, 2 * B_WIDTH:]
    ba_all = ba_ref[...]
    beta_all = _sigmoid(ba_all)
    g_step_all = -jnp.exp(hp_ref[0:1, :]) * _softplus(ba_all + hp_ref[1:2, :])

    same_head, incl, strict, same_block = _block_masks(B_HEADS)
    n_rows = B_HEADS * CHUNK
    probs = []
    for cc in range(z.shape[0] // CHUNK):
        rows = slice(cc * CHUNK, (cc + 1) * CHUNK)
        q, k, v = q_all[rows], k_all[rows], v_all[rows]
        gc = _cumsum_rows(g_step_all[rows])
        gc_t = gc.T
        beta_f = per_head(beta_all[rows], 0)
        g_col = per_head(gc, B_HEADS)
        g_last = g_col[CHUNK - 1:CHUNK, :]
        e_g = jnp.exp(g_col)
        kb = k * beta_f
        g_col_s = jnp.concatenate([gc[:, B_HEADS + h:B_HEADS + h + 1] for h in range(B_HEADS)], axis=0)
        g_row_s = jnp.concatenate([gc_t[B_HEADS + h:B_HEADS + h + 1, :] for h in range(B_HEADS)], axis=1)
        decay = jnp.where(incl, jnp.exp(jnp.where(incl, g_col_s - g_row_s, 0.0)), 0.0)
        lhs = jnp.concatenate([_stack_heads(kb.astype(BF16), B_HEADS), _stack_heads(q.astype(BF16), B_HEADS)],
                              axis=0)
        gram = _dot_nt(lhs, jnp.concatenate([k.astype(BF16)] * B_HEADS, axis=0))
        probs.append(dict(rows=rows, amat=jnp.where(strict, gram[0:n_rows] * decay, 0.0),
                          qk=gram[n_rows:] * decay, vb=v * beta_f, kbg=kb * e_g, qe=q * e_g,
                          kd=(k * jnp.exp(g_last - g_col)).astype(BF16), s_decay=jnp.exp(g_last)))
    tinvs = _unit_lower_inverses([-pr["amat"] for pr in probs], same_block)
    solved = [_dot(tinv, jnp.concatenate([_stack_heads(pr["vb"].astype(BF16), B_HEADS),
                                          _stack_heads(pr["kbg"].astype(BF16), B_HEADS)], axis=1))
              for tinv, pr in zip(tinvs, probs)]

    states = [state_ref[hd] for hd in range(B_HEADS)]
    for pr, sol in zip(probs, solved):
        rows, qk = pr["rows"], pr["qk"]
        u = pr["vb"] + _unstack_heads(sol[:, 0:B_WIDTH], B_HEADS)
        w = pr["kbg"] + _unstack_heads(sol[:, B_WIDTH:], B_HEADS)
        wq = jnp.concatenate([w, pr["qe"]], axis=0).astype(BF16)
        kd, g_last_e = pr["kd"], pr["s_decay"]
        v_new, q_state = [], []
        for hd in range(B_HEADS):
            sl = slice(hd * B_HEAD, (hd + 1) * B_HEAD)
            from_state = _dot(wq[:, sl], states[hd])
            vn = u[:, sl] - from_state[0:CHUNK]
            states[hd] = states[hd] * g_last_e[:, sl] + _dot_tn(kd[:, sl], vn)
            v_new.append(vn)
            q_state.append(from_state[CHUNK:])
        v_new = jnp.concatenate(v_new, axis=1)
        o = (jnp.concatenate(q_state, axis=1)
             + _unstack_heads(_dot(qk, _stack_heads(v_new.astype(BF16), B_HEADS)), B_HEADS))
        for hd in range(B_HEADS):
            sl = slice(hd * B_HEAD, (hd + 1) * B_HEAD)
            gate = gate_ref[rows, sl]
            y_ref[rows, sl] = (_rms(o[:, sl], ng_ref[...]) * (gate * _sigmoid(gate))).astype(y_ref.dtype)
    for hd in range(B_HEADS):
        state_ref[hd] = states[hd]


def _gdn(qkvb, gate, ba, conv_w, hp, norm_g, batch, seq):
    m = qkvb.shape[0]
    nc = seq // B_SCAN_ROWS
    row = lambda w: pl.BlockSpec((B_SCAN_ROWS, w), lambda b, c: (b * nc + c, 0))
    const = lambda arr: pl.BlockSpec(arr.shape, lambda b, c: (0,) * arr.ndim)
    return pl.pallas_call(
        _gdn_kernel,
        grid=(batch, nc),
        in_specs=[row(3 * B_WIDTH), row(B_WIDTH), row(LANES), const(conv_w), const(hp), const(norm_g)],
        out_specs=row(B_WIDTH),
        out_shape=jax.ShapeDtypeStruct((m, B_WIDTH), BF16),
        scratch_shapes=[pltpu.VMEM((B_HEADS, B_HEAD, B_HEAD), F32),
                        pltpu.VMEM((B_SCAN_ROWS + SUBLANES, 3 * B_WIDTH), F32)],
        compiler_params=pltpu.CompilerParams(dimension_semantics=("arbitrary", "arbitrary"),
                                             vmem_limit_bytes=VMEM_LIMIT),
        name="gdn_scan",
    )(qkvb, gate, ba, conv_w, hp, norm_g)


def _head_rms(x, ones_bd, g):
    sq = x * x
    hi = sq.astype(BF16)
    lo = (sq - hi.astype(F32)).astype(BF16)
    ssum = (jnp.dot(hi, ones_bd, preferred_element_type=F32)
            + jnp.dot(lo, ones_bd, preferred_element_type=F32))
    return x * lax.rsqrt(ssum * (1.0 / C_HEAD) + NORM_EPS) * g


def _attn_qkv_kernel(x_ref, g_ref, w_ref, qg_ref, kg_ref, ones_ref, o_ref):
    t = pl.program_id(1)

    @pl.when(t == 0)
    def _():
        o_ref[...] = jnp.zeros_like(o_ref)

    @pl.when(t > 0)
    def _():
        hb = _rms(x_ref[...], g_ref[...]).astype(BF16)
        ones_bd = ones_ref[...]
        blk = ones_bd.shape[0]
        for j in range(D_MODEL // blk):
            sl = slice(j * blk, (j + 1) * blk)
            q = _dot(hb, w_ref[:, sl])
            o_ref[:, sl] = (_head_rms(q, ones_bd, qg_ref[...]) * (C_HEAD ** -0.5)).astype(o_ref.dtype)
            ksl = slice(D_MODEL + j * blk, D_MODEL + (j + 1) * blk)
            k = _dot(hb, w_ref[:, ksl])
            o_ref[:, ksl] = _head_rms(k, ones_bd, kg_ref[...]).astype(o_ref.dtype)
        o_ref[:, 2 * D_MODEL:] = _dot(hb, w_ref[:, 2 * D_MODEL:]).astype(o_ref.dtype)


def _attn_qkv(x2d, g, w, qg, kg, ones_bd, batch, seq, tm):
    tiles = seq // tm
    pad_tiles = C_WINDOW // tm
    const = lambda arr: pl.BlockSpec(arr.shape, lambda b, t: (0,) * arr.ndim)
    return pl.pallas_call(
        _attn_qkv_kernel,
        grid=(batch, tiles + pad_tiles),
        in_specs=[pl.BlockSpec((tm, D_MODEL), lambda b, t: (b * tiles + jnp.maximum(t - pad_tiles, 0), 0)),
                  const(g), const(w), const(qg), const(kg), const(ones_bd)],
        out_specs=pl.BlockSpec((tm, 3 * D_MODEL), lambda b, t: (b * (tiles + pad_tiles) + t, 0)),
        out_shape=jax.ShapeDtypeStruct((batch * (seq + C_WINDOW), 3 * D_MODEL), BF16),
        compiler_params=pltpu.CompilerParams(dimension_semantics=("parallel", "arbitrary"),
                                             vmem_limit_bytes=VMEM_LIMIT),
        name="attn_qkv",
    )(x2d, g, w, qg, kg, ones_bd)


def _attn_kernel(q_ref, k_ref, v_ref, bias_ref, o_ref):
    t = pl.program_id(2)
    lane_lo = lax.broadcasted_iota(jnp.int32, (CHUNK, LANES), 1) < C_HEAD
    key_idx = lax.broadcasted_iota(jnp.int32, (CHUNK, C_BAND), 1)
    zero = jnp.zeros((CHUNK, LANES), q_ref.dtype)
    scores, windows = [], []
    for cc in range(C_QTILE // CHUNK):
        first_key = pl.multiple_of(t * C_QTILE + cc * CHUNK, CHUNK)
        kw = k_ref[pl.ds(first_key, C_BAND), :]
        windows.append(v_ref[pl.ds(first_key, C_BAND), :])
        q = q_ref[cc * CHUNK:(cc + 1) * CHUNK, :]
        valid = key_idx >= C_WINDOW - first_key
        for head in range(2):
            keep = lane_lo if head == 0 else jnp.logical_not(lane_lo)
            s = _dot_nt(jnp.where(keep, q, zero), kw) + bias_ref[head]
            scores.append(jnp.where(valid, s, MASK_VALUE))
    probs = [jnp.exp(s - jnp.max(s, axis=-1, keepdims=True)) for s in scores]
    denoms = [jnp.sum(p, axis=-1, keepdims=True) for p in probs]
    outs = [_dot(p, windows[j // 2]) / d for j, (p, d) in enumerate(zip(probs, denoms))]
    for cc in range(C_QTILE // CHUNK):
        o_ref[cc * CHUNK:(cc + 1) * CHUNK, :] = jnp.where(lane_lo, outs[2 * cc], outs[2 * cc + 1]).astype(o_ref.dtype)


def _attn(qkv_pad, bias, batch, seq):
    nt = seq // C_QTILE
    pad_rows = seq + C_WINDOW
    n_pairs = D_MODEL // LANES
    q_off = C_WINDOW // C_QTILE
    return pl.pallas_call(
        _attn_kernel,
        grid=(batch, n_pairs, nt),
        in_specs=[pl.BlockSpec((C_QTILE, LANES), lambda b, h, t: (b * (pad_rows // C_QTILE) + q_off + t, h)),
                  pl.BlockSpec((pad_rows, LANES), lambda b, h, t: (b, n_pairs + h)),
                  pl.BlockSpec((pad_rows, LANES), lambda b, h, t: (b, 2 * n_pairs + h)),
                  pl.BlockSpec((2, CHUNK, C_BAND), lambda b, h, t: (h, 0, 0))],
        out_specs=pl.BlockSpec((C_QTILE, LANES), lambda b, h, t: (b * nt + t, h)),
        out_shape=jax.ShapeDtypeStruct((batch * seq, D_MODEL), BF16),
        compiler_params=pltpu.CompilerParams(dimension_semantics=("parallel", "parallel", "arbitrary"),
                                             vmem_limit_bytes=VMEM_LIMIT),
        name="band_attn",
    )(qkv_pad, qkv_pad, qkv_pad, bias)


def _post_kernel(x_ref, p_ref, *rest, n_mix):
    mix_refs = rest[:n_mix]
    wo_ref, g_ref, w1_ref, w2_ref, wp_ref, pg_ref, wg_ref, o_ref = rest[n_mix:]
    mix = jnp.concatenate([mref[...] for mref in mix_refs], axis=1)
    x = x_ref[...] + _dot(mix, wo_ref[...])
    hb = _rms(x, g_ref[...]).astype(BF16)
    acc = None
    for j in range(D_FF // FF_CHUNK):
        sl = slice(j * FF_CHUNK, (j + 1) * FF_CHUNK)
        hid = jnp.maximum(_dot(hb, w1_ref[:, sl]), 0.0)
        part = _dot(hid * hid, w2_ref[sl, :])
        acc = part if acc is None else acc + part
    x = x + acc
    emb = _rms(_dot(p_ref[...], wp_ref[...]), pg_ref[...])
    o_ref[...] = x + emb * _sigmoid(_dot(x, wg_ref[...]))


def _post(x2d, p2d, mixes, wo, g, w1, w2, wp, pg, wg, tm):
    m = x2d.shape[0]
    const = lambda arr: pl.BlockSpec(arr.shape, lambda i: (0,) * arr.ndim, pipeline_mode=pl.Buffered(1))
    row = lambda w: pl.BlockSpec((tm, w), lambda i: (i, 0))
    weights = [wo, g, w1, w2, wp, pg, wg]
    return pl.pallas_call(
        functools.partial(_post_kernel, n_mix=len(mixes)),
        grid=(m // tm,),
        in_specs=[row(D_MODEL), row(p2d.shape[1])] + [row(mx.shape[1]) for mx in mixes] + [const(w) for w in weights],
        out_specs=row(D_MODEL),
        out_shape=jax.ShapeDtypeStruct((m, D_MODEL), F32),
        compiler_params=pltpu.CompilerParams(dimension_semantics=("parallel",), vmem_limit_bytes=VMEM_LIMIT),
        name="mix_out_mlp_ple",
    )(x2d, p2d, *mixes, *weights)


def _pad_to(w, rows=None, cols=None):
    r = (rows or w.shape[0]) - w.shape[0]
    c = (cols or w.shape[1]) - w.shape[1]
    return jnp.pad(w, ((0, r), (0, c)))


def _rel_bias_table(rel_bias):
    n_heads = rel_bias.shape[0]
    span = CHUNK + C_BAND - 1
    rel = (C_BAND - 1) - jnp.arange(span)
    f = rel_bias[:, jnp.clip(rel, -C_MAX_REL, C_MAX_REL) + C_MAX_REL]
    g = jnp.tile(jnp.pad(f, ((0, 0), (0, 1))), (1, CHUNK))[:, :CHUNK * span].reshape(n_heads, CHUNK, span)
    return g[:, :, CHUNK - 1:CHUNK - 1 + C_BAND]


def kernel(x, p, norm_mix_g, norm_ffn_g, even_w_in, rwkv_mu_proj, rwkv_mu_lora, rwkv_w0, rwkv_w1, rwkv_w2, rwkv_a0, rwkv_a1, rwkv_a2, rwkv_g1, rwkv_g2, rwkv_k_k, rwkv_k_a, rwkv_r_k, rwkv_ln_g, rwkv_ln_b, rwkv_v_mu, rwkv_v0, rwkv_v1, rwkv_v2, gdn_conv_w, gdn_a_log, gdn_dt_bias, gdn_norm_g, even_w_out, attn_w_qkv, attn_q_g, attn_k_g, attn_rel_bias, attn_w_out, mlp_w1, mlp_w2, ple_w_proj, ple_norm_g, ple_w_gate):
    batch, seq, _ = x.shape
    depth = p.shape[0]
    assert seq % 512 == 0 and x.shape[2] == D_MODEL
    tm_in, tm_post, tm_qkv = 256, 512, 512
    xs = x.reshape(batch * seq, D_MODEL)
    row1 = lambda vec: vec.reshape(1, -1)
    main_cols = 3 * A_WIDTH + 4 * B_WIDTH

    blk = 4 * C_HEAD
    ones_bd = (jnp.arange(blk)[:, None] // C_HEAD == jnp.arange(blk)[None, :] // C_HEAD).astype(BF16)

    v_first = None
    for i in range(depth):
        if i % 2 == 0:
            e = i // 2
            has_vres = e > 0
            win = even_w_in[e]
            mus = [rwkv_mu_lora[e, 0], rwkv_mu_lora[e, 1], rwkv_mu_lora[e, 2]]
            l1s = [rwkv_w1[e], rwkv_a1[e], rwkv_g1[e]]
            l2s = [rwkv_w2[e], rwkv_a2[e], rwkv_g2[e]]
            lbs = [rwkv_w0[e], rwkv_a0[e]]
            if has_vres:
                mus.append(rwkv_v_mu[e - 1])
                l1s.append(rwkv_v1[e - 1])
                l2s.append(rwkv_v2[e - 1])
                lbs.append(rwkv_v0[e - 1])
            l1 = jnp.stack([_pad_to(w, cols=A_LORA_PAD) for w in l1s]).astype(BF16)
            l2 = jnp.stack([_pad_to(w, rows=A_LORA_PAD) for w in l2s]).astype(BF16)
            outs = _even_in(xs, seq, row1(norm_mix_g[i]), win[:, :main_cols].astype(BF16),
                            _pad_to(win[:, main_cols:], cols=LANES).astype(BF16),
                            jnp.stack(mus), l1, l2, jnp.stack(lbs), has_vres, tm_in)
            rkv, qkvb, gate, ba, lw, a_lr, gg = outs[:7]
            par = jnp.stack([rwkv_k_k[e], rwkv_k_a[e], rwkv_r_k[e].reshape(-1), rwkv_ln_g[e], rwkv_ln_b[e]])
            y_a, v_first = _rwkv(rkv, lw, a_lr, gg, outs[7] if has_vres else None, v_first, par,
                                 rwkv_mu_proj[e].reshape(1, -1), batch, seq)
            hp = jnp.stack([_pad_to(jnp.pad(row1(gdn_a_log[e]), ((0, 0), (B_HEADS, 0))), cols=LANES)[0],
                            _pad_to(jnp.pad(row1(gdn_dt_bias[e]), ((0, 0), (B_HEADS, 0))), cols=LANES)[0]])
            y_b = _gdn(qkvb, gate, ba, gdn_conv_w[e], hp, row1(gdn_norm_g[e]), batch, seq)
            mixes, wo = [y_a, y_b], even_w_out[e]
        else:
            o = i // 2
            tile4 = lambda gvec: jnp.tile(gvec, blk // C_HEAD).reshape(1, blk)
            qkv_pad = _attn_qkv(xs, row1(norm_mix_g[i]), attn_w_qkv[o].astype(BF16), tile4(attn_q_g[o]),
                                tile4(attn_k_g[o]), ones_bd, batch, seq, tm_qkv)
            mixes, wo = [_attn(qkv_pad, _rel_bias_table(attn_rel_bias[o]), batch, seq)], attn_w_out[o]
        xs = _post(xs, p[i].reshape(batch * seq, -1), mixes, wo.astype(BF16), row1(norm_ffn_g[i]),
                   mlp_w1[i].astype(BF16), mlp_w2[i].astype(BF16), ple_w_proj[i].astype(BF16),
                   row1(ple_norm_g[i]), ple_w_gate[i].astype(BF16), tm_post)
    return xs.reshape(batch, seq, D_MODEL)
```

```python
import functools

import jax
import jax.numpy as jnp
from jax import lax
from jax.experimental import pallas as pl
from jax.experimental.pallas import tpu as pltpu

F32 = jnp.float32
BF16 = jnp.bfloat16

D_MODEL = 1024
CHUNK = 64
NORM_EPS = 1e-6
L2_EPS = 1e-6
A_WIDTH = 512
A_HEAD = 64
A_GN_EPS = 64e-5
A_LORA_PAD = 128
A_GROUP = 256
A_SCAN_ROWS = 4 * CHUNK
B_SCAN_ROWS = 4 * CHUNK
B_WIDTH = 512
B_HEADS = 4
B_HEAD = 128
B_CONV = 4
C_HEADS = 16
C_HEAD = 64
C_WINDOW = 8 * CHUNK
C_BAND = C_WINDOW + CHUNK
C_QTILE = 8 * CHUNK
C_MAX_REL = 256
D_FF = 4096
FF_CHUNK = 1024
LANES = 128
SUBLANES = 8
INV_BLOCK = 16
MASK_VALUE = -1e30
VMEM_LIMIT = 56 * 1024 * 1024


def _dot(a, b):
    return jnp.dot(a.astype(BF16), b.astype(BF16), preferred_element_type=F32)


def _dot_nt(a, b):
    return lax.dot_general(a.astype(BF16), b.astype(BF16), (((1,), (1,)), ((), ())),
                           preferred_element_type=F32)


def _dot_tn(a, b):
    return jnp.dot(a.astype(BF16).T, b.astype(BF16), preferred_element_type=F32)


def _rms(x, g, eps=NORM_EPS):
    return x * lax.rsqrt(jnp.mean(x * x, axis=-1, keepdims=True) + eps) * g


def _sigmoid(z):
    return 1.0 / (1.0 + jnp.exp(-z))


def _softplus(z):
    return jnp.maximum(z, 0.0) + jnp.log(1.0 + jnp.exp(-jnp.abs(z)))


def _shift_rows(x, prev_row):
    row = lax.broadcasted_iota(jnp.int32, x.shape, 0)
    return jnp.where(row == 0, prev_row, pltpu.roll(x, 1, 0))


def _cumsum_rows(x):
    row = lax.broadcasted_iota(jnp.int32, x.shape, 0)
    step = 1
    while step < x.shape[0]:
        x = x + jnp.where(row >= step, pltpu.roll(x, step, 0), 0.0)
        step *= 2
    return x


def _stack_heads(x, n_heads):
    head_w = x.shape[1] // n_heads
    lane_head = lax.broadcasted_iota(jnp.int32, x.shape, 1) >> (head_w.bit_length() - 1)
    zero = jnp.zeros_like(x)
    return jnp.concatenate([jnp.where(lane_head == h, x, zero) for h in range(n_heads)], axis=0)


def _unstack_heads(xs, n_heads):
    out = xs[0:CHUNK]
    for h in range(1, n_heads):
        out = out + xs[h * CHUNK:(h + 1) * CHUNK]
    return out


def _block_masks(n_heads):
    n = n_heads * CHUNK
    r = lax.broadcasted_iota(jnp.int32, (n, n), 0)
    c = lax.broadcasted_iota(jnp.int32, (n, n), 1)
    same_head = (r ^ c) < CHUNK
    delta = jnp.where(same_head, r - c, -1)
    return same_head, delta >= 0, delta > 0, (r ^ c) < INV_BLOCK


def _unit_lower_inverses(n_mats, same_block):
    nds = [jnp.where(same_block, n, 0.0) for n in n_mats]
    nos = [n - nd for n, nd in zip(n_mats, nds)]
    tds, pws = nds, nds
    for _ in range(3):
        pws = [_dot(pw, pw) for pw in pws]
        tds = [td + pw + _dot(td, pw) for td, pw in zip(tds, pws)]
    es = [no + _dot(td, no) for td, no in zip(tds, nos)]
    e2s = [_dot(e, e) for e in es]
    fs = [e + e2 + _dot(e, e2) for e, e2 in zip(es, e2s)]
    return [f + td + _dot(f, td) for f, td in zip(fs, tds)]


def _seg_sum(x):
    seg_lo = lax.broadcasted_iota(jnp.int32, (x.shape[0], LANES), 1) < LANES // 2
    outs = []
    for j in range(x.shape[1] // LANES):
        blk = x[:, j * LANES:(j + 1) * LANES]
        s_lo = jnp.sum(jnp.where(seg_lo, blk, 0.0), axis=-1, keepdims=True)
        s_hi = jnp.sum(jnp.where(seg_lo, 0.0, blk), axis=-1, keepdims=True)
        outs.append(jnp.where(seg_lo, s_lo, s_hi))
    return jnp.concatenate(outs, axis=1)


def _even_in_kernel(x_ref, halo_ref, g_ref, win_ref, wtail_ref, mu_ref, l1_ref, l2_ref, lb_ref,
                    rkv_ref, qkvb_ref, gate_ref, ba_ref, lw_ref, a_ref, gg_ref, *vg_ref,
                    tiles_per_seq):
    i = pl.program_id(0)
    g = g_ref[...]
    h = _rms(x_ref[...], g)
    prev = _rms(halo_ref[...], g)[SUBLANES - 1:SUBLANES, :]
    prev = jnp.where(i % tiles_per_seq == 0, 0.0, prev)
    dh = _shift_rows(h, prev) - h
    hb = h.astype(BF16)
    rkv_ref[...] = _dot(hb, win_ref[:, 0:3 * A_WIDTH])
    qkvb_ref[...] = _dot(hb, win_ref[:, 3 * A_WIDTH:3 * A_WIDTH + 3 * B_WIDTH])
    gate_ref[...] = _dot(hb, win_ref[:, 3 * A_WIDTH + 3 * B_WIDTH:])
    ba_ref[...] = _dot(hb, wtail_ref[...])

    n_lora = 4 if vg_ref else 3
    mid = [_dot(h + dh * mu_ref[j:j + 1, :], l1_ref[j]) for j in range(n_lora)]
    mid[0] = jnp.tanh(mid[0])
    mid[2] = _sigmoid(mid[2])
    up = [_dot(mid[j], l2_ref[j]) for j in range(n_lora)]
    w_log = -_softplus(-(lb_ref[0:1, :] + up[0])) - 0.5
    lw_ref[...] = -jnp.exp(w_log)
    a_ref[...] = _sigmoid(lb_ref[1:2, :] + up[1])
    gg_ref[...] = up[2]
    if vg_ref:
        vg_ref[0][...] = _sigmoid(lb_ref[2:3, :] + up[3])


def _even_in(x2d, seq, g, win, wtail, mu, l1, l2, lb, has_vres, tm):
    m = x2d.shape[0]
    n_out = 8 if has_vres else 7
    widths = [3 * A_WIDTH, 3 * B_WIDTH, B_WIDTH, LANES, A_WIDTH, A_WIDTH, A_WIDTH, A_WIDTH][:n_out]
    const = lambda a: pl.BlockSpec(a.shape, lambda i: (0,) * a.ndim)
    return pl.pallas_call(
        functools.partial(_even_in_kernel, tiles_per_seq=seq // tm),
        grid=(m // tm,),
        in_specs=[pl.BlockSpec((tm, D_MODEL), lambda i: (i, 0)),
                  pl.BlockSpec((SUBLANES, D_MODEL), lambda i: (jnp.maximum(i * (tm // SUBLANES) - 1, 0), 0)),
                  const(g), const(win), const(wtail), const(mu), const(l1), const(l2), const(lb)],
        out_specs=[pl.BlockSpec((tm, w), lambda i: (i, 0)) for w in widths],
        out_shape=[jax.ShapeDtypeStruct((m, w), F32) for w in widths],
        compiler_params=pltpu.CompilerParams(dimension_semantics=("parallel",), vmem_limit_bytes=VMEM_LIMIT),
        name="even_in",
    )(x2d, x2d, g, win, wtail, mu, l1, l2, lb)


def _rwkv_kernel(rkv_ref, lw_ref, a_ref, gg_ref, *rest, has_vres):
    if has_vres:
        vg_ref, vfirst_ref, par_ref, mu_ref, y_ref, state_ref, tail_ref = rest
    else:
        par_ref, mu_ref, y_ref, vfirst_out_ref, state_ref, tail_ref = rest
    c = pl.program_id(1)

    @pl.when(c == 0)
    def _():
        state_ref[...] = jnp.zeros_like(state_ref)
        tail_ref[...] = jnp.zeros_like(tail_ref)

    rkv = rkv_ref[...]
    rkv_prev = _shift_rows(rkv, tail_ref[SUBLANES - 1:SUBLANES, :])
    tail_ref[...] = rkv[rkv.shape[0] - SUBLANES:, :]
    rkv = rkv + (rkv_prev - rkv) * mu_ref[...]
    r = rkv[:, 0:A_WIDTH]
    k = rkv[:, A_WIDTH:2 * A_WIDTH]
    v = rkv[:, 2 * A_WIDTH:]
    k_k, k_a, r_k, ln_g, ln_b = (par_ref[j:j + 1, :] for j in range(5))
    a = a_ref[...]
    if has_vres:
        v = v + (vfirst_ref[...] - v) * vg_ref[...]
    else:
        vfirst_out_ref[...] = v

    lw_all = lw_ref[...]
    kk_all = k * k_k
    kk_all = kk_all * lax.rsqrt(_seg_sum(kk_all * kk_all) + L2_EPS)
    k2_all = k * (1.0 + (a - 1.0) * k_a)
    b_all = kk_all * a
    bonus = _seg_sum(r * k2_all * r_k) * v
    v_bf_all = v.astype(BF16)

    heads = A_GROUP // A_HEAD
    n_groups = A_WIDTH // A_GROUP
    n_rows = heads * CHUNK
    same_head, incl, strict, same_block = _block_masks(heads)
    incl2 = jnp.concatenate([incl, incl], axis=1)
    n_chunks = rkv.shape[0] // CHUNK
    group_sl = [slice(g * A_GROUP, (g + 1) * A_GROUP) for g in range(n_groups)]

    probs = []
    for cc in range(n_chunks):
        rows = slice(cc * CHUNK, (cc + 1) * CHUNK)
        lw, kk, k2, b = lw_all[rows], kk_all[rows], k2_all[rows], b_all[rows]
        cw = _cumsum_rows(lw)
        cw_last = cw[CHUNK - 1:CHUNK, :]
        e_neg = jnp.exp(-cw)
        e_last = jnp.exp(cw_last - cw)
        r_t = (r[rows] * jnp.exp(cw)).astype(BF16)
        a_t = (-kk * jnp.exp(cw - lw)).astype(BF16)
        b_t = (b * e_neg).astype(BF16)
        k_t = (k2 * e_neg).astype(BF16)
        b_hat = (b * e_last).astype(BF16)
        k_hat = (k2 * e_last).astype(BF16)
        for sl in group_sl:
            lhs = jnp.concatenate([_stack_heads(a_t[:, sl], heads), _stack_heads(r_t[:, sl], heads)], axis=0)
            rhs = jnp.concatenate([b_t[:, sl]] * heads + [k_t[:, sl]] * heads, axis=0)
            probs.append(dict(
                rows=rows, sl=sl, gram=_dot_nt(lhs, rhs),
                ar=jnp.concatenate([a_t[:, sl], r_t[:, sl]], axis=0),
                bk_hat=jnp.concatenate([b_hat[:, sl], k_hat[:, sl]], axis=0),
                v=v_bf_all[rows, sl], w_last=jnp.exp(cw_last[:, sl])))
    minvs = _unit_lower_inverses([jnp.where(strict, pr["gram"][0:n_rows, 0:n_rows], 0.0) for pr in probs],
                                 same_block)

    states = [state_ref[g] for g in range(n_groups)]
    for cc in range(n_chunks):
        prs = probs[cc * n_groups:(cc + 1) * n_groups]
        mis = minvs[cc * n_groups:(cc + 1) * n_groups]
        from_state = [_dot_nt(pr["ar"], st) for pr, st in zip(prs, states)]
        v_s = [_stack_heads(pr["v"], heads) for pr in prs]
        xs = [fs[0:CHUNK] + _unstack_heads(_dot(jnp.where(strict, pr["gram"][0:n_rows, n_rows:], 0.0), vs), heads)
              for fs, pr, vs in zip(from_state, prs, v_s)]
        us = [x + _unstack_heads(_dot(mi, _stack_heads(x.astype(BF16), heads)), heads) for x, mi in zip(xs, mis)]
        u_bf = [u.astype(BF16) for u in us]
        ys = [fs[CHUNK:] + _unstack_heads(_dot(jnp.where(incl2, pr["gram"][n_rows:, :], 0.0),
                                               jnp.concatenate([_stack_heads(ub, heads), vs], axis=0)), heads)
              for fs, pr, ub, vs in zip(from_state, prs, u_bf, v_s)]
        upds = [_dot_tn(jnp.concatenate([ub, pr["v"]], axis=0), pr["bk_hat"]) for ub, pr in zip(u_bf, prs)]
        states = [st * pr["w_last"] + jnp.where(same_head, upd, 0.0) for st, pr, upd in zip(states, prs, upds)]
        for pr, y in zip(prs, ys):
            rows, sl = pr["rows"], pr["sl"]
            mean = _seg_sum(y) * (1.0 / A_HEAD)
            yc = y - mean
            var = _seg_sum(yc * yc) * (1.0 / A_HEAD)
            yn = yc * lax.rsqrt(var + A_GN_EPS) * ln_g[:, sl] + ln_b[:, sl]
            y_ref[rows, sl] = ((yn + bonus[rows, sl]) * gg_ref[rows, sl]).astype(y_ref.dtype)
    for g in range(n_groups):
        state_ref[g] = states[g]


def _rwkv(rkv, lw, a, gg, vg, vfirst, par, mu, batch, seq):
    m = rkv.shape[0]
    nc = seq // A_SCAN_ROWS
    has_vres = vg is not None
    row = lambda w: pl.BlockSpec((A_SCAN_ROWS, w), lambda b, c: (b * nc + c, 0))
    const = lambda arr: pl.BlockSpec(arr.shape, lambda b, c: (0,) * arr.ndim)
    ins = [rkv, lw, a, gg] + ([vg, vfirst] if has_vres else []) + [par, mu]
    in_specs = [row(3 * A_WIDTH)] + [row(A_WIDTH)] * (5 if has_vres else 3) + [const(par), const(mu)]
    n_out = 1 if has_vres else 2
    outs = pl.pallas_call(
        functools.partial(_rwkv_kernel, has_vres=has_vres),
        grid=(batch, nc),
        in_specs=in_specs,
        out_specs=[row(A_WIDTH)] * n_out,
        out_shape=[jax.ShapeDtypeStruct((m, A_WIDTH), dt) for dt in (BF16, F32)[:n_out]],
        scratch_shapes=[pltpu.VMEM((A_WIDTH // A_GROUP, A_GROUP, A_GROUP), F32),
                        pltpu.VMEM((SUBLANES, 3 * A_WIDTH), F32)],
        compiler_params=pltpu.CompilerParams(dimension_semantics=("arbitrary", "arbitrary"),
                                             vmem_limit_bytes=VMEM_LIMIT),
        name="rwkv7_scan",
    )(*ins)
    return outs if not has_vres else (outs[0], vfirst)


def _gdn_kernel(qkv_ref, gate_ref, ba_ref, conv_ref, hp_ref, ng_ref, y_ref, state_ref, zz_ref):
    c = pl.program_id(1)

    @pl.when(c == 0)
    def _():
        state_ref[...] = jnp.zeros_like(state_ref)
        zz_ref[0:SUBLANES, :] = jnp.zeros((SUBLANES, 3 * B_WIDTH), F32)

    z = qkv_ref[...]
    zz_ref[SUBLANES:, :] = z
    zz = zz_ref[...]
    zz_ref[0:SUBLANES, :] = z[z.shape[0] - SUBLANES:, :]
    conv = zz[SUBLANES:, :] * conv_ref[B_CONV - 1:B_CONV, :]
    for j in range(B_CONV - 1):
        conv = conv + pltpu.roll(zz, B_CONV - 1 - j, 0)[SUBLANES:, :] * conv_ref[j:j + 1, :]
    qkv = conv * _sigmoid(conv)

    def per_head(tile, lane0):
        return jnp.concatenate([jnp.broadcast_to(tile[:, lane0 + h:lane0 + h + 1], (CHUNK, B_HEAD))
                                for h in range(B_HEADS)], axis=1)

    def l2n(t):
        return jnp.concatenate(
            [t[:, h * B_HEAD:(h + 1) * B_HEAD]
             * lax.rsqrt(jnp.sum(jnp.square(t[:, h * B_HEAD:(h + 1) * B_HEAD]), axis=-1, keepdims=True) + L2_EPS)
             for h in range(B_HEADS)], axis=1)

    q_all = l2n(qkv[:, 0:B_WIDTH]) * (B_HEAD ** -0.5)
    k_all = l2n(qkv[:, B_WIDTH:2 * B_WIDTH])
    v_all = qkv[:, 2 * B_WIDTH:]
    ba_all = ba_ref[...]
    beta_all = _sigmoid(ba_all)
    g_step_all = -jnp.exp(hp_ref[0:1, :]) * _softplus(ba_all + hp_ref[1:2, :])

    same_head, incl, strict, same_block = _block_masks(B_HEADS)
    n_rows = B_HEADS * CHUNK
    probs = []
    for cc in range(z.shape[0] // CHUNK):
        rows = slice(cc * CHUNK, (cc + 1) * CHUNK)
        q, k, v = q_all[rows], k_all[rows], v_all[rows]
        gc = _cumsum_rows(g_step_all[rows])
        gc_t = gc.T
        beta_f = per_head(beta_all[rows], 0)
        g_col = per_head(gc, B_HEADS)
        g_last = g_col[CHUNK - 1:CHUNK, :]
        e_g = jnp.exp(g_col)
        kb = k * beta_f
        g_col_s = jnp.concatenate([gc[:, B_HEADS + h:B_HEADS + h + 1] for h in range(B_HEADS)], axis=0)
        g_row_s = jnp.concatenate([gc_t[B_HEADS + h:B_HEADS + h + 1, :] for h in range(B_HEADS)], axis=1)
        decay = jnp.where(incl, jnp.exp(jnp.where(incl, g_col_s - g_row_s, 0.0)), 0.0)
        lhs = jnp.concatenate([_stack_heads(kb.astype(BF16), B_HEADS), _stack_heads(q.astype(BF16), B_HEADS)],
                              axis=0)
        gram = _dot_nt(lhs, jnp.concatenate([k.astype(BF16)] * B_HEADS, axis=0))
        probs.append(dict(rows=rows, amat=jnp.where(strict, gram[0:n_rows] * decay, 0.0),
                          qk=gram[n_rows:] * decay, vb=v * beta_f, kbg=kb * e_g, qe=q * e_g,
                          kd=(k * jnp.exp(g_last - g_col)).astype(BF16), s_decay=jnp.exp(g_last)))
    tinvs = _unit_lower_inverses([-pr["amat"] for pr in probs], same_block)
    solved = [_dot(tinv, jnp.concatenate([_stack_heads(pr["vb"].astype(BF16), B_HEADS),
                                          _stack_heads(pr["kbg"].astype(BF16), B_HEADS)], axis=1))
              for tinv, pr in zip(tinvs, probs)]

    states = [state_ref[hd] for hd in range(B_HEADS)]
    for pr, sol in zip(probs, solved):
        rows, qk = pr["rows"], pr["qk"]
        u = pr["vb"] + _unstack_heads(sol[:, 0:B_WIDTH], B_HEADS)
        w = pr["kbg"] + _unstack_heads(sol[:, B_WIDTH:], B_HEADS)
        wq = jnp.concatenate([w, pr["qe"]], axis=0).astype(BF16)
        kd, g_last_e = pr["kd"], pr["s_decay"]
        v_new, q_state = [], []
        for hd in range(B_HEADS):
            sl = slice(hd * B_HEAD, (hd + 1) * B_HEAD)
            from_state = _dot(wq[:, sl], states[hd])
            vn = u[:, sl] - from_state[0:CHUNK]
            states[hd] = states[hd] * g_last_e[:, sl] + _dot_tn(kd[:, sl], vn)
            v_new.append(vn)
            q_state.append(from_state[CHUNK:])
        v_new = jnp.concatenate(v_new, axis=1)
        o = (jnp.concatenate(q_state, axis=1)
             + _unstack_heads(_dot(qk, _stack_heads(v_new.astype(BF16), B_HEADS)), B_HEADS))
        for hd in range(B_HEADS):
            sl = slice(hd * B_HEAD, (hd + 1) * B_HEAD)
            gate = gate_ref[rows, sl]
            y_ref[rows, sl] = (_rms(o[:, sl], ng_ref[...]) * (gate * _sigmoid(gate))).astype(y_ref.dtype)
    for hd in range(B_HEADS):
        state_ref[hd] = states[hd]


def _gdn(qkvb, gate, ba, conv_w, hp, norm_g, batch, seq):
    m = qkvb.shape[0]
    nc = seq // B_SCAN_ROWS
    row = lambda w: pl.BlockSpec((B_SCAN_ROWS, w), lambda b, c: (b * nc + c, 0))
    const = lambda arr: pl.BlockSpec(arr.shape, lambda b, c: (0,) * arr.ndim)
    return pl.pallas_call(
        _gdn_kernel,
        grid=(batch, nc),
        in_specs=[row(3 * B_WIDTH), row(B_WIDTH), row(LANES), const(conv_w), const(hp), const(norm_g)],
        out_specs=row(B_WIDTH),
        out_shape=jax.ShapeDtypeStruct((m, B_WIDTH), BF16),
        scratch_shapes=[pltpu.VMEM((B_HEADS, B_HEAD, B_HEAD), F32),
                        pltpu.VMEM((B_SCAN_ROWS + SUBLANES, 3 * B_WIDTH), F32)],
        compiler_params=pltpu.CompilerParams(dimension_semantics=("arbitrary", "arbitrary"),
                                             vmem_limit_bytes=VMEM_LIMIT),
        name="gdn_scan",
    )(qkvb, gate, ba, conv_w, hp, norm_g)


def _head_rms_many(xs, ones_bd, gs):
    sqs = [x * x for x in xs]
    his = [sq.astype(BF16) for sq in sqs]
    los = [(sq - hi.astype(F32)).astype(BF16) for sq, hi in zip(sqs, his)]
    sums = [jnp.dot(hi, ones_bd, preferred_element_type=F32) for hi in his]
    sums = [s + jnp.dot(lo, ones_bd, preferred_element_type=F32) for s, lo in zip(sums, los)]
    return [x * lax.rsqrt(s * (1.0 / C_HEAD) + NORM_EPS) * g for x, s, g in zip(xs, sums, gs)]


def _attn_qkv_kernel(x_ref, g_ref, w_ref, qg_ref, kg_ref, ones_ref, o_ref):
    t = pl.program_id(1)

    @pl.when(t == 0)
    def _():
        o_ref[...] = jnp.zeros_like(o_ref)

    @pl.when(t > 0)
    def _():
        hb = _rms(x_ref[...], g_ref[...]).astype(BF16)
        ones_bd = ones_ref[...]
        blk = ones_bd.shape[0]
        o_ref[:, 2 * D_MODEL:] = _dot(hb, w_ref[:, 2 * D_MODEL:]).astype(o_ref.dtype)
        q_gain = qg_ref[...] * (C_HEAD ** -0.5)
        per_pass = D_MODEL // blk // 2
        for half in range(2):
            cols = [slice(c * blk, (c + 1) * blk) for c in range(half * per_pass, (half + 1) * per_pass)]
            cols = cols + [slice(D_MODEL + sl.start, D_MODEL + sl.stop) for sl in cols]
            gains = [q_gain] * (len(cols) // 2) + [kg_ref[...]] * (len(cols) // 2)
            normed = _head_rms_many([_dot(hb, w_ref[:, sl]) for sl in cols], ones_bd, gains)
            for sl, val in zip(cols, normed):
                o_ref[:, sl] = val.astype(o_ref.dtype)


def _attn_qkv(x2d, g, w, qg, kg, ones_bd, batch, seq, tm):
    tiles = seq // tm
    pad_tiles = C_WINDOW // tm
    const = lambda arr: pl.BlockSpec(arr.shape, lambda b, t: (0,) * arr.ndim)
    return pl.pallas_call(
        _attn_qkv_kernel,
        grid=(batch, tiles + pad_tiles),
        in_specs=[pl.BlockSpec((tm, D_MODEL), lambda b, t: (b * tiles + jnp.maximum(t - pad_tiles, 0), 0)),
                  const(g), const(w), const(qg), const(kg), const(ones_bd)],
        out_specs=pl.BlockSpec((tm, 3 * D_MODEL), lambda b, t: (b * (tiles + pad_tiles) + t, 0)),
        out_shape=jax.ShapeDtypeStruct((batch * (seq + C_WINDOW), 3 * D_MODEL), BF16),
        compiler_params=pltpu.CompilerParams(dimension_semantics=("parallel", "arbitrary"),
                                             vmem_limit_bytes=VMEM_LIMIT),
        name="attn_qkv",
    )(x2d, g, w, qg, kg, ones_bd)


def _attn_kernel(q_ref, k_ref, v_ref, bias_ref, o_ref):
    t = pl.program_id(2)
    lane_lo = lax.broadcasted_iota(jnp.int32, (CHUNK, LANES), 1) < C_HEAD
    key_idx = lax.broadcasted_iota(jnp.int32, (CHUNK, C_BAND), 1)
    zero = jnp.zeros((CHUNK, LANES), q_ref.dtype)
    scores, windows = [], []
    for cc in range(C_QTILE // CHUNK):
        first_key = pl.multiple_of(t * C_QTILE + cc * CHUNK, CHUNK)
        kw = k_ref[pl.ds(first_key, C_BAND), :]
        windows.append(v_ref[pl.ds(first_key, C_BAND), :])
        q = q_ref[cc * CHUNK:(cc + 1) * CHUNK, :]
        valid = key_idx >= C_WINDOW - first_key
        for head in range(2):
            keep = lane_lo if head == 0 else jnp.logical_not(lane_lo)
            s = _dot_nt(jnp.where(keep, q, zero), kw) + bias_ref[head]
            scores.append(jnp.where(valid, s, MASK_VALUE))
    probs = [jnp.exp(s - jnp.max(s, axis=-1, keepdims=True)) for s in scores]
    denoms = [jnp.sum(p, axis=-1, keepdims=True) for p in probs]
    outs = [_dot(p, windows[j // 2]) / d for j, (p, d) in enumerate(zip(probs, denoms))]
    for cc in range(C_QTILE // CHUNK):
        o_ref[cc * CHUNK:(cc + 1) * CHUNK, :] = jnp.where(lane_lo, outs[2 * cc], outs[2 * cc + 1]).astype(o_ref.dtype)


def _attn(qkv_pad, bias, batch, seq):
    nt = seq // C_QTILE
    pad_rows = seq + C_WINDOW
    n_pairs = D_MODEL // LANES
    q_off = C_WINDOW // C_QTILE
    return pl.pallas_call(
        _attn_kernel,
        grid=(batch, n_pairs, nt),
        in_specs=[pl.BlockSpec((C_QTILE, LANES), lambda b, h, t: (b * (pad_rows // C_QTILE) + q_off + t, h)),
                  pl.BlockSpec((pad_rows, LANES), lambda b, h, t: (b, n_pairs + h)),
                  pl.BlockSpec((pad_rows, LANES), lambda b, h, t: (b, 2 * n_pairs + h)),
                  pl.BlockSpec((2, CHUNK, C_BAND), lambda b, h, t: (h, 0, 0))],
        out_specs=pl.BlockSpec((C_QTILE, LANES), lambda b, h, t: (b * nt + t, h)),
        out_shape=jax.ShapeDtypeStruct((batch * seq, D_MODEL), BF16),
        compiler_params=pltpu.CompilerParams(dimension_semantics=("parallel", "parallel", "arbitrary"),
                                             vmem_limit_bytes=VMEM_LIMIT),
        name="band_attn",
    )(qkv_pad, qkv_pad, qkv_pad, bias)


def _post_kernel(x_ref, p_ref, *rest, n_mix):
    mix_refs = rest[:n_mix]
    wo_ref, g_ref, w1_ref, w2_ref, wp_ref, pg_ref, wg_ref, o_ref = rest[n_mix:]
    mix = jnp.concatenate([mref[...] for mref in mix_refs], axis=1)
    x = x_ref[...] + _dot(mix, wo_ref[...])
    hb = _rms(x, g_ref[...]).astype(BF16)
    acc = None
    for j in range(D_FF // FF_CHUNK):
        sl = slice(j * FF_CHUNK, (j + 1) * FF_CHUNK)
        hid = jnp.maximum(_dot(hb, w1_ref[:, sl]), 0.0)
        part = _dot(hid * hid, w2_ref[sl, :])
        acc = part if acc is None else acc + part
    x = x + acc
    emb = _rms(_dot(p_ref[...], wp_ref[...]), pg_ref[...])
    o_ref[...] = x + emb * _sigmoid(_dot(x, wg_ref[...]))


def _post(x2d, p2d, mixes, wo, g, w1, w2, wp, pg, wg, tm):
    m = x2d.shape[0]
    const = lambda arr: pl.BlockSpec(arr.shape, lambda i: (0,) * arr.ndim, pipeline_mode=pl.Buffered(1))
    row = lambda w: pl.BlockSpec((tm, w), lambda i: (i, 0))
    weights = [wo, g, w1, w2, wp, pg, wg]
    return pl.pallas_call(
        functools.partial(_post_kernel, n_mix=len(mixes)),
        grid=(m // tm,),
        in_specs=[row(D_MODEL), row(p2d.shape[1])] + [row(mx.shape[1]) for mx in mixes] + [const(w) for w in weights],
        out_specs=row(D_MODEL),
        out_shape=jax.ShapeDtypeStruct((m, D_MODEL), F32),
        compiler_params=pltpu.CompilerParams(dimension_semantics=("parallel",), vmem_limit_bytes=VMEM_LIMIT),
        name="mix_out_mlp_ple",
    )(x2d, p2d, *mixes, *weights)


def _pad_to(w, rows=None, cols=None):
    r = (rows or w.shape[0]) - w.shape[0]
    c = (cols or w.shape[1]) - w.shape[1]
    return jnp.pad(w, ((0, r), (0, c)))


def _rel_bias_table(rel_bias):
    n_heads = rel_bias.shape[0]
    span = CHUNK + C_BAND - 1
    rel = (C_BAND - 1) - jnp.arange(span)
    f = rel_bias[:, jnp.clip(rel, -C_MAX_REL, C_MAX_REL) + C_MAX_REL]
    g = jnp.tile(jnp.pad(f, ((0, 0), (0, 1))), (1, CHUNK))[:, :CHUNK * span].reshape(n_heads, CHUNK, span)
    return g[:, :, CHUNK - 1:CHUNK - 1 + C_BAND]


def kernel(x, p, norm_mix_g, norm_ffn_g, even_w_in, rwkv_mu_proj, rwkv_mu_lora, rwkv_w0, rwkv_w1, rwkv_w2, rwkv_a0, rwkv_a1, rwkv_a2, rwkv_g1, rwkv_g2, rwkv_k_k, rwkv_k_a, rwkv_r_k, rwkv_ln_g, rwkv_ln_b, rwkv_v_mu, rwkv_v0, rwkv_v1, rwkv_v2, gdn_conv_w, gdn_a_log, gdn_dt_bias, gdn_norm_g, even_w_out, attn_w_qkv, attn_q_g, attn_k_g, attn_rel_bias, attn_w_out, mlp_w1, mlp_w2, ple_w_proj, ple_norm_g, ple_w_gate):
    batch, seq, _ = x.shape
    depth = p.shape[0]
    assert seq % 512 == 0 and x.shape[2] == D_MODEL
    tm_in, tm_post, tm_qkv = 256, 512, 512
    xs = x.reshape(batch * seq, D_MODEL)
    row1 = lambda vec: vec.reshape(1, -1)
    main_cols = 3 * A_WIDTH + 4 * B_WIDTH

    blk = 4 * C_HEAD
    ones_bd = (jnp.arange(blk)[:, None] // C_HEAD == jnp.arange(blk)[None, :] // C_HEAD).astype(BF16)

    v_first = None
    for i in range(depth):
        if i % 2 == 0:
            e = i // 2
            has_vres = e > 0
            win = even_w_in[e]
            mus = [rwkv_mu_lora[e, 0], rwkv_mu_lora[e, 1], rwkv_mu_lora[e, 2]]
            l1s = [rwkv_w1[e], rwkv_a1[e], rwkv_g1[e]]
            l2s = [rwkv_w2[e], rwkv_a2[e], rwkv_g2[e]]
            lbs = [rwkv_w0[e], rwkv_a0[e]]
            if has_vres:
                mus.append(rwkv_v_mu[e - 1])
                l1s.append(rwkv_v1[e - 1])
                l2s.append(rwkv_v2[e - 1])
                lbs.append(rwkv_v0[e - 1])
            l1 = jnp.stack([_pad_to(w, cols=A_LORA_PAD) for w in l1s]).astype(BF16)
            l2 = jnp.stack([_pad_to(w, rows=A_LORA_PAD) for w in l2s]).astype(BF16)
            outs = _even_in(xs, seq, row1(norm_mix_g[i]), win[:, :main_cols].astype(BF16),
                            _pad_to(win[:, main_cols:], cols=LANES).astype(BF16),
                            jnp.stack(mus), l1, l2, jnp.stack(lbs), has_vres, tm_in)
            rkv, qkvb, gate, ba, lw, a_lr, gg = outs[:7]
            par = jnp.stack([rwkv_k_k[e], rwkv_k_a[e], rwkv_r_k[e].reshape(-1), rwkv_ln_g[e], rwkv_ln_b[e]])
            y_a, v_first = _rwkv(rkv, lw, a_lr, gg, outs[7] if has_vres else None, v_first, par,
                                 rwkv_mu_proj[e].reshape(1, -1), batch, seq)
            hp = jnp.stack([_pad_to(jnp.pad(row1(gdn_a_log[e]), ((0, 0), (B_HEADS, 0))), cols=LANES)[0],
                            _pad_to(jnp.pad(row1(gdn_dt_bias[e]), ((0, 0), (B_HEADS, 0))), cols=LANES)[0]])
            y_b = _gdn(qkvb, gate, ba, gdn_conv_w[e], hp, row1(gdn_norm_g[e]), batch, seq)
            mixes, wo = [y_a, y_b], even_w_out[e]
        else:
            o = i // 2
            tile4 = lambda gvec: jnp.tile(gvec, blk // C_HEAD).reshape(1, blk)
            qkv_pad = _attn_qkv(xs, row1(norm_mix_g[i]), attn_w_qkv[o].astype(BF16), tile4(attn_q_g[o]),
                                tile4(attn_k_g[o]), ones_bd, batch, seq, tm_qkv)
            mixes, wo = [_attn(qkv_pad, _rel_bias_table(attn_rel_bias[o]), batch, seq)], attn_w_out[o]
        xs = _post(xs, p[i].reshape(batch * seq, -1), mixes, wo.astype(BF16), row1(norm_ffn_g[i]),
                   mlp_w1[i].astype(BF16), mlp_w2[i].astype(BF16), ple_w_proj[i].astype(BF16),
                   row1(ple_norm_g[i]), ple_w_gate[i].astype(BF16), tm_post)
    return xs.reshape(batch, seq, D_MODEL)
```

```python
import functools

import jax
import jax.numpy as jnp
from jax import lax
from jax.experimental import pallas as pl
from jax.experimental.pallas import tpu as pltpu

F32 = jnp.float32
BF16 = jnp.bfloat16

D_MODEL = 1024
CHUNK = 64
NORM_EPS = 1e-6
L2_EPS = 1e-6
A_WIDTH = 512
A_HEAD = 64
A_GN_EPS = 64e-5
A_LORA_PAD = 128
A_GROUP = 256
A_SCAN_ROWS = 4 * CHUNK
B_SCAN_ROWS = 4 * CHUNK
B_WIDTH = 512
B_HEADS = 4
B_HEAD = 128
B_CONV = 4
C_HEADS = 16
C_HEAD = 64
C_WINDOW = 8 * CHUNK
C_BAND = C_WINDOW + CHUNK
C_QTILE = 8 * CHUNK
C_MAX_REL = 256
D_FF = 4096
FF_CHUNK = 1024
LANES = 128
SUBLANES = 8
INV_BLOCK = 16
MASK_VALUE = -1e30
VMEM_LIMIT = 56 * 1024 * 1024


def _dot(a, b):
    return jnp.dot(a.astype(BF16), b.astype(BF16), preferred_element_type=F32)


def _dot_nt(a, b):
    return lax.dot_general(a.astype(BF16), b.astype(BF16), (((1,), (1,)), ((), ())),
                           preferred_element_type=F32)


def _dot_tn(a, b):
    return jnp.dot(a.astype(BF16).T, b.astype(BF16), preferred_element_type=F32)


def _rms(x, g, eps=NORM_EPS):
    return x * lax.rsqrt(jnp.mean(x * x, axis=-1, keepdims=True) + eps) * g


def _sigmoid(z):
    return 1.0 / (1.0 + jnp.exp(-z))


def _softplus(z):
    return jnp.maximum(z, 0.0) + jnp.log(1.0 + jnp.exp(-jnp.abs(z)))


def _shift_rows(x, prev_row):
    row = lax.broadcasted_iota(jnp.int32, x.shape, 0)
    return jnp.where(row == 0, prev_row, pltpu.roll(x, 1, 0))


def _cumsum_rows(x):
    row = lax.broadcasted_iota(jnp.int32, x.shape, 0)
    step = 1
    while step < x.shape[0]:
        x = x + jnp.where(row >= step, pltpu.roll(x, step, 0), 0.0)
        step *= 2
    return x


def _stack_heads(x, n_heads):
    head_w = x.shape[1] // n_heads
    lane_head = lax.broadcasted_iota(jnp.int32, x.shape, 1) >> (head_w.bit_length() - 1)
    zero = jnp.zeros_like(x)
    return jnp.concatenate([jnp.where(lane_head == h, x, zero) for h in range(n_heads)], axis=0)


def _unstack_heads(xs, n_heads):
    out = xs[0:CHUNK]
    for h in range(1, n_heads):
        out = out + xs[h * CHUNK:(h + 1) * CHUNK]
    return out


def _block_masks(n_heads):
    n = n_heads * CHUNK
    r = lax.broadcasted_iota(jnp.int32, (n, n), 0)
    c = lax.broadcasted_iota(jnp.int32, (n, n), 1)
    same_head = (r ^ c) < CHUNK
    delta = jnp.where(same_head, r - c, -1)
    return same_head, delta >= 0, delta > 0, (r ^ c) < INV_BLOCK


def _unit_lower_inverses(n_mats, same_block):
    nds = [jnp.where(same_block, n, 0.0) for n in n_mats]
    nos = [n - nd for n, nd in zip(n_mats, nds)]
    tds, pws = nds, nds
    for _ in range(3):
        pws = [_dot(pw, pw) for pw in pws]
        tds = [td + pw + _dot(td, pw) for td, pw in zip(tds, pws)]
    es = [no + _dot(td, no) for td, no in zip(tds, nos)]
    e2s = [_dot(e, e) for e in es]
    fs = [e + e2 + _dot(e, e2) for e, e2 in zip(es, e2s)]
    return [f + td + _dot(f, td) for f, td in zip(fs, tds)]


def _seg_sum(x):
    seg_lo = lax.broadcasted_iota(jnp.int32, (x.shape[0], LANES), 1) < LANES // 2
    outs = []
    for j in range(x.shape[1] // LANES):
        blk = x[:, j * LANES:(j + 1) * LANES]
        s_lo = jnp.sum(jnp.where(seg_lo, blk, 0.0), axis=-1, keepdims=True)
        s_hi = jnp.sum(jnp.where(seg_lo, 0.0, blk), axis=-1, keepdims=True)
        outs.append(jnp.where(seg_lo, s_lo, s_hi))
    return jnp.concatenate(outs, axis=1)


def _even_in_kernel(x_ref, halo_ref, g_ref, win_ref, wtail_ref, mu_ref, l1_ref, l2_ref, lb_ref,
                    rkv_ref, qkvb_ref, gate_ref, ba_ref, lw_ref, a_ref, gg_ref, *vg_ref,
                    tiles_per_seq):
    i = pl.program_id(0)
    g = g_ref[...]
    h = _rms(x_ref[...], g)
    prev = _rms(halo_ref[...], g)[SUBLANES - 1:SUBLANES, :]
    prev = jnp.where(i % tiles_per_seq == 0, 0.0, prev)
    dh = _shift_rows(h, prev) - h
    hb = h.astype(BF16)
    rkv_ref[...] = _dot(hb, win_ref[:, 0:3 * A_WIDTH])
    qkvb_ref[...] = _dot(hb, win_ref[:, 3 * A_WIDTH:3 * A_WIDTH + 3 * B_WIDTH])
    gate_ref[...] = _dot(hb, win_ref[:, 3 * A_WIDTH + 3 * B_WIDTH:])
    ba_ref[...] = _dot(hb, wtail_ref[...])

    n_lora = 4 if vg_ref else 3
    mid = [_dot(h + dh * mu_ref[j:j + 1, :], l1_ref[j]) for j in range(n_lora)]
    mid[0] = jnp.tanh(mid[0])
    mid[2] = _sigmoid(mid[2])
    up = [_dot(mid[j], l2_ref[j]) for j in range(n_lora)]
    w_log = -_softplus(-(lb_ref[0:1, :] + up[0])) - 0.5
    lw_ref[...] = -jnp.exp(w_log)
    a_ref[...] = _sigmoid(lb_ref[1:2, :] + up[1])
    gg_ref[...] = up[2]
    if vg_ref:
        vg_ref[0][...] = _sigmoid(lb_ref[2:3, :] + up[3])


def _even_in(x2d, seq, g, win, wtail, mu, l1, l2, lb, has_vres, tm):
    m = x2d.shape[0]
    n_out = 8 if has_vres else 7
    widths = [3 * A_WIDTH, 3 * B_WIDTH, B_WIDTH, LANES, A_WIDTH, A_WIDTH, A_WIDTH, A_WIDTH][:n_out]
    const = lambda a: pl.BlockSpec(a.shape, lambda i: (0,) * a.ndim)
    return pl.pallas_call(
        functools.partial(_even_in_kernel, tiles_per_seq=seq // tm),
        grid=(m // tm,),
        in_specs=[pl.BlockSpec((tm, D_MODEL), lambda i: (i, 0)),
                  pl.BlockSpec((SUBLANES, D_MODEL), lambda i: (jnp.maximum(i * (tm // SUBLANES) - 1, 0), 0)),
                  const(g), const(win), const(wtail), const(mu), const(l1), const(l2), const(lb)],
        out_specs=[pl.BlockSpec((tm, w), lambda i: (i, 0)) for w in widths],
        out_shape=[jax.ShapeDtypeStruct((m, w), F32) for w in widths],
        compiler_params=pltpu.CompilerParams(dimension_semantics=("parallel",), vmem_limit_bytes=VMEM_LIMIT),
        name="even_in",
    )(x2d, x2d, g, win, wtail, mu, l1, l2, lb)


def _rwkv_kernel(rkv_ref, lw_ref, a_ref, gg_ref, *rest, has_vres):
    if has_vres:
        vg_ref, vfirst_ref, par_ref, mu_ref, y_ref, state_ref, tail_ref = rest
    else:
        par_ref, mu_ref, y_ref, vfirst_out_ref, state_ref, tail_ref = rest
    c = pl.program_id(1)

    @pl.when(c == 0)
    def _():
        state_ref[...] = jnp.zeros_like(state_ref)
        tail_ref[...] = jnp.zeros_like(tail_ref)

    rkv = rkv_ref[...]
    rkv_prev = _shift_rows(rkv, tail_ref[SUBLANES - 1:SUBLANES, :])
    tail_ref[...] = rkv[rkv.shape[0] - SUBLANES:, :]
    rkv = rkv + (rkv_prev - rkv) * mu_ref[...]
    r = rkv[:, 0:A_WIDTH]
    k = rkv[:, A_WIDTH:2 * A_WIDTH]
    v = rkv[:, 2 * A_WIDTH:]
    k_k, k_a, r_k, ln_g, ln_b = (par_ref[j:j + 1, :] for j in range(5))
    a = a_ref[...]
    if has_vres:
        v = v + (vfirst_ref[...] - v) * vg_ref[...]
    else:
        vfirst_out_ref[...] = v

    lw_all = lw_ref[...]
    kk_all = k * k_k
    kk_all = kk_all * lax.rsqrt(_seg_sum(kk_all * kk_all) + L2_EPS)
    k2_all = k * (1.0 + (a - 1.0) * k_a)
    b_all = kk_all * a
    bonus = _seg_sum(r * k2_all * r_k) * v
    v_bf_all = v.astype(BF16)

    heads = A_GROUP // A_HEAD
    n_groups = A_WIDTH // A_GROUP
    n_rows = heads * CHUNK
    same_head, incl, strict, same_block = _block_masks(heads)
    incl2 = jnp.concatenate([incl, incl], axis=1)
    n_chunks = rkv.shape[0] // CHUNK
    group_sl = [slice(g * A_GROUP, (g + 1) * A_GROUP) for g in range(n_groups)]

    probs = []
    for cc in range(n_chunks):
        rows = slice(cc * CHUNK, (cc + 1) * CHUNK)
        lw, kk, k2, b = lw_all[rows], kk_all[rows], k2_all[rows], b_all[rows]
        cw = _cumsum_rows(lw)
        cw_last = cw[CHUNK - 1:CHUNK, :]
        e_neg = jnp.exp(-cw)
        e_last = jnp.exp(cw_last - cw)
        r_t = (r[rows] * jnp.exp(cw)).astype(BF16)
        a_t = (-kk * jnp.exp(cw - lw)).astype(BF16)
        b_t = (b * e_neg).astype(BF16)
        k_t = (k2 * e_neg).astype(BF16)
        b_hat = (b * e_last).astype(BF16)
        k_hat = (k2 * e_last).astype(BF16)
        for sl in group_sl:
            lhs = jnp.concatenate([_stack_heads(a_t[:, sl], heads), _stack_heads(r_t[:, sl], heads)], axis=0)
            rhs = jnp.concatenate([b_t[:, sl]] * heads + [k_t[:, sl]] * heads, axis=0)
            probs.append(dict(
                rows=rows, sl=sl, gram=_dot_nt(lhs, rhs),
                ar=jnp.concatenate([a_t[:, sl], r_t[:, sl]], axis=0),
                bk_hat=jnp.concatenate([b_hat[:, sl], k_hat[:, sl]], axis=0),
                v=v_bf_all[rows, sl], w_last=jnp.exp(cw_last[:, sl])))
    minvs = _unit_lower_inverses([jnp.where(strict, pr["gram"][0:n_rows, 0:n_rows], 0.0) for pr in probs],
                                 same_block)

    states = [state_ref[g] for g in range(n_groups)]
    for cc in range(n_chunks):
        prs = probs[cc * n_groups:(cc + 1) * n_groups]
        mis = minvs[cc * n_groups:(cc + 1) * n_groups]
        from_state = [_dot_nt(pr["ar"], st) for pr, st in zip(prs, states)]
        v_s = [_stack_heads(pr["v"], heads) for pr in prs]
        xs = [fs[0:CHUNK] + _unstack_heads(_dot(jnp.where(strict, pr["gram"][0:n_rows, n_rows:], 0.0), vs), heads)
              for fs, pr, vs in zip(from_state, prs, v_s)]
        us = [x + _unstack_heads(_dot(mi, _stack_heads(x.astype(BF16), heads)), heads) for x, mi in zip(xs, mis)]
        u_bf = [u.astype(BF16) for u in us]
        ys = [fs[CHUNK:] + _unstack_heads(_dot(jnp.where(incl2, pr["gram"][n_rows:, :], 0.0),
                                               jnp.concatenate([_stack_heads(ub, heads), vs], axis=0)), heads)
              for fs, pr, ub, vs in zip(from_state, prs, u_bf, v_s)]
        upds = [_dot_tn(jnp.concatenate([ub, pr["v"]], axis=0), pr["bk_hat"]) for ub, pr in zip(u_bf, prs)]
        states = [st * pr["w_last"] + jnp.where(same_head, upd, 0.0) for st, pr, upd in zip(states, prs, upds)]
        for pr, y in zip(prs, ys):
            rows, sl = pr["rows"], pr["sl"]
            mean = _seg_sum(y) * (1.0 / A_HEAD)
            yc = y - mean
            var = _seg_sum(yc * yc) * (1.0 / A_HEAD)
            yn = yc * lax.rsqrt(var + A_GN_EPS) * ln_g[:, sl] + ln_b[:, sl]
            y_ref[rows, sl] = ((yn + bonus[rows, sl]) * gg_ref[rows, sl]).astype(y_ref.dtype)
    for g in range(n_groups):
        state_ref[g] = states[g]


def _rwkv(rkv, lw, a, gg, vg, vfirst, par, mu, batch, seq):
    m = rkv.shape[0]
    nc = seq // A_SCAN_ROWS
    has_vres = vg is not None
    row = lambda w: pl.BlockSpec((A_SCAN_ROWS, w), lambda b, c: (b * nc + c, 0))
    const = lambda arr: pl.BlockSpec(arr.shape, lambda b, c: (0,) * arr.ndim)
    ins = [rkv, lw, a, gg] + ([vg, vfirst] if has_vres else []) + [par, mu]
    in_specs = [row(3 * A_WIDTH)] + [row(A_WIDTH)] * (5 if has_vres else 3) + [const(par), const(mu)]
    n_out = 1 if has_vres else 2
    outs = pl.pallas_call(
        functools.partial(_rwkv_kernel, has_vres=has_vres),
        grid=(batch, nc),
        in_specs=in_specs,
        out_specs=[row(A_WIDTH)] * n_out,
        out_shape=[jax.ShapeDtypeStruct((m, A_WIDTH), dt) for dt in (BF16, F32)[:n_out]],
        scratch_shapes=[pltpu.VMEM((A_WIDTH // A_GROUP, A_GROUP, A_GROUP), F32),
                        pltpu.VMEM((SUBLANES, 3 * A_WIDTH), F32)],
        compiler_params=pltpu.CompilerParams(dimension_semantics=("arbitrary", "arbitrary"),
                                             vmem_limit_bytes=VMEM_LIMIT),
        name="rwkv7_scan",
    )(*ins)
    return outs if not has_vres else (outs[0], vfirst)


def _gdn_kernel(qkv_ref, gate_ref, ba_ref, conv_ref, hp_ref, ng_ref, y_ref, state_ref, zz_ref):
    c = pl.program_id(1)

    @pl.when(c == 0)
    def _():
        state_ref[...] = jnp.zeros_like(state_ref)
        zz_ref[0:SUBLANES, :] = jnp.zeros((SUBLANES, 3 * B_WIDTH), F32)

    z = qkv_ref[...]
    zz_ref[SUBLANES:, :] = z
    zz = zz_ref[...]
    zz_ref[0:SUBLANES, :] = z[z.shape[0] - SUBLANES:, :]
    conv = zz[SUBLANES:, :] * conv_ref[B_CONV - 1:B_CONV, :]
    for j in range(B_CONV - 1):
        conv = conv + pltpu.roll(zz, B_CONV - 1 - j, 0)[SUBLANES:, :] * conv_ref[j:j + 1, :]
    qkv = conv * _sigmoid(conv)

    def per_head(tile, lane0):
        return jnp.concatenate([jnp.broadcast_to(tile[:, lane0 + h:lane0 + h + 1], (CHUNK, B_HEAD))
                                for h in range(B_HEADS)], axis=1)

    def l2n(t):
        return jnp.concatenate(
            [t[:, h * B_HEAD:(h + 1) * B_HEAD]
             * lax.rsqrt(jnp.sum(jnp.square(t[:, h * B_HEAD:(h + 1) * B_HEAD]), axis=-1, keepdims=True) + L2_EPS)
             for h in range(B_HEADS)], axis=1)

    q_all = l2n(qkv[:, 0:B_WIDTH]) * (B_HEAD ** -0.5)
    k_all = l2n(qkv[:, B_WIDTH:2 * B_WIDTH])
    v_all = qkv[:, 2 * B_WIDTH:]
    ba_all = ba_ref[...]
    beta_all = _sigmoid(ba_all)
    g_step_all = -jnp.exp(hp_ref[0:1, :]) * _softplus(ba_all + hp_ref[1:2, :])

    same_head, incl, strict, same_block = _block_masks(B_HEADS)
    n_rows = B_HEADS * CHUNK
    probs = []
    for cc in range(z.shape[0] // CHUNK):
        rows = slice(cc * CHUNK, (cc + 1) * CHUNK)
        q, k, v = q_all[rows], k_all[rows], v_all[rows]
        gc = _cumsum_rows(g_step_all[rows])
        gc_t = gc.T
        beta_f = per_head(beta_all[rows], 0)
        g_col = per_head(gc, B_HEADS)
        g_last = g_col[CHUNK - 1:CHUNK, :]
        e_g = jnp.exp(g_col)
        kb = k * beta_f
        g_col_s = jnp.concatenate([gc[:, B_HEADS + h:B_HEADS + h + 1] for h in range(B_HEADS)], axis=0)
        g_row_s = jnp.concatenate([gc_t[B_HEADS + h:B_HEADS + h + 1, :] for h in range(B_HEADS)], axis=1)
        decay = jnp.where(incl, jnp.exp(jnp.where(incl, g_col_s - g_row_s, 0.0)), 0.0)
        lhs = jnp.concatenate([_stack_heads(kb.astype(BF16), B_HEADS), _stack_heads(q.astype(BF16), B_HEADS)],
                              axis=0)
        gram = _dot_nt(lhs, jnp.concatenate([k.astype(BF16)] * B_HEADS, axis=0))
        probs.append(dict(rows=rows, amat=jnp.where(strict, gram[0:n_rows] * decay, 0.0),
                          qk=gram[n_rows:] * decay, vb=v * beta_f, kbg=kb * e_g, qe=q * e_g,
                          kd=(k * jnp.exp(g_last - g_col)).astype(BF16), s_decay=jnp.exp(g_last)))
    tinvs = _unit_lower_inverses([-pr["amat"] for pr in probs], same_block)

    def in_block(data, hd):
        blocks = [jnp.zeros_like(data)] * B_HEADS
        blocks[hd] = data
        return jnp.concatenate(blocks, axis=0)

    head_sl = [slice(hd * B_HEAD, (hd + 1) * B_HEAD) for hd in range(B_HEADS)]
    head_rows = [slice(hd * CHUNK, (hd + 1) * CHUNK) for hd in range(B_HEADS)]
    solved = [[_dot(tinv[head_rows[hd]],
                    in_block(jnp.concatenate([pr["vb"][:, head_sl[hd]], pr["kbg"][:, head_sl[hd]]],
                                             axis=1).astype(BF16), hd))
               for hd in range(B_HEADS)] for tinv, pr in zip(tinvs, probs)]

    states = [state_ref[hd] for hd in range(B_HEADS)]
    for pr, sol in zip(probs, solved):
        rows, qk = pr["rows"], pr["qk"].astype(BF16)
        kd, g_last_e = pr["kd"], pr["s_decay"]
        us = [pr["vb"][:, sl] + s[:, 0:B_HEAD] for sl, s in zip(head_sl, sol)]
        wqs = [jnp.concatenate([pr["kbg"][:, sl] + s[:, B_HEAD:], pr["qe"][:, sl]], axis=0)
               for sl, s in zip(head_sl, sol)]
        from_state = [_dot(wq, st) for wq, st in zip(wqs, states)]
        v_new = [u - fs[0:CHUNK] for u, fs in zip(us, from_state)]
        states = [st * g_last_e[:, sl] + _dot_tn(kd[:, sl], vn) for st, sl, vn in zip(states, head_sl, v_new)]
        outs = [fs[CHUNK:] + _dot(qk[hr], in_block(vn.astype(BF16), hd))
                for hd, (fs, hr, vn) in enumerate(zip(from_state, head_rows, v_new))]
        for sl, o in zip(head_sl, outs):
            gate = gate_ref[rows, sl]
            y_ref[rows, sl] = (_rms(o, ng_ref[...]) * (gate * _sigmoid(gate))).astype(y_ref.dtype)
    for hd in range(B_HEADS):
        state_ref[hd] = states[hd]


def _gdn(qkvb, gate, ba, conv_w, hp, norm_g, batch, seq):
    m = qkvb.shape[0]
    nc = seq // B_SCAN_ROWS
    row = lambda w: pl.BlockSpec((B_SCAN_ROWS, w), lambda b, c: (b * nc + c, 0))
    const = lambda arr: pl.BlockSpec(arr.shape, lambda b, c: (0,) * arr.ndim)
    return pl.pallas_call(
        _gdn_kernel,
        grid=(batch, nc),
        in_specs=[row(3 * B_WIDTH), row(B_WIDTH), row(LANES), const(conv_w), const(hp), const(norm_g)],
        out_specs=row(B_WIDTH),
        out_shape=jax.ShapeDtypeStruct((m, B_WIDTH), BF16),
        scratch_shapes=[pltpu.VMEM((B_HEADS, B_HEAD, B_HEAD), F32),
                        pltpu.VMEM((B_SCAN_ROWS + SUBLANES, 3 * B_WIDTH), F32)],
        compiler_params=pltpu.CompilerParams(dimension_semantics=("arbitrary", "arbitrary"),
                                             vmem_limit_bytes=VMEM_LIMIT),
        name="gdn_scan",
    )(qkvb, gate, ba, conv_w, hp, norm_g)


def _head_rms_many(xs, ones_bd, gs):
    sqs = [x * x for x in xs]
    his = [sq.astype(BF16) for sq in sqs]
    los = [(sq - hi.astype(F32)).astype(BF16) for sq, hi in zip(sqs, his)]
    sums = [jnp.dot(hi, ones_bd, preferred_element_type=F32) for hi in his]
    sums = [s + jnp.dot(lo, ones_bd, preferred_element_type=F32) for s, lo in zip(sums, los)]
    return [x * lax.rsqrt(s * (1.0 / C_HEAD) + NORM_EPS) * g for x, s, g in zip(xs, sums, gs)]


def _attn_qkv_kernel(x_ref, g_ref, w_ref, qg_ref, kg_ref, ones_ref, o_ref):
    t = pl.program_id(1)

    @pl.when(t == 0)
    def _():
        o_ref[...] = jnp.zeros_like(o_ref)

    @pl.when(t > 0)
    def _():
        hb = _rms(x_ref[...], g_ref[...]).astype(BF16)
        ones_bd = ones_ref[...]
        blk = ones_bd.shape[0]
        o_ref[:, 2 * D_MODEL:] = _dot(hb, w_ref[:, 2 * D_MODEL:]).astype(o_ref.dtype)
        q_gain = qg_ref[...] * (C_HEAD ** -0.5)
        per_pass = D_MODEL // blk // 2
        for half in range(2):
            cols = [slice(c * blk, (c + 1) * blk) for c in range(half * per_pass, (half + 1) * per_pass)]
            cols = cols + [slice(D_MODEL + sl.start, D_MODEL + sl.stop) for sl in cols]
            gains = [q_gain] * (len(cols) // 2) + [kg_ref[...]] * (len(cols) // 2)
            normed = _head_rms_many([_dot(hb, w_ref[:, sl]) for sl in cols], ones_bd, gains)
            for sl, val in zip(cols, normed):
                o_ref[:, sl] = val.astype(o_ref.dtype)


def _attn_qkv(x2d, g, w, qg, kg, ones_bd, batch, seq, tm):
    tiles = seq // tm
    pad_tiles = C_WINDOW // tm
    const = lambda arr: pl.BlockSpec(arr.shape, lambda b, t: (0,) * arr.ndim)
    return pl.pallas_call(
        _attn_qkv_kernel,
        grid=(batch, tiles + pad_tiles),
        in_specs=[pl.BlockSpec((tm, D_MODEL), lambda b, t: (b * tiles + jnp.maximum(t - pad_tiles, 0), 0)),
                  const(g), const(w), const(qg), const(kg), const(ones_bd)],
        out_specs=pl.BlockSpec((tm, 3 * D_MODEL), lambda b, t: (b * (tiles + pad_tiles) + t, 0)),
        out_shape=jax.ShapeDtypeStruct((batch * (seq + C_WINDOW), 3 * D_MODEL), BF16),
        compiler_params=pltpu.CompilerParams(dimension_semantics=("parallel", "arbitrary"),
                                             vmem_limit_bytes=VMEM_LIMIT),
        name="attn_qkv",
    )(x2d, g, w, qg, kg, ones_bd)


def _attn_kernel(q_ref, k_ref, v_ref, bias_ref, o_ref):
    t = pl.program_id(2)
    lane_lo = lax.broadcasted_iota(jnp.int32, (CHUNK, LANES), 1) < C_HEAD
    key_idx = lax.broadcasted_iota(jnp.int32, (CHUNK, C_BAND), 1)
    key_idx = jnp.concatenate([key_idx, key_idx], axis=0)
    zero = jnp.zeros((CHUNK, LANES), q_ref.dtype)
    bias = bias_ref[...]
    scores, windows = [], []
    for cc in range(C_QTILE // CHUNK):
        first_key = pl.multiple_of(t * C_QTILE + cc * CHUNK, CHUNK)
        kw = k_ref[pl.ds(first_key, C_BAND), :]
        windows.append(v_ref[pl.ds(first_key, C_BAND), :])
        q = q_ref[cc * CHUNK:(cc + 1) * CHUNK, :]
        q2 = jnp.concatenate([jnp.where(lane_lo, q, zero), jnp.where(lane_lo, zero, q)], axis=0)
        valid = key_idx >= C_WINDOW - first_key
        scores.append(jnp.where(valid, _dot_nt(q2, kw) + bias, MASK_VALUE))
    probs = [jnp.exp(s - jnp.max(s, axis=-1, keepdims=True)) for s in scores]
    denoms = [jnp.sum(p, axis=-1, keepdims=True) for p in probs]
    outs = [_dot(p, vw) / d for p, vw, d in zip(probs, windows, denoms)]
    for cc, o2 in enumerate(outs):
        o_ref[cc * CHUNK:(cc + 1) * CHUNK, :] = jnp.where(lane_lo, o2[0:CHUNK], o2[CHUNK:]).astype(o_ref.dtype)


def _attn(qkv_pad, bias, batch, seq):
    nt = seq // C_QTILE
    pad_rows = seq + C_WINDOW
    n_pairs = D_MODEL // LANES
    q_off = C_WINDOW // C_QTILE
    return pl.pallas_call(
        _attn_kernel,
        grid=(batch, n_pairs, nt),
        in_specs=[pl.BlockSpec((C_QTILE, LANES), lambda b, h, t: (b * (pad_rows // C_QTILE) + q_off + t, h)),
                  pl.BlockSpec((pad_rows, LANES), lambda b, h, t: (b, n_pairs + h)),
                  pl.BlockSpec((pad_rows, LANES), lambda b, h, t: (b, 2 * n_pairs + h)),
                  pl.BlockSpec((2 * CHUNK, C_BAND), lambda b, h, t: (h, 0))],
        out_specs=pl.BlockSpec((C_QTILE, LANES), lambda b, h, t: (b * nt + t, h)),
        out_shape=jax.ShapeDtypeStruct((batch * seq, D_MODEL), BF16),
        compiler_params=pltpu.CompilerParams(dimension_semantics=("parallel", "parallel", "arbitrary"),
                                             vmem_limit_bytes=VMEM_LIMIT),
        name="band_attn",
    )(qkv_pad, qkv_pad, qkv_pad, bias.reshape(-1, C_BAND))


def _post_kernel(x_ref, p_ref, *rest, n_mix):
    mix_refs = rest[:n_mix]
    wo_ref, g_ref, w1_ref, w2_ref, wp_ref, pg_ref, wg_ref, o_ref = rest[n_mix:]
    mix = jnp.concatenate([mref[...] for mref in mix_refs], axis=1)
    x = x_ref[...] + _dot(mix, wo_ref[...])
    hb = _rms(x, g_ref[...]).astype(BF16)
    acc = None
    for j in range(D_FF // FF_CHUNK):
        sl = slice(j * FF_CHUNK, (j + 1) * FF_CHUNK)
        hid = jnp.maximum(_dot(hb, w1_ref[:, sl]), 0.0)
        part = _dot(hid * hid, w2_ref[sl, :])
        acc = part if acc is None else acc + part
    x = x + acc
    emb = _rms(_dot(p_ref[...], wp_ref[...]), pg_ref[...])
    o_ref[...] = x + emb * _sigmoid(_dot(x, wg_ref[...]))


def _post(x2d, p3d, layer, mixes, wo, g, w1, w2, wp, pg, wg, tm):
    m = x2d.shape[0]
    const = lambda arr: pl.BlockSpec(arr.shape, lambda i: (0,) * arr.ndim, pipeline_mode=pl.Buffered(1))
    row = lambda w: pl.BlockSpec((tm, w), lambda i: (i, 0))
    weights = [wo, g, w1, w2, wp, pg, wg]
    return pl.pallas_call(
        functools.partial(_post_kernel, n_mix=len(mixes)),
        grid=(m // tm,),
        in_specs=[row(D_MODEL), pl.BlockSpec((None, tm, p3d.shape[2]), lambda i: (layer, i, 0))]
                 + [row(mx.shape[1]) for mx in mixes] + [const(w) for w in weights],
        out_specs=row(D_MODEL),
        out_shape=jax.ShapeDtypeStruct((m, D_MODEL), F32),
        compiler_params=pltpu.CompilerParams(dimension_semantics=("parallel",), vmem_limit_bytes=VMEM_LIMIT),
        name="mix_out_mlp_ple",
    )(x2d, p3d, *mixes, *weights)


def _pad_to(w, rows=None, cols=None):
    r = (rows or w.shape[0]) - w.shape[0]
    c = (cols or w.shape[1]) - w.shape[1]
    return jnp.pad(w, ((0, r), (0, c)))


def _rel_bias_table(rel_bias):
    n_heads = rel_bias.shape[0]
    span = CHUNK + C_BAND - 1
    rel = (C_BAND - 1) - jnp.arange(span)
    f = rel_bias[:, jnp.clip(rel, -C_MAX_REL, C_MAX_REL) + C_MAX_REL]
    g = jnp.tile(jnp.pad(f, ((0, 0), (0, 1))), (1, CHUNK))[:, :CHUNK * span].reshape(n_heads, CHUNK, span)
    return g[:, :, CHUNK - 1:CHUNK - 1 + C_BAND]


def kernel(x, p, norm_mix_g, norm_ffn_g, even_w_in, rwkv_mu_proj, rwkv_mu_lora, rwkv_w0, rwkv_w1, rwkv_w2, rwkv_a0, rwkv_a1, rwkv_a2, rwkv_g1, rwkv_g2, rwkv_k_k, rwkv_k_a, rwkv_r_k, rwkv_ln_g, rwkv_ln_b, rwkv_v_mu, rwkv_v0, rwkv_v1, rwkv_v2, gdn_conv_w, gdn_a_log, gdn_dt_bias, gdn_norm_g, even_w_out, attn_w_qkv, attn_q_g, attn_k_g, attn_rel_bias, attn_w_out, mlp_w1, mlp_w2, ple_w_proj, ple_norm_g, ple_w_gate):
    batch, seq, _ = x.shape
    depth = p.shape[0]
    assert seq % 512 == 0 and x.shape[2] == D_MODEL
    tm_in, tm_post, tm_qkv = 256, 512, 512
    xs = x.reshape(batch * seq, D_MODEL)
    row1 = lambda vec: vec.reshape(1, -1)
    main_cols = 3 * A_WIDTH + 4 * B_WIDTH

    blk = 4 * C_HEAD
    ones_bd = (jnp.arange(blk)[:, None] // C_HEAD == jnp.arange(blk)[None, :] // C_HEAD).astype(BF16)

    v_first = None
    for i in range(depth):
        if i % 2 == 0:
            e = i // 2
            has_vres = e > 0
            win = even_w_in[e]
            mus = [rwkv_mu_lora[e, 0], rwkv_mu_lora[e, 1], rwkv_mu_lora[e, 2]]
            l1s = [rwkv_w1[e], rwkv_a1[e], rwkv_g1[e]]
            l2s = [rwkv_w2[e], rwkv_a2[e], rwkv_g2[e]]
            lbs = [rwkv_w0[e], rwkv_a0[e]]
            if has_vres:
                mus.append(rwkv_v_mu[e - 1])
                l1s.append(rwkv_v1[e - 1])
                l2s.append(rwkv_v2[e - 1])
                lbs.append(rwkv_v0[e - 1])
            l1 = jnp.stack([_pad_to(w, cols=A_LORA_PAD) for w in l1s]).astype(BF16)
            l2 = jnp.stack([_pad_to(w, rows=A_LORA_PAD) for w in l2s]).astype(BF16)
            outs = _even_in(xs, seq, row1(norm_mix_g[i]), win[:, :main_cols].astype(BF16),
                            _pad_to(win[:, main_cols:], cols=LANES).astype(BF16),
                            jnp.stack(mus), l1, l2, jnp.stack(lbs), has_vres, tm_in)
            rkv, qkvb, gate, ba, lw, a_lr, gg = outs[:7]
            par = jnp.stack([rwkv_k_k[e], rwkv_k_a[e], rwkv_r_k[e].reshape(-1), rwkv_ln_g[e], rwkv_ln_b[e]])
            y_a, v_first = _rwkv(rkv, lw, a_lr, gg, outs[7] if has_vres else None, v_first, par,
                                 rwkv_mu_proj[e].reshape(1, -1), batch, seq)
            hp = jnp.stack([_pad_to(jnp.pad(row1(gdn_a_log[e]), ((0, 0), (B_HEADS, 0))), cols=LANES)[0],
                            _pad_to(jnp.pad(row1(gdn_dt_bias[e]), ((0, 0), (B_HEADS, 0))), cols=LANES)[0]])
            y_b = _gdn(qkvb, gate, ba, gdn_conv_w[e], hp, row1(gdn_norm_g[e]), batch, seq)
            mixes, wo = [y_a, y_b], even_w_out[e]
        else:
            o = i // 2
            tile4 = lambda gvec: jnp.tile(gvec, blk // C_HEAD).reshape(1, blk)
            qkv_pad = _attn_qkv(xs, row1(norm_mix_g[i]), attn_w_qkv[o].astype(BF16), tile4(attn_q_g[o]),
                                tile4(attn_k_g[o]), ones_bd, batch, seq, tm_qkv)
            mixes, wo = [_attn(qkv_pad, _rel_bias_table(attn_rel_bias[o]), batch, seq)], attn_w_out[o]
        xs = _post(xs, p.reshape(depth, batch * seq, -1), i, mixes, wo.astype(BF16), row1(norm_ffn_g[i]),
                   mlp_w1[i].astype(BF16), mlp_w2[i].astype(BF16), ple_w_proj[i].astype(BF16),
                   row1(ple_norm_g[i]), ple_w_gate[i].astype(BF16), tm_post)
    return xs.reshape(batch, seq, D_MODEL)
```

```python
import functools

import jax
import jax.numpy as jnp
from jax import lax
from jax.experimental import pallas as pl
from jax.experimental.pallas import tpu as pltpu

F32 = jnp.float32
BF16 = jnp.bfloat16

D_MODEL = 1024
CHUNK = 64
NORM_EPS = 1e-6
L2_EPS = 1e-6
A_WIDTH = 512
A_HEAD = 64
A_GN_EPS = 64e-5
A_LORA_PAD = 128
A_GROUP = 256
A_SCAN_ROWS = 4 * CHUNK
B_SCAN_ROWS = 4 * CHUNK
B_WIDTH = 512
B_HEADS = 4
B_HEAD = 128
B_CONV = 4
C_HEADS = 16
C_HEAD = 64
C_WINDOW = 8 * CHUNK
C_BAND = C_WINDOW + CHUNK
C_QTILE = 8 * CHUNK
C_MAX_REL = 256
D_FF = 4096
FF_CHUNK = 1024
LANES = 128
SUBLANES = 8
INV_BLOCK = 16
MASK_VALUE = -1e30
VMEM_LIMIT = 56 * 1024 * 1024


def _dot(a, b):
    return jnp.dot(a.astype(BF16), b.astype(BF16), preferred_element_type=F32)


def _dot_nt(a, b):
    return lax.dot_general(a.astype(BF16), b.astype(BF16), (((1,), (1,)), ((), ())),
                           preferred_element_type=F32)


def _dot_tn(a, b):
    return jnp.dot(a.astype(BF16).T, b.astype(BF16), preferred_element_type=F32)


def _rms(x, g, eps=NORM_EPS):
    return x * lax.rsqrt(jnp.mean(x * x, axis=-1, keepdims=True) + eps) * g


def _sigmoid(z):
    return 1.0 / (1.0 + jnp.exp(-z))


def _softplus(z):
    return jnp.maximum(z, 0.0) + jnp.log(1.0 + jnp.exp(-jnp.abs(z)))


def _shift_rows(x, prev_row):
    row = lax.broadcasted_iota(jnp.int32, x.shape, 0)
    return jnp.where(row == 0, prev_row, pltpu.roll(x, 1, 0))


def _cumsum_rows(x):
    row = lax.broadcasted_iota(jnp.int32, x.shape, 0)
    step = 1
    while step < x.shape[0]:
        x = x + jnp.where(row >= step, pltpu.roll(x, step, 0), 0.0)
        step *= 2
    return x


def _stack_heads(x, n_heads):
    head_w = x.shape[1] // n_heads
    lane_head = lax.broadcasted_iota(jnp.int32, x.shape, 1) >> (head_w.bit_length() - 1)
    zero = jnp.zeros_like(x)
    return jnp.concatenate([jnp.where(lane_head == h, x, zero) for h in range(n_heads)], axis=0)


def _unstack_heads(xs, n_heads):
    out = xs[0:CHUNK]
    for h in range(1, n_heads):
        out = out + xs[h * CHUNK:(h + 1) * CHUNK]
    return out


def _block_masks(n_heads):
    n = n_heads * CHUNK
    r = lax.broadcasted_iota(jnp.int32, (n, n), 0)
    c = lax.broadcasted_iota(jnp.int32, (n, n), 1)
    same_head = (r ^ c) < CHUNK
    delta = jnp.where(same_head, r - c, -1)
    return same_head, delta >= 0, delta > 0, (r ^ c) < INV_BLOCK


def _unit_lower_inverses(n_mats, same_block):
    nds = [jnp.where(same_block, n, 0.0) for n in n_mats]
    nos = [n - nd for n, nd in zip(n_mats, nds)]
    tds, pws = nds, nds
    for _ in range(3):
        pws = [_dot(pw, pw) for pw in pws]
        tds = [td + pw + _dot(td, pw) for td, pw in zip(tds, pws)]
    es = [no + _dot(td, no) for td, no in zip(tds, nos)]
    e2s = [_dot(e, e) for e in es]
    fs = [e + e2 + _dot(e, e2) for e, e2 in zip(es, e2s)]
    return [f + td + _dot(f, td) for f, td in zip(fs, tds)]


def _seg_sum(x):
    seg_lo = lax.broadcasted_iota(jnp.int32, (x.shape[0], LANES), 1) < LANES // 2
    outs = []
    for j in range(x.shape[1] // LANES):
        blk = x[:, j * LANES:(j + 1) * LANES]
        s_lo = jnp.sum(jnp.where(seg_lo, blk, 0.0), axis=-1, keepdims=True)
        s_hi = jnp.sum(jnp.where(seg_lo, 0.0, blk), axis=-1, keepdims=True)
        outs.append(jnp.where(seg_lo, s_lo, s_hi))
    return jnp.concatenate(outs, axis=1)


def _even_in_kernel(x_ref, halo_ref, g_ref, win_ref, wtail_ref, mu_ref, l1_ref, l2_ref, lb_ref,
                    rkv_ref, qkvb_ref, gate_ref, ba_ref, lw_ref, a_ref, gg_ref, *vg_ref,
                    tiles_per_seq):
    i = pl.program_id(0)
    g = g_ref[...]
    h = _rms(x_ref[...], g)
    prev = _rms(halo_ref[...], g)[SUBLANES - 1:SUBLANES, :]
    prev = jnp.where(i % tiles_per_seq == 0, 0.0, prev)
    dh = _shift_rows(h, prev) - h
    hb = h.astype(BF16)
    rkv_ref[...] = _dot(hb, win_ref[:, 0:3 * A_WIDTH])
    qkvb_ref[...] = _dot(hb, win_ref[:, 3 * A_WIDTH:3 * A_WIDTH + 3 * B_WIDTH])
    gate_ref[...] = _dot(hb, win_ref[:, 3 * A_WIDTH + 3 * B_WIDTH:])
    ba_ref[...] = _dot(hb, wtail_ref[...])

    n_lora = 4 if vg_ref else 3
    mid = [_dot(h + dh * mu_ref[j:j + 1, :], l1_ref[j]) for j in range(n_lora)]
    mid[0] = jnp.tanh(mid[0])
    mid[2] = _sigmoid(mid[2])
    up = [_dot(mid[j], l2_ref[j]) for j in range(n_lora)]
    w_log = -_softplus(-(lb_ref[0:1, :] + up[0])) - 0.5
    lw_ref[...] = -jnp.exp(w_log)
    a_ref[...] = _sigmoid(lb_ref[1:2, :] + up[1])
    gg_ref[...] = up[2]
    if vg_ref:
        vg_ref[0][...] = _sigmoid(lb_ref[2:3, :] + up[3])


def _even_in(x2d, seq, g, win, wtail, mu, l1, l2, lb, has_vres, tm):
    m = x2d.shape[0]
    n_out = 8 if has_vres else 7
    widths = [3 * A_WIDTH, 3 * B_WIDTH, B_WIDTH, LANES, A_WIDTH, A_WIDTH, A_WIDTH, A_WIDTH][:n_out]
    const = lambda a: pl.BlockSpec(a.shape, lambda i: (0,) * a.ndim, pipeline_mode=pl.Buffered(1))
    return pl.pallas_call(
        functools.partial(_even_in_kernel, tiles_per_seq=seq // tm),
        grid=(m // tm,),
        in_specs=[pl.BlockSpec((tm, D_MODEL), lambda i: (i, 0)),
                  pl.BlockSpec((SUBLANES, D_MODEL), lambda i: (jnp.maximum(i * (tm // SUBLANES) - 1, 0), 0)),
                  const(g), const(win), const(wtail), const(mu), const(l1), const(l2), const(lb)],
        out_specs=[pl.BlockSpec((tm, w), lambda i: (i, 0)) for w in widths],
        out_shape=[jax.ShapeDtypeStruct((m, w), F32) for w in widths],
        compiler_params=pltpu.CompilerParams(dimension_semantics=("parallel",), vmem_limit_bytes=VMEM_LIMIT),
        name="even_in",
    )(x2d, x2d, g, win, wtail, mu, l1, l2, lb)


def _rwkv_kernel(rkv_ref, lw_ref, a_ref, gg_ref, *rest, has_vres):
    if has_vres:
        vg_ref, vfirst_ref, par_ref, mu_ref, y_ref, state_ref, tail_ref = rest
    else:
        par_ref, mu_ref, y_ref, vfirst_out_ref, state_ref, tail_ref = rest
    c = pl.program_id(1)

    @pl.when(c == 0)
    def _():
        state_ref[...] = jnp.zeros_like(state_ref)
        tail_ref[...] = jnp.zeros_like(tail_ref)

    rkv = rkv_ref[...]
    rkv_prev = _shift_rows(rkv, tail_ref[SUBLANES - 1:SUBLANES, :])
    tail_ref[...] = rkv[rkv.shape[0] - SUBLANES:, :]
    rkv = rkv + (rkv_prev - rkv) * mu_ref[...]
    r = rkv[:, 0:A_WIDTH]
    k = rkv[:, A_WIDTH:2 * A_WIDTH]
    v = rkv[:, 2 * A_WIDTH:]
    k_k, k_a, r_k, ln_g, ln_b = (par_ref[j:j + 1, :] for j in range(5))
    a = a_ref[...]
    if has_vres:
        v = v + (vfirst_ref[...] - v) * vg_ref[...]
    else:
        vfirst_out_ref[...] = v

    lw_all = lw_ref[...]
    kk_all = k * k_k
    kk_all = kk_all * lax.rsqrt(_seg_sum(kk_all * kk_all) + L2_EPS)
    k2_all = k * (1.0 + (a - 1.0) * k_a)
    b_all = kk_all * a
    bonus = _seg_sum(r * k2_all * r_k) * v
    v_bf_all = v.astype(BF16)

    heads = A_GROUP // A_HEAD
    n_groups = A_WIDTH // A_GROUP
    n_rows = heads * CHUNK
    same_head, incl, strict, same_block = _block_masks(heads)
    incl2 = jnp.concatenate([incl, incl], axis=1)
    n_chunks = rkv.shape[0] // CHUNK
    group_sl = [slice(g * A_GROUP, (g + 1) * A_GROUP) for g in range(n_groups)]

    probs = []
    for cc in range(n_chunks):
        rows = slice(cc * CHUNK, (cc + 1) * CHUNK)
        lw, kk, k2, b = lw_all[rows], kk_all[rows], k2_all[rows], b_all[rows]
        cw = _cumsum_rows(lw)
        cw_last = cw[CHUNK - 1:CHUNK, :]
        e_neg = jnp.exp(-cw)
        e_last = jnp.exp(cw_last - cw)
        r_t = (r[rows] * jnp.exp(cw)).astype(BF16)
        a_t = (-kk * jnp.exp(cw - lw)).astype(BF16)
        b_t = (b * e_neg).astype(BF16)
        k_t = (k2 * e_neg).astype(BF16)
        b_hat = (b * e_last).astype(BF16)
        k_hat = (k2 * e_last).astype(BF16)
        for sl in group_sl:
            lhs = jnp.concatenate([_stack_heads(a_t[:, sl], heads), _stack_heads(r_t[:, sl], heads)], axis=0)
            rhs = jnp.concatenate([b_t[:, sl]] * heads + [k_t[:, sl]] * heads, axis=0)
            probs.append(dict(
                rows=rows, sl=sl, gram=_dot_nt(lhs, rhs),
                ar=jnp.concatenate([a_t[:, sl], r_t[:, sl]], axis=0),
                bk_hat=jnp.concatenate([b_hat[:, sl], k_hat[:, sl]], axis=0),
                v=v_bf_all[rows, sl], w_last=jnp.exp(cw_last[:, sl])))
    minvs = _unit_lower_inverses([jnp.where(strict, pr["gram"][0:n_rows, 0:n_rows], 0.0) for pr in probs],
                                 same_block)

    states = [state_ref[g] for g in range(n_groups)]
    for cc in range(n_chunks):
        prs = probs[cc * n_groups:(cc + 1) * n_groups]
        mis = minvs[cc * n_groups:(cc + 1) * n_groups]
        from_state = [_dot_nt(pr["ar"], st) for pr, st in zip(prs, states)]
        v_s = [_stack_heads(pr["v"], heads) for pr in prs]
        xs = [fs[0:CHUNK] + _unstack_heads(_dot(jnp.where(strict, pr["gram"][0:n_rows, n_rows:], 0.0), vs), heads)
              for fs, pr, vs in zip(from_state, prs, v_s)]
        us = [x + _unstack_heads(_dot(mi, _stack_heads(x.astype(BF16), heads)), heads) for x, mi in zip(xs, mis)]
        u_bf = [u.astype(BF16) for u in us]
        ys = [fs[CHUNK:] + _unstack_heads(_dot(jnp.where(incl2, pr["gram"][n_rows:, :], 0.0),
                                               jnp.concatenate([_stack_heads(ub, heads), vs], axis=0)), heads)
              for fs, pr, ub, vs in zip(from_state, prs, u_bf, v_s)]
        upds = [_dot_tn(jnp.concatenate([ub, pr["v"]], axis=0), pr["bk_hat"]) for ub, pr in zip(u_bf, prs)]
        states = [st * pr["w_last"] + jnp.where(same_head, upd, 0.0) for st, pr, upd in zip(states, prs, upds)]
        for pr, y in zip(prs, ys):
            rows, sl = pr["rows"], pr["sl"]
            mean = _seg_sum(y) * (1.0 / A_HEAD)
            yc = y - mean
            var = _seg_sum(yc * yc) * (1.0 / A_HEAD)
            yn = yc * lax.rsqrt(var + A_GN_EPS) * ln_g[:, sl] + ln_b[:, sl]
            y_ref[rows, sl] = ((yn + bonus[rows, sl]) * gg_ref[rows, sl]).astype(y_ref.dtype)
    for g in range(n_groups):
        state_ref[g] = states[g]


def _rwkv(rkv, lw, a, gg, vg, vfirst, par, mu, batch, seq):
    m = rkv.shape[0]
    nc = seq // A_SCAN_ROWS
    has_vres = vg is not None
    row = lambda w: pl.BlockSpec((A_SCAN_ROWS, w), lambda b, c: (b * nc + c, 0))
    const = lambda arr: pl.BlockSpec(arr.shape, lambda b, c: (0,) * arr.ndim)
    ins = [rkv, lw, a, gg] + ([vg, vfirst] if has_vres else []) + [par, mu]
    in_specs = [row(3 * A_WIDTH)] + [row(A_WIDTH)] * (5 if has_vres else 3) + [const(par), const(mu)]
    n_out = 1 if has_vres else 2
    outs = pl.pallas_call(
        functools.partial(_rwkv_kernel, has_vres=has_vres),
        grid=(batch, nc),
        in_specs=in_specs,
        out_specs=[row(A_WIDTH)] * n_out,
        out_shape=[jax.ShapeDtypeStruct((m, A_WIDTH), dt) for dt in (BF16, F32)[:n_out]],
        scratch_shapes=[pltpu.VMEM((A_WIDTH // A_GROUP, A_GROUP, A_GROUP), F32),
                        pltpu.VMEM((SUBLANES, 3 * A_WIDTH), F32)],
        compiler_params=pltpu.CompilerParams(dimension_semantics=("arbitrary", "arbitrary"),
                                             vmem_limit_bytes=VMEM_LIMIT),
        name="rwkv7_scan",
    )(*ins)
    return outs if not has_vres else (outs[0], vfirst)


def _gdn_kernel(qkv_ref, gate_ref, ba_ref, conv_ref, hp_ref, ng_ref, y_ref, state_ref, zz_ref):
    c = pl.program_id(1)

    @pl.when(c == 0)
    def _():
        state_ref[...] = jnp.zeros_like(state_ref)
        zz_ref[0:SUBLANES, :] = jnp.zeros((SUBLANES, 3 * B_WIDTH), F32)

    z = qkv_ref[...]
    zz_ref[SUBLANES:, :] = z
    zz = zz_ref[...]
    zz_ref[0:SUBLANES, :] = z[z.shape[0] - SUBLANES:, :]
    conv = zz[SUBLANES:, :] * conv_ref[B_CONV - 1:B_CONV, :]
    for j in range(B_CONV - 1):
        conv = conv + pltpu.roll(zz, B_CONV - 1 - j, 0)[SUBLANES:, :] * conv_ref[j:j + 1, :]
    qkv = conv * _sigmoid(conv)

    def per_head(tile, lane0):
        return jnp.concatenate([jnp.broadcast_to(tile[:, lane0 + h:lane0 + h + 1], (CHUNK, B_HEAD))
                                for h in range(B_HEADS)], axis=1)

    def l2n(t):
        return jnp.concatenate(
            [t[:, h * B_HEAD:(h + 1) * B_HEAD]
             * lax.rsqrt(jnp.sum(jnp.square(t[:, h * B_HEAD:(h + 1) * B_HEAD]), axis=-1, keepdims=True) + L2_EPS)
             for h in range(B_HEADS)], axis=1)

    q_all = l2n(qkv[:, 0:B_WIDTH]) * (B_HEAD ** -0.5)
    k_all = l2n(qkv[:, B_WIDTH:2 * B_WIDTH])
    v_all = qkv[:, 2 * B_WIDTH:]
    ba_all = ba_ref[...]
    beta_all = _sigmoid(ba_all)
    g_step_all = -jnp.exp(hp_ref[0:1, :]) * _softplus(ba_all + hp_ref[1:2, :])

    same_head, incl, strict, same_block = _block_masks(B_HEADS)
    n_rows = B_HEADS * CHUNK
    probs = []
    for cc in range(z.shape[0] // CHUNK):
        rows = slice(cc * CHUNK, (cc + 1) * CHUNK)
        q, k, v = q_all[rows], k_all[rows], v_all[rows]
        gc = _cumsum_rows(g_step_all[rows])
        gc_t = gc.T
        beta_f = per_head(beta_all[rows], 0)
        g_col = per_head(gc, B_HEADS)
        g_last = g_col[CHUNK - 1:CHUNK, :]
        e_g = jnp.exp(g_col)
        kb = k * beta_f
        g_col_s = jnp.concatenate([gc[:, B_HEADS + h:B_HEADS + h + 1] for h in range(B_HEADS)], axis=0)
        g_row_s = jnp.concatenate([gc_t[B_HEADS + h:B_HEADS + h + 1, :] for h in range(B_HEADS)], axis=1)
        decay = jnp.where(incl, jnp.exp(jnp.where(incl, g_col_s - g_row_s, 0.0)), 0.0)
        lhs = jnp.concatenate([_stack_heads(kb.astype(BF16), B_HEADS), _stack_heads(q.astype(BF16), B_HEADS)],
                              axis=0)
        gram = _dot_nt(lhs, jnp.concatenate([k.astype(BF16)] * B_HEADS, axis=0))
        probs.append(dict(rows=rows, amat=jnp.where(strict, gram[0:n_rows] * decay, 0.0),
                          qk=gram[n_rows:] * decay, vb=v * beta_f, kbg=kb * e_g, qe=q * e_g,
                          kd=(k * jnp.exp(g_last - g_col)).astype(BF16), s_decay=jnp.exp(g_last)))
    tinvs = _unit_lower_inverses([-pr["amat"] for pr in probs], same_block)

    def in_block(data, hd):
        blocks = [jnp.zeros_like(data)] * B_HEADS
        blocks[hd] = data
        return jnp.concatenate(blocks, axis=0)

    head_sl = [slice(hd * B_HEAD, (hd + 1) * B_HEAD) for hd in range(B_HEADS)]
    head_rows = [slice(hd * CHUNK, (hd + 1) * CHUNK) for hd in range(B_HEADS)]
    solved = [[_dot(tinv[head_rows[hd]],
                    in_block(jnp.concatenate([pr["vb"][:, head_sl[hd]], pr["kbg"][:, head_sl[hd]]],
                                             axis=1).astype(BF16), hd))
               for hd in range(B_HEADS)] for tinv, pr in zip(tinvs, probs)]

    states = [state_ref[hd] for hd in range(B_HEADS)]
    for pr, sol in zip(probs, solved):
        rows, qk = pr["rows"], pr["qk"].astype(BF16)
        kd, g_last_e = pr["kd"], pr["s_decay"]
        us = [pr["vb"][:, sl] + s[:, 0:B_HEAD] for sl, s in zip(head_sl, sol)]
        wqs = [jnp.concatenate([pr["kbg"][:, sl] + s[:, B_HEAD:], pr["qe"][:, sl]], axis=0)
               for sl, s in zip(head_sl, sol)]
        from_state = [_dot(wq, st) for wq, st in zip(wqs, states)]
        v_new = [u - fs[0:CHUNK] for u, fs in zip(us, from_state)]
        states = [st * g_last_e[:, sl] + _dot_tn(kd[:, sl], vn) for st, sl, vn in zip(states, head_sl, v_new)]
        outs = [fs[CHUNK:] + _dot(qk[hr], in_block(vn.astype(BF16), hd))
                for hd, (fs, hr, vn) in enumerate(zip(from_state, head_rows, v_new))]
        for sl, o in zip(head_sl, outs):
            gate = gate_ref[rows, sl]
            y_ref[rows, sl] = (_rms(o, ng_ref[...]) * (gate * _sigmoid(gate))).astype(y_ref.dtype)
    for hd in range(B_HEADS):
        state_ref[hd] = states[hd]


def _gdn(qkvb, gate, ba, conv_w, hp, norm_g, batch, seq):
    m = qkvb.shape[0]
    nc = seq // B_SCAN_ROWS
    row = lambda w: pl.BlockSpec((B_SCAN_ROWS, w), lambda b, c: (b * nc + c, 0))
    const = lambda arr: pl.BlockSpec(arr.shape, lambda b, c: (0,) * arr.ndim)
    return pl.pallas_call(
        _gdn_kernel,
        grid=(batch, nc),
        in_specs=[row(3 * B_WIDTH), row(B_WIDTH), row(LANES), const(conv_w), const(hp), const(norm_g)],
        out_specs=row(B_WIDTH),
        out_shape=jax.ShapeDtypeStruct((m, B_WIDTH), BF16),
        scratch_shapes=[pltpu.VMEM((B_HEADS, B_HEAD, B_HEAD), F32),
                        pltpu.VMEM((B_SCAN_ROWS + SUBLANES, 3 * B_WIDTH), F32)],
        compiler_params=pltpu.CompilerParams(dimension_semantics=("arbitrary", "arbitrary"),
                                             vmem_limit_bytes=VMEM_LIMIT),
        name="gdn_scan",
    )(qkvb, gate, ba, conv_w, hp, norm_g)


def _head_rms_many(xs, ones_bd, gs):
    sqs = [x * x for x in xs]
    his = [sq.astype(BF16) for sq in sqs]
    los = [(sq - hi.astype(F32)).astype(BF16) for sq, hi in zip(sqs, his)]
    sums = [jnp.dot(hi, ones_bd, preferred_element_type=F32) for hi in his]
    sums = [s + jnp.dot(lo, ones_bd, preferred_element_type=F32) for s, lo in zip(sums, los)]
    return [x * lax.rsqrt(s * (1.0 / C_HEAD) + NORM_EPS) * g for x, s, g in zip(xs, sums, gs)]


def _attn_qkv_kernel(x_ref, g_ref, w_ref, qg_ref, kg_ref, ones_ref, o_ref):
    t = pl.program_id(1)

    @pl.when(t == 0)
    def _():
        o_ref[...] = jnp.zeros_like(o_ref)

    @pl.when(t > 0)
    def _():
        hb = _rms(x_ref[...], g_ref[...]).astype(BF16)
        ones_bd = ones_ref[...]
        blk = ones_bd.shape[0]
        o_ref[:, 2 * D_MODEL:] = _dot(hb, w_ref[:, 2 * D_MODEL:]).astype(o_ref.dtype)
        q_gain = qg_ref[...] * (C_HEAD ** -0.5)
        per_pass = D_MODEL // blk // 2
        for half in range(2):
            cols = [slice(c * blk, (c + 1) * blk) for c in range(half * per_pass, (half + 1) * per_pass)]
            cols = cols + [slice(D_MODEL + sl.start, D_MODEL + sl.stop) for sl in cols]
            gains = [q_gain] * (len(cols) // 2) + [kg_ref[...]] * (len(cols) // 2)
            normed = _head_rms_many([_dot(hb, w_ref[:, sl]) for sl in cols], ones_bd, gains)
            for sl, val in zip(cols, normed):
                o_ref[:, sl] = val.astype(o_ref.dtype)


def _attn_qkv(x2d, g, w, qg, kg, ones_bd, batch, seq, tm):
    tiles = seq // tm
    pad_tiles = C_WINDOW // tm
    const = lambda arr: pl.BlockSpec(arr.shape, lambda b, t: (0,) * arr.ndim)
    return pl.pallas_call(
        _attn_qkv_kernel,
        grid=(batch, tiles + pad_tiles),
        in_specs=[pl.BlockSpec((tm, D_MODEL), lambda b, t: (b * tiles + jnp.maximum(t - pad_tiles, 0), 0)),
                  const(g), const(w), const(qg), const(kg), const(ones_bd)],
        out_specs=pl.BlockSpec((tm, 3 * D_MODEL), lambda b, t: (b * (tiles + pad_tiles) + t, 0)),
        out_shape=jax.ShapeDtypeStruct((batch * (seq + C_WINDOW), 3 * D_MODEL), BF16),
        compiler_params=pltpu.CompilerParams(dimension_semantics=("parallel", "arbitrary"),
                                             vmem_limit_bytes=VMEM_LIMIT),
        name="attn_qkv",
    )(x2d, g, w, qg, kg, ones_bd)


def _attn_kernel(q_ref, k_ref, v_ref, bias_ref, o_ref):
    t = pl.program_id(2)
    lane_lo = lax.broadcasted_iota(jnp.int32, (CHUNK, LANES), 1) < C_HEAD
    zero = jnp.zeros((CHUNK, LANES), q_ref.dtype)

    def tile(mask_start):
        bias = bias_ref[...]
        key_idx = lax.broadcasted_iota(jnp.int32, (2 * CHUNK, C_BAND), 1)
        scores, windows = [], []
        for cc in range(C_QTILE // CHUNK):
            first_key = pl.multiple_of(t * C_QTILE + cc * CHUNK, CHUNK)
            kw = k_ref[pl.ds(first_key, C_BAND), :]
            windows.append(v_ref[pl.ds(first_key, C_BAND), :])
            q = q_ref[cc * CHUNK:(cc + 1) * CHUNK, :]
            q2 = jnp.concatenate([jnp.where(lane_lo, q, zero), jnp.where(lane_lo, zero, q)], axis=0)
            s = _dot_nt(q2, kw) + bias
            if mask_start:
                s = jnp.where(key_idx >= C_WINDOW - first_key, s, MASK_VALUE)
            scores.append(s)
        probs = [jnp.exp(s - jnp.max(s, axis=-1, keepdims=True)) for s in scores]
        denoms = [jnp.sum(p, axis=-1, keepdims=True) for p in probs]
        outs = [_dot(p, vw) / d for p, vw, d in zip(probs, windows, denoms)]
        for cc, o2 in enumerate(outs):
            o_ref[cc * CHUNK:(cc + 1) * CHUNK, :] = jnp.where(lane_lo, o2[0:CHUNK], o2[CHUNK:]).astype(o_ref.dtype)

    first_tiles = -(-C_WINDOW // C_QTILE)
    pl.when(t < first_tiles)(functools.partial(tile, True))
    pl.when(t >= first_tiles)(functools.partial(tile, False))


def _attn(qkv_pad, bias, batch, seq):
    nt = seq // C_QTILE
    pad_rows = seq + C_WINDOW
    n_pairs = D_MODEL // LANES
    q_off = C_WINDOW // C_QTILE
    return pl.pallas_call(
        _attn_kernel,
        grid=(batch, n_pairs, nt),
        in_specs=[pl.BlockSpec((C_QTILE, LANES), lambda b, h, t: (b * (pad_rows // C_QTILE) + q_off + t, h)),
                  pl.BlockSpec((pad_rows, LANES), lambda b, h, t: (b, n_pairs + h)),
                  pl.BlockSpec((pad_rows, LANES), lambda b, h, t: (b, 2 * n_pairs + h)),
                  pl.BlockSpec((2 * CHUNK, C_BAND), lambda b, h, t: (h, 0))],
        out_specs=pl.BlockSpec((C_QTILE, LANES), lambda b, h, t: (b * nt + t, h)),
        out_shape=jax.ShapeDtypeStruct((batch * seq, D_MODEL), BF16),
        compiler_params=pltpu.CompilerParams(dimension_semantics=("parallel", "parallel", "arbitrary"),
                                             vmem_limit_bytes=VMEM_LIMIT),
        name="band_attn",
    )(qkv_pad, qkv_pad, qkv_pad, bias.reshape(-1, C_BAND))


def _post_kernel(x_ref, p_ref, *rest, n_mix):
    mix_refs = rest[:n_mix]
    wo_ref, g_ref, w1_ref, w2_ref, wp_ref, pg_ref, wg_ref, o_ref = rest[n_mix:]
    mix = jnp.concatenate([mref[...] for mref in mix_refs], axis=1)
    x = x_ref[...] + _dot(mix, wo_ref[...])
    hb = _rms(x, g_ref[...]).astype(BF16)
    acc = None
    for j in range(D_FF // FF_CHUNK):
        sl = slice(j * FF_CHUNK, (j + 1) * FF_CHUNK)
        hid = jnp.maximum(_dot(hb, w1_ref[:, sl]), 0.0)
        part = _dot(hid * hid, w2_ref[sl, :])
        acc = part if acc is None else acc + part
    x = x + acc
    emb = _rms(_dot(p_ref[...], wp_ref[...]), pg_ref[...])
    o_ref[...] = x + emb * _sigmoid(_dot(x, wg_ref[...]))


def _post(x2d, p3d, layer, mixes, wo, g, w1, w2, wp, pg, wg, tm):
    m = x2d.shape[0]
    const = lambda arr: pl.BlockSpec(arr.shape, lambda i: (0,) * arr.ndim, pipeline_mode=pl.Buffered(1))
    row = lambda w: pl.BlockSpec((tm, w), lambda i: (i, 0))
    weights = [wo, g, w1, w2, wp, pg, wg]
    return pl.pallas_call(
        functools.partial(_post_kernel, n_mix=len(mixes)),
        grid=(m // tm,),
        in_specs=[row(D_MODEL), pl.BlockSpec((None, tm, p3d.shape[2]), lambda i: (layer, i, 0))]
                 + [row(mx.shape[1]) for mx in mixes] + [const(w) for w in weights],
        out_specs=row(D_MODEL),
        out_shape=jax.ShapeDtypeStruct((m, D_MODEL), F32),
        compiler_params=pltpu.CompilerParams(dimension_semantics=("parallel",), vmem_limit_bytes=VMEM_LIMIT),
        name="mix_out_mlp_ple",
    )(x2d, p3d, *mixes, *weights)


def _pad_to(w, rows=None, cols=None):
    r = (rows or w.shape[0]) - w.shape[0]
    c = (cols or w.shape[1]) - w.shape[1]
    return jnp.pad(w, ((0, r), (0, c)))


def _rel_bias_table(rel_bias):
    n_heads = rel_bias.shape[0]
    span = CHUNK + C_BAND - 1
    rel = (C_BAND - 1) - jnp.arange(span)
    f = rel_bias[:, jnp.clip(rel, -C_MAX_REL, C_MAX_REL) + C_MAX_REL]
    g = jnp.tile(jnp.pad(f, ((0, 0), (0, 1))), (1, CHUNK))[:, :CHUNK * span].reshape(n_heads, CHUNK, span)
    return g[:, :, CHUNK - 1:CHUNK - 1 + C_BAND]


def kernel(x, p, norm_mix_g, norm_ffn_g, even_w_in, rwkv_mu_proj, rwkv_mu_lora, rwkv_w0, rwkv_w1, rwkv_w2, rwkv_a0, rwkv_a1, rwkv_a2, rwkv_g1, rwkv_g2, rwkv_k_k, rwkv_k_a, rwkv_r_k, rwkv_ln_g, rwkv_ln_b, rwkv_v_mu, rwkv_v0, rwkv_v1, rwkv_v2, gdn_conv_w, gdn_a_log, gdn_dt_bias, gdn_norm_g, even_w_out, attn_w_qkv, attn_q_g, attn_k_g, attn_rel_bias, attn_w_out, mlp_w1, mlp_w2, ple_w_proj, ple_norm_g, ple_w_gate):
    batch, seq, _ = x.shape
    depth = p.shape[0]
    assert seq % 512 == 0 and x.shape[2] == D_MODEL
    tm_in, tm_post, tm_qkv = 512, 512, 512
    xs = x.reshape(batch * seq, D_MODEL)
    row1 = lambda vec: vec.reshape(1, -1)
    main_cols = 3 * A_WIDTH + 4 * B_WIDTH

    blk = 4 * C_HEAD
    ones_bd = (jnp.arange(blk)[:, None] // C_HEAD == jnp.arange(blk)[None, :] // C_HEAD).astype(BF16)

    v_first = None
    for i in range(depth):
        if i % 2 == 0:
            e = i // 2
            has_vres = e > 0
            win = even_w_in[e]
            mus = [rwkv_mu_lora[e, 0], rwkv_mu_lora[e, 1], rwkv_mu_lora[e, 2]]
            l1s = [rwkv_w1[e], rwkv_a1[e], rwkv_g1[e]]
            l2s = [rwkv_w2[e], rwkv_a2[e], rwkv_g2[e]]
            lbs = [rwkv_w0[e], rwkv_a0[e]]
            if has_vres:
                mus.append(rwkv_v_mu[e - 1])
                l1s.append(rwkv_v1[e - 1])
                l2s.append(rwkv_v2[e - 1])
                lbs.append(rwkv_v0[e - 1])
            l1 = jnp.stack([_pad_to(w, cols=A_LORA_PAD) for w in l1s]).astype(BF16)
            l2 = jnp.stack([_pad_to(w, rows=A_LORA_PAD) for w in l2s]).astype(BF16)
            outs = _even_in(xs, seq, row1(norm_mix_g[i]), win[:, :main_cols].astype(BF16),
                            _pad_to(win[:, main_cols:], cols=LANES).astype(BF16),
                            jnp.stack(mus), l1, l2, jnp.stack(lbs), has_vres, tm_in)
            rkv, qkvb, gate, ba, lw, a_lr, gg = outs[:7]
            par = jnp.stack([rwkv_k_k[e], rwkv_k_a[e], rwkv_r_k[e].reshape(-1), rwkv_ln_g[e], rwkv_ln_b[e]])
            y_a, v_first = _rwkv(rkv, lw, a_lr, gg, outs[7] if has_vres else None, v_first, par,
                                 rwkv_mu_proj[e].reshape(1, -1), batch, seq)
            hp = jnp.stack([_pad_to(jnp.pad(row1(gdn_a_log[e]), ((0, 0), (B_HEADS, 0))), cols=LANES)[0],
                            _pad_to(jnp.pad(row1(gdn_dt_bias[e]), ((0, 0), (B_HEADS, 0))), cols=LANES)[0]])
            y_b = _gdn(qkvb, gate, ba, gdn_conv_w[e], hp, row1(gdn_norm_g[e]), batch, seq)
            mixes, wo = [y_a, y_b], even_w_out[e]
        else:
            o = i // 2
            tile4 = lambda gvec: jnp.tile(gvec, blk // C_HEAD).reshape(1, blk)
            qkv_pad = _attn_qkv(xs, row1(norm_mix_g[i]), attn_w_qkv[o].astype(BF16), tile4(attn_q_g[o]),
                                tile4(attn_k_g[o]), ones_bd, batch, seq, tm_qkv)
            mixes, wo = [_attn(qkv_pad, _rel_bias_table(attn_rel_bias[o]), batch, seq)], attn_w_out[o]
        xs = _post(xs, p.reshape(depth, batch * seq, -1), i, mixes, wo.astype(BF16), row1(norm_ffn_g[i]),
                   mlp_w1[i].astype(BF16), mlp_w2[i].astype(BF16), ple_w_proj[i].astype(BF16),
                   row1(ple_norm_g[i]), ple_w_gate[i].astype(BF16), tm_post)
    return xs.reshape(batch, seq, D_MODEL)
```

```python
import functools

import jax
import jax.numpy as jnp
from jax import lax
from jax.experimental import pallas as pl
from jax.experimental.pallas import tpu as pltpu

F32 = jnp.float32
BF16 = jnp.bfloat16

D_MODEL = 1024
CHUNK = 64
NORM_EPS = 1e-6
L2_EPS = 1e-6
A_WIDTH = 512
A_HEAD = 64
A_GN_EPS = 64e-5
A_LORA_PAD = 128
A_GROUP = 256
A_SCAN_ROWS = 4 * CHUNK
B_SCAN_ROWS = 4 * CHUNK
B_WIDTH = 512
B_HEADS = 4
B_HEAD = 128
B_CONV = 4
C_HEADS = 16
C_HEAD = 64
C_WINDOW = 8 * CHUNK
C_BAND = C_WINDOW + CHUNK
C_QTILE = 8 * CHUNK
C_IN_FLIGHT = 2
C_MAX_REL = 256
D_FF = 4096
FF_CHUNK = 1024
LANES = 128
SUBLANES = 8
INV_BLOCK = 16
MASK_VALUE = -1e30
VMEM_LIMIT = 56 * 1024 * 1024


def _dot(a, b):
    return jnp.dot(a.astype(BF16), b.astype(BF16), preferred_element_type=F32)


def _dot_nt(a, b):
    return lax.dot_general(a.astype(BF16), b.astype(BF16), (((1,), (1,)), ((), ())),
                           preferred_element_type=F32)


def _dot_tn(a, b):
    return jnp.dot(a.astype(BF16).T, b.astype(BF16), preferred_element_type=F32)


def _rms(x, g, eps=NORM_EPS):
    return x * lax.rsqrt(jnp.mean(x * x, axis=-1, keepdims=True) + eps) * g


def _sigmoid(z):
    return 1.0 / (1.0 + jnp.exp(-z))


def _softplus(z):
    return jnp.maximum(z, 0.0) + jnp.log(1.0 + jnp.exp(-jnp.abs(z)))


def _shift_rows(x, prev_row):
    row = lax.broadcasted_iota(jnp.int32, x.shape, 0)
    return jnp.where(row == 0, prev_row, pltpu.roll(x, 1, 0))


def _cumsum_rows(x):
    row = lax.broadcasted_iota(jnp.int32, x.shape, 0)
    step = 1
    while step < x.shape[0]:
        x = x + jnp.where(row >= step, pltpu.roll(x, step, 0), 0.0)
        step *= 2
    return x


def _stack_heads(x, n_heads):
    head_w = x.shape[1] // n_heads
    lane_head = lax.broadcasted_iota(jnp.int32, x.shape, 1) >> (head_w.bit_length() - 1)
    zero = jnp.zeros_like(x)
    return jnp.concatenate([jnp.where(lane_head == h, x, zero) for h in range(n_heads)], axis=0)


def _unstack_heads(xs, n_heads):
    out = xs[0:CHUNK]
    for h in range(1, n_heads):
        out = out + xs[h * CHUNK:(h + 1) * CHUNK]
    return out


def _block_masks(n_heads):
    n = n_heads * CHUNK
    r = lax.broadcasted_iota(jnp.int32, (n, n), 0)
    c = lax.broadcasted_iota(jnp.int32, (n, n), 1)
    same_head = (r ^ c) < CHUNK
    delta = jnp.where(same_head, r - c, -1)
    return same_head, delta >= 0, delta > 0, (r ^ c) < INV_BLOCK


def _unit_lower_inverses(n_mats, same_block):
    nds = [jnp.where(same_block, n, 0.0) for n in n_mats]
    nos = [n - nd for n, nd in zip(n_mats, nds)]
    tds, pws = nds, nds
    for _ in range(3):
        pws = [_dot(pw, pw) for pw in pws]
        tds = [td + pw + _dot(td, pw) for td, pw in zip(tds, pws)]
    es = [no + _dot(td, no) for td, no in zip(tds, nos)]
    e2s = [_dot(e, e) for e in es]
    fs = [e + e2 + _dot(e, e2) for e, e2 in zip(es, e2s)]
    return [f + td + _dot(f, td) for f, td in zip(fs, tds)]


def _seg_sum(x):
    seg_lo = lax.broadcasted_iota(jnp.int32, (x.shape[0], LANES), 1) < LANES // 2
    outs = []
    for j in range(x.shape[1] // LANES):
        blk = x[:, j * LANES:(j + 1) * LANES]
        s_lo = jnp.sum(jnp.where(seg_lo, blk, 0.0), axis=-1, keepdims=True)
        s_hi = jnp.sum(jnp.where(seg_lo, 0.0, blk), axis=-1, keepdims=True)
        outs.append(jnp.where(seg_lo, s_lo, s_hi))
    return jnp.concatenate(outs, axis=1)


def _even_in_kernel(x_ref, halo_ref, g_ref, win_ref, wtail_ref, mu_ref, l1_ref, l2_ref, lb_ref,
                    rkv_ref, qkvb_ref, gate_ref, ba_ref, lw_ref, a_ref, gg_ref, *vg_ref,
                    tiles_per_seq):
    i = pl.program_id(0)
    g = g_ref[...]
    h = _rms(x_ref[...], g)
    prev = _rms(halo_ref[...], g)[SUBLANES - 1:SUBLANES, :]
    prev = jnp.where(i % tiles_per_seq == 0, 0.0, prev)
    dh = _shift_rows(h, prev) - h
    hb = h.astype(BF16)
    rkv_ref[...] = _dot(hb, win_ref[:, 0:3 * A_WIDTH])
    qkvb_ref[...] = _dot(hb, win_ref[:, 3 * A_WIDTH:3 * A_WIDTH + 3 * B_WIDTH])
    gate_ref[...] = _dot(hb, win_ref[:, 3 * A_WIDTH + 3 * B_WIDTH:])
    ba_ref[...] = _dot(hb, wtail_ref[...])

    n_lora = 4 if vg_ref else 3
    mid = [_dot(h + dh * mu_ref[j:j + 1, :], l1_ref[j]) for j in range(n_lora)]
    mid[0] = jnp.tanh(mid[0])
    mid[2] = _sigmoid(mid[2])
    up = [_dot(mid[j], l2_ref[j]) for j in range(n_lora)]
    w_log = -_softplus(-(lb_ref[0:1, :] + up[0])) - 0.5
    lw_ref[...] = -jnp.exp(w_log)
    a_ref[...] = _sigmoid(lb_ref[1:2, :] + up[1])
    gg_ref[...] = up[2]
    if vg_ref:
        vg_ref[0][...] = _sigmoid(lb_ref[2:3, :] + up[3])


def _even_in(x2d, seq, g, win, wtail, mu, l1, l2, lb, has_vres, tm):
    m = x2d.shape[0]
    n_out = 8 if has_vres else 7
    widths = [3 * A_WIDTH, 3 * B_WIDTH, B_WIDTH, LANES, A_WIDTH, A_WIDTH, A_WIDTH, A_WIDTH][:n_out]
    const = lambda a: pl.BlockSpec(a.shape, lambda i: (0,) * a.ndim, pipeline_mode=pl.Buffered(1))
    return pl.pallas_call(
        functools.partial(_even_in_kernel, tiles_per_seq=seq // tm),
        grid=(m // tm,),
        in_specs=[pl.BlockSpec((tm, D_MODEL), lambda i: (i, 0)),
                  pl.BlockSpec((SUBLANES, D_MODEL), lambda i: (jnp.maximum(i * (tm // SUBLANES) - 1, 0), 0)),
                  const(g), const(win), const(wtail), const(mu), const(l1), const(l2), const(lb)],
        out_specs=[pl.BlockSpec((tm, w), lambda i: (i, 0)) for w in widths],
        out_shape=[jax.ShapeDtypeStruct((m, w), F32) for w in widths],
        compiler_params=pltpu.CompilerParams(dimension_semantics=("parallel",), vmem_limit_bytes=VMEM_LIMIT),
        name="even_in",
    )(x2d, x2d, g, win, wtail, mu, l1, l2, lb)


def _rwkv_kernel(rkv_ref, lw_ref, a_ref, gg_ref, *rest, has_vres):
    if has_vres:
        vg_ref, vfirst_ref, par_ref, mu_ref, y_ref, state_ref, tail_ref = rest
    else:
        par_ref, mu_ref, y_ref, vfirst_out_ref, state_ref, tail_ref = rest
    c = pl.program_id(1)

    @pl.when(c == 0)
    def _():
        state_ref[...] = jnp.zeros_like(state_ref)
        tail_ref[...] = jnp.zeros_like(tail_ref)

    rkv = rkv_ref[...]
    rkv_prev = _shift_rows(rkv, tail_ref[SUBLANES - 1:SUBLANES, :])
    tail_ref[...] = rkv[rkv.shape[0] - SUBLANES:, :]
    rkv = rkv + (rkv_prev - rkv) * mu_ref[...]
    r = rkv[:, 0:A_WIDTH]
    k = rkv[:, A_WIDTH:2 * A_WIDTH]
    v = rkv[:, 2 * A_WIDTH:]
    k_k, k_a, r_k, ln_g, ln_b = (par_ref[j:j + 1, :] for j in range(5))
    a = a_ref[...]
    if has_vres:
        v = v + (vfirst_ref[...] - v) * vg_ref[...]
    else:
        vfirst_out_ref[...] = v

    lw_all = lw_ref[...]
    kk_all = k * k_k
    kk_all = kk_all * lax.rsqrt(_seg_sum(kk_all * kk_all) + L2_EPS)
    k2_all = k * (1.0 + (a - 1.0) * k_a)
    b_all = kk_all * a
    bonus = _seg_sum(r * k2_all * r_k) * v
    v_bf_all = v.astype(BF16)

    heads = A_GROUP // A_HEAD
    n_groups = A_WIDTH // A_GROUP
    n_rows = heads * CHUNK
    same_head, incl, strict, same_block = _block_masks(heads)
    incl2 = jnp.concatenate([incl, incl], axis=1)
    n_chunks = rkv.shape[0] // CHUNK
    group_sl = [slice(g * A_GROUP, (g + 1) * A_GROUP) for g in range(n_groups)]

    probs = []
    for cc in range(n_chunks):
        rows = slice(cc * CHUNK, (cc + 1) * CHUNK)
        lw, kk, k2, b = lw_all[rows], kk_all[rows], k2_all[rows], b_all[rows]
        cw = _cumsum_rows(lw)
        cw_last = cw[CHUNK - 1:CHUNK, :]
        e_neg = jnp.exp(-cw)
        e_last = jnp.exp(cw_last - cw)
        r_t = (r[rows] * jnp.exp(cw)).astype(BF16)
        a_t = (-kk * jnp.exp(cw - lw)).astype(BF16)
        b_t = (b * e_neg).astype(BF16)
        k_t = (k2 * e_neg).astype(BF16)
        b_hat = (b * e_last).astype(BF16)
        k_hat = (k2 * e_last).astype(BF16)
        for sl in group_sl:
            lhs = jnp.concatenate([_stack_heads(a_t[:, sl], heads), _stack_heads(r_t[:, sl], heads)], axis=0)
            rhs = jnp.concatenate([b_t[:, sl]] * heads + [k_t[:, sl]] * heads, axis=0)
            probs.append(dict(
                rows=rows, sl=sl, gram=_dot_nt(lhs, rhs),
                ar=jnp.concatenate([a_t[:, sl], r_t[:, sl]], axis=0),
                bk_hat=jnp.concatenate([b_hat[:, sl], k_hat[:, sl]], axis=0),
                v=v_bf_all[rows, sl], w_last=jnp.exp(cw_last[:, sl])))
    minvs = _unit_lower_inverses([jnp.where(strict, pr["gram"][0:n_rows, 0:n_rows], 0.0) for pr in probs],
                                 same_block)

    states = [state_ref[g] for g in range(n_groups)]
    for cc in range(n_chunks):
        prs = probs[cc * n_groups:(cc + 1) * n_groups]
        mis = minvs[cc * n_groups:(cc + 1) * n_groups]
        from_state = [_dot_nt(pr["ar"], st) for pr, st in zip(prs, states)]
        v_s = [_stack_heads(pr["v"], heads) for pr in prs]
        xs = [fs[0:CHUNK] + _unstack_heads(_dot(jnp.where(strict, pr["gram"][0:n_rows, n_rows:], 0.0), vs), heads)
              for fs, pr, vs in zip(from_state, prs, v_s)]
        us = [x + _unstack_heads(_dot(mi, _stack_heads(x.astype(BF16), heads)), heads) for x, mi in zip(xs, mis)]
        u_bf = [u.astype(BF16) for u in us]
        ys = [fs[CHUNK:] + _unstack_heads(_dot(jnp.where(incl2, pr["gram"][n_rows:, :], 0.0),
                                               jnp.concatenate([_stack_heads(ub, heads), vs], axis=0)), heads)
              for fs, pr, ub, vs in zip(from_state, prs, u_bf, v_s)]
        upds = [_dot_tn(jnp.concatenate([ub, pr["v"]], axis=0), pr["bk_hat"]) for ub, pr in zip(u_bf, prs)]
        states = [st * pr["w_last"] + jnp.where(same_head, upd, 0.0) for st, pr, upd in zip(states, prs, upds)]
        for pr, y in zip(prs, ys):
            rows, sl = pr["rows"], pr["sl"]
            mean = _seg_sum(y) * (1.0 / A_HEAD)
            yc = y - mean
            var = _seg_sum(yc * yc) * (1.0 / A_HEAD)
            yn = yc * lax.rsqrt(var + A_GN_EPS) * ln_g[:, sl] + ln_b[:, sl]
            y_ref[rows, sl] = ((yn + bonus[rows, sl]) * gg_ref[rows, sl]).astype(y_ref.dtype)
    for g in range(n_groups):
        state_ref[g] = states[g]


def _rwkv(rkv, lw, a, gg, vg, vfirst, par, mu, batch, seq):
    m = rkv.shape[0]
    nc = seq // A_SCAN_ROWS
    has_vres = vg is not None
    row = lambda w: pl.BlockSpec((A_SCAN_ROWS, w), lambda b, c: (b * nc + c, 0))
    const = lambda arr: pl.BlockSpec(arr.shape, lambda b, c: (0,) * arr.ndim)
    ins = [rkv, lw, a, gg] + ([vg, vfirst] if has_vres else []) + [par, mu]
    in_specs = [row(3 * A_WIDTH)] + [row(A_WIDTH)] * (5 if has_vres else 3) + [const(par), const(mu)]
    n_out = 1 if has_vres else 2
    outs = pl.pallas_call(
        functools.partial(_rwkv_kernel, has_vres=has_vres),
        grid=(batch, nc),
        in_specs=in_specs,
        out_specs=[row(A_WIDTH)] * n_out,
        out_shape=[jax.ShapeDtypeStruct((m, A_WIDTH), dt) for dt in (BF16, F32)[:n_out]],
        scratch_shapes=[pltpu.VMEM((A_WIDTH // A_GROUP, A_GROUP, A_GROUP), F32),
                        pltpu.VMEM((SUBLANES, 3 * A_WIDTH), F32)],
        compiler_params=pltpu.CompilerParams(dimension_semantics=("arbitrary", "arbitrary"),
                                             vmem_limit_bytes=VMEM_LIMIT),
        name="rwkv7_scan",
    )(*ins)
    return outs if not has_vres else (outs[0], vfirst)


def _gdn_kernel(qkv_ref, gate_ref, ba_ref, conv_ref, hp_ref, ng_ref, y_ref, state_ref, zz_ref):
    c = pl.program_id(1)

    @pl.when(c == 0)
    def _():
        state_ref[...] = jnp.zeros_like(state_ref)
        zz_ref[0:SUBLANES, :] = jnp.zeros((SUBLANES, 3 * B_WIDTH), F32)

    z = qkv_ref[...]
    zz_ref[SUBLANES:, :] = z
    zz = zz_ref[...]
    zz_ref[0:SUBLANES, :] = z[z.shape[0] - SUBLANES:, :]
    conv = zz[SUBLANES:, :] * conv_ref[B_CONV - 1:B_CONV, :]
    for j in range(B_CONV - 1):
        conv = conv + pltpu.roll(zz, B_CONV - 1 - j, 0)[SUBLANES:, :] * conv_ref[j:j + 1, :]
    qkv = conv * _sigmoid(conv)

    def per_head(tile, lane0):
        return jnp.concatenate([jnp.broadcast_to(tile[:, lane0 + h:lane0 + h + 1], (CHUNK, B_HEAD))
                                for h in range(B_HEADS)], axis=1)

    def l2n(t):
        return jnp.concatenate(
            [t[:, h * B_HEAD:(h + 1) * B_HEAD]
             * lax.rsqrt(jnp.sum(jnp.square(t[:, h * B_HEAD:(h + 1) * B_HEAD]), axis=-1, keepdims=True) + L2_EPS)
             for h in range(B_HEADS)], axis=1)

    q_all = l2n(qkv[:, 0:B_WIDTH]) * (B_HEAD ** -0.5)
    k_all = l2n(qkv[:, B_WIDTH:2 * B_WIDTH])
    v_all = qkv[:, 2 * B_WIDTH:]
    ba_all = ba_ref[...]
    beta_all = _sigmoid(ba_all)
    g_step_all = -jnp.exp(hp_ref[0:1, :]) * _softplus(ba_all + hp_ref[1:2, :])

    same_head, incl, strict, same_block = _block_masks(B_HEADS)
    n_rows = B_HEADS * CHUNK
    probs = []
    for cc in range(z.shape[0] // CHUNK):
        rows = slice(cc * CHUNK, (cc + 1) * CHUNK)
        q, k, v = q_all[rows], k_all[rows], v_all[rows]
        gc = _cumsum_rows(g_step_all[rows])
        gc_t = gc.T
        beta_f = per_head(beta_all[rows], 0)
        g_col = per_head(gc, B_HEADS)
        g_last = g_col[CHUNK - 1:CHUNK, :]
        e_g = jnp.exp(g_col)
        kb = k * beta_f
        g_col_s = jnp.concatenate([gc[:, B_HEADS + h:B_HEADS + h + 1] for h in range(B_HEADS)], axis=0)
        g_row_s = jnp.concatenate([gc_t[B_HEADS + h:B_HEADS + h + 1, :] for h in range(B_HEADS)], axis=1)
        decay = jnp.where(incl, jnp.exp(jnp.where(incl, g_col_s - g_row_s, 0.0)), 0.0)
        lhs = jnp.concatenate([_stack_heads(kb.astype(BF16), B_HEADS), _stack_heads(q.astype(BF16), B_HEADS)],
                              axis=0)
        gram = _dot_nt(lhs, jnp.concatenate([k.astype(BF16)] * B_HEADS, axis=0))
        probs.append(dict(rows=rows, amat=jnp.where(strict, gram[0:n_rows] * decay, 0.0),
                          qk=gram[n_rows:] * decay, vb=v * beta_f, kbg=kb * e_g, qe=q * e_g,
                          kd=(k * jnp.exp(g_last - g_col)).astype(BF16), s_decay=jnp.exp(g_last)))
    tinvs = _unit_lower_inverses([-pr["amat"] for pr in probs], same_block)

    def in_block(data, hd):
        blocks = [jnp.zeros_like(data)] * B_HEADS
        blocks[hd] = data
        return jnp.concatenate(blocks, axis=0)

    head_sl = [slice(hd * B_HEAD, (hd + 1) * B_HEAD) for hd in range(B_HEADS)]
    head_rows = [slice(hd * CHUNK, (hd + 1) * CHUNK) for hd in range(B_HEADS)]
    solved = [[_dot(tinv[head_rows[hd]],
                    in_block(jnp.concatenate([pr["vb"][:, head_sl[hd]], pr["kbg"][:, head_sl[hd]]],
                                             axis=1).astype(BF16), hd))
               for hd in range(B_HEADS)] for tinv, pr in zip(tinvs, probs)]

    states = [state_ref[hd] for hd in range(B_HEADS)]
    for pr, sol in zip(probs, solved):
        rows, qk = pr["rows"], pr["qk"].astype(BF16)
        kd, g_last_e = pr["kd"], pr["s_decay"]
        us = [pr["vb"][:, sl] + s[:, 0:B_HEAD] for sl, s in zip(head_sl, sol)]
        wqs = [jnp.concatenate([pr["kbg"][:, sl] + s[:, B_HEAD:], pr["qe"][:, sl]], axis=0)
               for sl, s in zip(head_sl, sol)]
        from_state = [_dot(wq, st) for wq, st in zip(wqs, states)]
        v_new = [u - fs[0:CHUNK] for u, fs in zip(us, from_state)]
        states = [st * g_last_e[:, sl] + _dot_tn(kd[:, sl], vn) for st, sl, vn in zip(states, head_sl, v_new)]
        outs = [fs[CHUNK:] + _dot(qk[hr], in_block(vn.astype(BF16), hd))
                for hd, (fs, hr, vn) in enumerate(zip(from_state, head_rows, v_new))]
        for sl, o in zip(head_sl, outs):
            gate = gate_ref[rows, sl]
            y_ref[rows, sl] = (_rms(o, ng_ref[...]) * (gate * _sigmoid(gate))).astype(y_ref.dtype)
    for hd in range(B_HEADS):
        state_ref[hd] = states[hd]


def _gdn(qkvb, gate, ba, conv_w, hp, norm_g, batch, seq):
    m = qkvb.shape[0]
    nc = seq // B_SCAN_ROWS
    row = lambda w: pl.BlockSpec((B_SCAN_ROWS, w), lambda b, c: (b * nc + c, 0))
    const = lambda arr: pl.BlockSpec(arr.shape, lambda b, c: (0,) * arr.ndim)
    return pl.pallas_call(
        _gdn_kernel,
        grid=(batch, nc),
        in_specs=[row(3 * B_WIDTH), row(B_WIDTH), row(LANES), const(conv_w), const(hp), const(norm_g)],
        out_specs=row(B_WIDTH),
        out_shape=jax.ShapeDtypeStruct((m, B_WIDTH), BF16),
        scratch_shapes=[pltpu.VMEM((B_HEADS, B_HEAD, B_HEAD), F32),
                        pltpu.VMEM((B_SCAN_ROWS + SUBLANES, 3 * B_WIDTH), F32)],
        compiler_params=pltpu.CompilerParams(dimension_semantics=("arbitrary", "arbitrary"),
                                             vmem_limit_bytes=VMEM_LIMIT),
        name="gdn_scan",
    )(qkvb, gate, ba, conv_w, hp, norm_g)


def _head_rms_many(xs, ones_bd, gs):
    sqs = [x * x for x in xs]
    his = [sq.astype(BF16) for sq in sqs]
    los = [(sq - hi.astype(F32)).astype(BF16) for sq, hi in zip(sqs, his)]
    sums = [jnp.dot(hi, ones_bd, preferred_element_type=F32) for hi in his]
    sums = [s + jnp.dot(lo, ones_bd, preferred_element_type=F32) for s, lo in zip(sums, los)]
    return [x * lax.rsqrt(s * (1.0 / C_HEAD) + NORM_EPS) * g for x, s, g in zip(xs, sums, gs)]


def _attn_qkv_kernel(x_ref, g_ref, w_ref, qg_ref, kg_ref, ones_ref, o_ref):
    t = pl.program_id(1)

    @pl.when(t == 0)
    def _():
        o_ref[...] = jnp.zeros_like(o_ref)

    @pl.when(t > 0)
    def _():
        hb = _rms(x_ref[...], g_ref[...]).astype(BF16)
        ones_bd = ones_ref[...]
        blk = ones_bd.shape[0]
        o_ref[:, 2 * D_MODEL:] = _dot(hb, w_ref[:, 2 * D_MODEL:]).astype(o_ref.dtype)
        q_gain = qg_ref[...] * (C_HEAD ** -0.5)
        per_pass = D_MODEL // blk // 2
        for half in range(2):
            cols = [slice(c * blk, (c + 1) * blk) for c in range(half * per_pass, (half + 1) * per_pass)]
            cols = cols + [slice(D_MODEL + sl.start, D_MODEL + sl.stop) for sl in cols]
            gains = [q_gain] * (len(cols) // 2) + [kg_ref[...]] * (len(cols) // 2)
            normed = _head_rms_many([_dot(hb, w_ref[:, sl]) for sl in cols], ones_bd, gains)
            for sl, val in zip(cols, normed):
                o_ref[:, sl] = val.astype(o_ref.dtype)


def _attn_qkv(x2d, g, w, qg, kg, ones_bd, batch, seq, tm):
    tiles = seq // tm
    pad_tiles = C_WINDOW // tm
    const = lambda arr: pl.BlockSpec(arr.shape, lambda b, t: (0,) * arr.ndim)
    return pl.pallas_call(
        _attn_qkv_kernel,
        grid=(batch, tiles + pad_tiles),
        in_specs=[pl.BlockSpec((tm, D_MODEL), lambda b, t: (b * tiles + jnp.maximum(t - pad_tiles, 0), 0)),
                  const(g), const(w), const(qg), const(kg), const(ones_bd)],
        out_specs=pl.BlockSpec((tm, 3 * D_MODEL), lambda b, t: (b * (tiles + pad_tiles) + t, 0)),
        out_shape=jax.ShapeDtypeStruct((batch * (seq + C_WINDOW), 3 * D_MODEL), BF16),
        compiler_params=pltpu.CompilerParams(dimension_semantics=("parallel", "arbitrary"),
                                             vmem_limit_bytes=VMEM_LIMIT),
        name="attn_qkv",
    )(x2d, g, w, qg, kg, ones_bd)


def _attn_kernel(q_ref, k_ref, v_ref, bias_ref, o_ref):
    t = pl.program_id(2)
    lane_lo = lax.broadcasted_iota(jnp.int32, (CHUNK, LANES), 1) < C_HEAD
    zero = jnp.zeros((CHUNK, LANES), q_ref.dtype)

    def tile(mask_start):
        bias = bias_ref[...]
        key_idx = lax.broadcasted_iota(jnp.int32, (2 * CHUNK, C_BAND), 1)
        groups = [range(c0, c0 + C_IN_FLIGHT) for c0 in range(0, C_QTILE // CHUNK, C_IN_FLIGHT)]

        def first_key(cc):
            return pl.multiple_of(t * C_QTILE + cc * CHUNK, CHUNK)

        def qk(chunks):
            scores = []
            for cc in chunks:
                q = q_ref[cc * CHUNK:(cc + 1) * CHUNK, :]
                q2 = jnp.concatenate([jnp.where(lane_lo, q, zero), jnp.where(lane_lo, zero, q)], axis=0)
                s = _dot_nt(q2, k_ref[pl.ds(first_key(cc), C_BAND), :]) + bias
                if mask_start:
                    s = jnp.where(key_idx >= C_WINDOW - first_key(cc), s, MASK_VALUE)
                scores.append(s)
            return scores

        def softmax(scores):
            probs = [jnp.exp(s - jnp.max(s, axis=-1, keepdims=True)) for s in scores]
            return probs, [jnp.sum(p, axis=-1, keepdims=True) for p in probs]

        def pv(chunks, probs, denoms):
            outs = [_dot(p, v_ref[pl.ds(first_key(cc), C_BAND), :]) / d for cc, p, d in zip(chunks, probs, denoms)]
            for cc, o2 in zip(chunks, outs):
                o_ref[cc * CHUNK:(cc + 1) * CHUNK, :] = (
                    jnp.where(lane_lo, o2[0:CHUNK], o2[CHUNK:]).astype(o_ref.dtype))

        scores, normed = {}, {}
        for step in range(len(groups) + 2):
            if 0 <= step - 2:
                pv(groups[step - 2], *normed.pop(step - 2))
            if step < len(groups):
                scores[step] = qk(groups[step])
            if 0 <= step - 1 < len(groups):
                normed[step - 1] = softmax(scores.pop(step - 1))

    first_tiles = -(-C_WINDOW // C_QTILE)
    pl.when(t < first_tiles)(functools.partial(tile, True))
    pl.when(t >= first_tiles)(functools.partial(tile, False))


def _attn(qkv_pad, bias, batch, seq):
    nt = seq // C_QTILE
    pad_rows = seq + C_WINDOW
    n_pairs = D_MODEL // LANES
    q_off = C_WINDOW // C_QTILE
    return pl.pallas_call(
        _attn_kernel,
        grid=(batch, n_pairs, nt),
        in_specs=[pl.BlockSpec((C_QTILE, LANES), lambda b, h, t: (b * (pad_rows // C_QTILE) + q_off + t, h)),
                  pl.BlockSpec((pad_rows, LANES), lambda b, h, t: (b, n_pairs + h)),
                  pl.BlockSpec((pad_rows, LANES), lambda b, h, t: (b, 2 * n_pairs + h)),
                  pl.BlockSpec((2 * CHUNK, C_BAND), lambda b, h, t: (h, 0))],
        out_specs=pl.BlockSpec((C_QTILE, LANES), lambda b, h, t: (b * nt + t, h)),
        out_shape=jax.ShapeDtypeStruct((batch * seq, D_MODEL), BF16),
        compiler_params=pltpu.CompilerParams(dimension_semantics=("parallel", "parallel", "arbitrary"),
                                             vmem_limit_bytes=VMEM_LIMIT),
        name="band_attn",
    )(qkv_pad, qkv_pad, qkv_pad, bias.reshape(-1, C_BAND))


def _post_kernel(x_ref, p_ref, *rest, n_mix):
    mix_refs = rest[:n_mix]
    wo_ref, g_ref, w1_ref, w2_ref, wp_ref, pg_ref, wg_ref, o_ref = rest[n_mix:]
    mix = jnp.concatenate([mref[...] for mref in mix_refs], axis=1)
    x = x_ref[...] + _dot(mix, wo_ref[...])
    hb = _rms(x, g_ref[...]).astype(BF16)
    acc = None
    for j in range(D_FF // FF_CHUNK):
        sl = slice(j * FF_CHUNK, (j + 1) * FF_CHUNK)
        hid = jnp.maximum(_dot(hb, w1_ref[:, sl]), 0.0)
        part = _dot(hid * hid, w2_ref[sl, :])
        acc = part if acc is None else acc + part
    x = x + acc
    emb = _rms(_dot(p_ref[...], wp_ref[...]), pg_ref[...])
    o_ref[...] = x + emb * _sigmoid(_dot(x, wg_ref[...]))


def _post(x2d, p3d, layer, mixes, wo, g, w1, w2, wp, pg, wg, tm):
    m = x2d.shape[0]
    const = lambda arr: pl.BlockSpec(arr.shape, lambda i: (0,) * arr.ndim, pipeline_mode=pl.Buffered(1))
    row = lambda w: pl.BlockSpec((tm, w), lambda i: (i, 0))
    weights = [wo, g, w1, w2, wp, pg, wg]
    return pl.pallas_call(
        functools.partial(_post_kernel, n_mix=len(mixes)),
        grid=(m // tm,),
        in_specs=[row(D_MODEL), pl.BlockSpec((None, tm, p3d.shape[2]), lambda i: (layer, i, 0))]
                 + [row(mx.shape[1]) for mx in mixes] + [const(w) for w in weights],
        out_specs=row(D_MODEL),
        out_shape=jax.ShapeDtypeStruct((m, D_MODEL), F32),
        compiler_params=pltpu.CompilerParams(dimension_semantics=("parallel",), vmem_limit_bytes=VMEM_LIMIT),
        name="mix_out_mlp_ple",
    )(x2d, p3d, *mixes, *weights)


def _pad_to(w, rows=None, cols=None):
    r = (rows or w.shape[0]) - w.shape[0]
    c = (cols or w.shape[1]) - w.shape[1]
    return jnp.pad(w, ((0, r), (0, c)))


def _rel_bias_table(rel_bias):
    n_heads = rel_bias.shape[0]
    span = CHUNK + C_BAND - 1
    rel = (C_BAND - 1) - jnp.arange(span)
    f = rel_bias[:, jnp.clip(rel, -C_MAX_REL, C_MAX_REL) + C_MAX_REL]
    g = jnp.tile(jnp.pad(f, ((0, 0), (0, 1))), (1, CHUNK))[:, :CHUNK * span].reshape(n_heads, CHUNK, span)
    return g[:, :, CHUNK - 1:CHUNK - 1 + C_BAND]


def kernel(x, p, norm_mix_g, norm_ffn_g, even_w_in, rwkv_mu_proj, rwkv_mu_lora, rwkv_w0, rwkv_w1, rwkv_w2, rwkv_a0, rwkv_a1, rwkv_a2, rwkv_g1, rwkv_g2, rwkv_k_k, rwkv_k_a, rwkv_r_k, rwkv_ln_g, rwkv_ln_b, rwkv_v_mu, rwkv_v0, rwkv_v1, rwkv_v2, gdn_conv_w, gdn_a_log, gdn_dt_bias, gdn_norm_g, even_w_out, attn_w_qkv, attn_q_g, attn_k_g, attn_rel_bias, attn_w_out, mlp_w1, mlp_w2, ple_w_proj, ple_norm_g, ple_w_gate):
    batch, seq, _ = x.shape
    depth = p.shape[0]
    assert seq % 512 == 0 and x.shape[2] == D_MODEL
    tm_in, tm_post, tm_qkv = 512, 512, 512
    xs = x.reshape(batch * seq, D_MODEL)
    row1 = lambda vec: vec.reshape(1, -1)
    main_cols = 3 * A_WIDTH + 4 * B_WIDTH

    blk = 4 * C_HEAD
    ones_bd = (jnp.arange(blk)[:, None] // C_HEAD == jnp.arange(blk)[None, :] // C_HEAD).astype(BF16)

    v_first = None
    for i in range(depth):
        if i % 2 == 0:
            e = i // 2
            has_vres = e > 0
            win = even_w_in[e]
            mus = [rwkv_mu_lora[e, 0], rwkv_mu_lora[e, 1], rwkv_mu_lora[e, 2]]
            l1s = [rwkv_w1[e], rwkv_a1[e], rwkv_g1[e]]
            l2s = [rwkv_w2[e], rwkv_a2[e], rwkv_g2[e]]
            lbs = [rwkv_w0[e], rwkv_a0[e]]
            if has_vres:
                mus.append(rwkv_v_mu[e - 1])
                l1s.append(rwkv_v1[e - 1])
                l2s.append(rwkv_v2[e - 1])
                lbs.append(rwkv_v0[e - 1])
            l1 = jnp.stack([_pad_to(w, cols=A_LORA_PAD) for w in l1s]).astype(BF16)
            l2 = jnp.stack([_pad_to(w, rows=A_LORA_PAD) for w in l2s]).astype(BF16)
            outs = _even_in(xs, seq, row1(norm_mix_g[i]), win[:, :main_cols].astype(BF16),
                            _pad_to(win[:, main_cols:], cols=LANES).astype(BF16),
                            jnp.stack(mus), l1, l2, jnp.stack(lbs), has_vres, tm_in)
            rkv, qkvb, gate, ba, lw, a_lr, gg = outs[:7]
            par = jnp.stack([rwkv_k_k[e], rwkv_k_a[e], rwkv_r_k[e].reshape(-1), rwkv_ln_g[e], rwkv_ln_b[e]])
            y_a, v_first = _rwkv(rkv, lw, a_lr, gg, outs[7] if has_vres else None, v_first, par,
                                 rwkv_mu_proj[e].reshape(1, -1), batch, seq)
            hp = jnp.stack([_pad_to(jnp.pad(row1(gdn_a_log[e]), ((0, 0), (B_HEADS, 0))), cols=LANES)[0],
                            _pad_to(jnp.pad(row1(gdn_dt_bias[e]), ((0, 0), (B_HEADS, 0))), cols=LANES)[0]])
            y_b = _gdn(qkvb, gate, ba, gdn_conv_w[e], hp, row1(gdn_norm_g[e]), batch, seq)
            mixes, wo = [y_a, y_b], even_w_out[e]
        else:
            o = i // 2
            tile4 = lambda gvec: jnp.tile(gvec, blk // C_HEAD).reshape(1, blk)
            qkv_pad = _attn_qkv(xs, row1(norm_mix_g[i]), attn_w_qkv[o].astype(BF16), tile4(attn_q_g[o]),
                                tile4(attn_k_g[o]), ones_bd, batch, seq, tm_qkv)
            mixes, wo = [_attn(qkv_pad, _rel_bias_table(attn_rel_bias[o]), batch, seq)], attn_w_out[o]
        xs = _post(xs, p.reshape(depth, batch * seq, -1), i, mixes, wo.astype(BF16), row1(norm_ffn_g[i]),
                   mlp_w1[i].astype(BF16), mlp_w2[i].astype(BF16), ple_w_proj[i].astype(BF16),
                   row1(ple_norm_g[i]), ple_w_gate[i].astype(BF16), tm_post)
    return xs.reshape(batch, seq, D_MODEL)
```

```python
import functools

import jax
import jax.numpy as jnp
from jax import lax
from jax.experimental import pallas as pl
from jax.experimental.pallas import tpu as pltpu

F32 = jnp.float32
BF16 = jnp.bfloat16

D_MODEL = 1024
CHUNK = 64
NORM_EPS = 1e-6
L2_EPS = 1e-6
A_WIDTH = 512
A_HEAD = 64
A_GN_EPS = 64e-5
A_LORA_PAD = 128
A_GROUP = 256
A_SCAN_ROWS = 8 * CHUNK
B_SCAN_ROWS = 8 * CHUNK
A_UNIT = 4
B_UNIT = 4
B_WIDTH = 512
B_HEADS = 4
B_HEAD = 128
B_CONV = 4
C_HEADS = 16
C_HEAD = 64
C_WINDOW = 8 * CHUNK
C_BAND = C_WINDOW + CHUNK
C_QTILE = 8 * CHUNK
C_IN_FLIGHT = 2
C_MAX_REL = 256
D_FF = 4096
FF_CHUNK = 1024
LANES = 128
SUBLANES = 8
INV_BLOCK = 16
MASK_VALUE = -1e30
VMEM_LIMIT = 56 * 1024 * 1024


def _dot(a, b):
    return jnp.dot(a.astype(BF16), b.astype(BF16), preferred_element_type=F32)


def _dot_nt(a, b):
    return lax.dot_general(a.astype(BF16), b.astype(BF16), (((1,), (1,)), ((), ())),
                           preferred_element_type=F32)


def _dot_tn(a, b):
    return jnp.dot(a.astype(BF16).T, b.astype(BF16), preferred_element_type=F32)


def _rms(x, g, eps=NORM_EPS):
    return x * lax.rsqrt(jnp.mean(x * x, axis=-1, keepdims=True) + eps) * g


def _sigmoid(z):
    return 1.0 / (1.0 + jnp.exp(-z))


def _softplus(z):
    return jnp.maximum(z, 0.0) + jnp.log(1.0 + jnp.exp(-jnp.abs(z)))


def _shift_rows(x, prev_row):
    row = lax.broadcasted_iota(jnp.int32, x.shape, 0)
    return jnp.where(row == 0, prev_row, pltpu.roll(x, 1, 0))


def _cumsum_rows(x):
    row = lax.broadcasted_iota(jnp.int32, x.shape, 0)
    step = 1
    while step < x.shape[0]:
        x = x + jnp.where(row >= step, pltpu.roll(x, step, 0), 0.0)
        step *= 2
    return x


def _stack_heads(x, n_heads):
    head_w = x.shape[1] // n_heads
    lane_head = lax.broadcasted_iota(jnp.int32, x.shape, 1) >> (head_w.bit_length() - 1)
    zero = jnp.zeros_like(x)
    return jnp.concatenate([jnp.where(lane_head == h, x, zero) for h in range(n_heads)], axis=0)


def _unstack_heads(xs, n_heads):
    out = xs[0:CHUNK]
    for h in range(1, n_heads):
        out = out + xs[h * CHUNK:(h + 1) * CHUNK]
    return out


def _block_masks(n_heads):
    n = n_heads * CHUNK
    r = lax.broadcasted_iota(jnp.int32, (n, n), 0)
    c = lax.broadcasted_iota(jnp.int32, (n, n), 1)
    same_head = (r ^ c) < CHUNK
    delta = jnp.where(same_head, r - c, -1)
    return same_head, delta >= 0, delta > 0, (r ^ c) < INV_BLOCK


def _unit_lower_inverses(n_mats, same_block):
    nds = [jnp.where(same_block, n, 0.0) for n in n_mats]
    nos = [n - nd for n, nd in zip(n_mats, nds)]
    tds, pws = nds, nds
    for _ in range(3):
        pws = [_dot(pw, pw) for pw in pws]
        yield
        tds = [td + pw + _dot(td, pw) for td, pw in zip(tds, pws)]
        yield
    es = [no + _dot(td, no) for td, no in zip(tds, nos)]
    yield
    e2s = [_dot(e, e) for e in es]
    yield
    fs = [e + e2 + _dot(e, e2) for e, e2 in zip(es, e2s)]
    yield
    return [f + td + _dot(f, td) for f, td in zip(fs, tds)]


def _pipeline(units, front, tail):
    ready = None
    for step in range(len(units) + 1):
        gens = ([front(units[step])] if step < len(units) else []) + ([tail(*ready)] if ready is not None else [])
        results = [None] * len(gens)
        live = list(range(len(gens)))
        while live:
            for i in list(live):
                try:
                    next(gens[i])
                except StopIteration as done:
                    results[i] = done.value
                    live.remove(i)
        ready = results[0] if step < len(units) else None


def _seg_sum(x):
    seg_lo = lax.broadcasted_iota(jnp.int32, (x.shape[0], LANES), 1) < LANES // 2
    outs = []
    for j in range(x.shape[1] // LANES):
        blk = x[:, j * LANES:(j + 1) * LANES]
        s_lo = jnp.sum(jnp.where(seg_lo, blk, 0.0), axis=-1, keepdims=True)
        s_hi = jnp.sum(jnp.where(seg_lo, 0.0, blk), axis=-1, keepdims=True)
        outs.append(jnp.where(seg_lo, s_lo, s_hi))
    return jnp.concatenate(outs, axis=1)


def _even_in_kernel(x_ref, halo_ref, g_ref, win_ref, wtail_ref, mu_ref, l1_ref, l2_ref, lb_ref,
                    rkv_ref, qkvb_ref, gate_ref, ba_ref, lw_ref, a_ref, gg_ref, *vg_ref,
                    tiles_per_seq):
    i = pl.program_id(0)
    g = g_ref[...]
    h = _rms(x_ref[...], g)
    prev = _rms(halo_ref[...], g)[SUBLANES - 1:SUBLANES, :]
    prev = jnp.where(i % tiles_per_seq == 0, 0.0, prev)
    dh = _shift_rows(h, prev) - h
    hb = h.astype(BF16)
    rkv_ref[...] = _dot(hb, win_ref[:, 0:3 * A_WIDTH])
    qkvb_ref[...] = _dot(hb, win_ref[:, 3 * A_WIDTH:3 * A_WIDTH + 3 * B_WIDTH])
    gate_ref[...] = _dot(hb, win_ref[:, 3 * A_WIDTH + 3 * B_WIDTH:])
    ba_ref[...] = _dot(hb, wtail_ref[...])

    n_lora = 4 if vg_ref else 3
    mid = [_dot(h + dh * mu_ref[j:j + 1, :], l1_ref[j]) for j in range(n_lora)]
    mid[0] = jnp.tanh(mid[0])
    mid[2] = _sigmoid(mid[2])
    up = [_dot(mid[j], l2_ref[j]) for j in range(n_lora)]
    w_log = -_softplus(-(lb_ref[0:1, :] + up[0])) - 0.5
    lw_ref[...] = -jnp.exp(w_log)
    a_ref[...] = _sigmoid(lb_ref[1:2, :] + up[1])
    gg_ref[...] = up[2]
    if vg_ref:
        vg_ref[0][...] = _sigmoid(lb_ref[2:3, :] + up[3])


def _even_in(x2d, seq, g, win, wtail, mu, l1, l2, lb, has_vres, tm):
    m = x2d.shape[0]
    n_out = 8 if has_vres else 7
    widths = [3 * A_WIDTH, 3 * B_WIDTH, B_WIDTH, LANES, A_WIDTH, A_WIDTH, A_WIDTH, A_WIDTH][:n_out]
    const = lambda a: pl.BlockSpec(a.shape, lambda i: (0,) * a.ndim, pipeline_mode=pl.Buffered(1))
    return pl.pallas_call(
        functools.partial(_even_in_kernel, tiles_per_seq=seq // tm),
        grid=(m // tm,),
        in_specs=[pl.BlockSpec((tm, D_MODEL), lambda i: (i, 0)),
                  pl.BlockSpec((SUBLANES, D_MODEL), lambda i: (jnp.maximum(i * (tm // SUBLANES) - 1, 0), 0)),
                  const(g), const(win), const(wtail), const(mu), const(l1), const(l2), const(lb)],
        out_specs=[pl.BlockSpec((tm, w), lambda i: (i, 0)) for w in widths],
        out_shape=[jax.ShapeDtypeStruct((m, w), F32) for w in widths],
        compiler_params=pltpu.CompilerParams(dimension_semantics=("parallel",), vmem_limit_bytes=VMEM_LIMIT),
        name="even_in",
    )(x2d, x2d, g, win, wtail, mu, l1, l2, lb)


def _rwkv_kernel(rkv_ref, lw_ref, a_ref, gg_ref, *rest, has_vres):
    if has_vres:
        vg_ref, vfirst_ref, par_ref, mu_ref, y_ref, state_ref, tail_ref = rest
    else:
        par_ref, mu_ref, y_ref, vfirst_out_ref, state_ref, tail_ref = rest
    c = pl.program_id(1)

    @pl.when(c == 0)
    def _():
        state_ref[...] = jnp.zeros_like(state_ref)
        tail_ref[...] = jnp.zeros_like(tail_ref)

    rkv = rkv_ref[...]
    rkv_prev = _shift_rows(rkv, tail_ref[SUBLANES - 1:SUBLANES, :])
    tail_ref[...] = rkv[rkv.shape[0] - SUBLANES:, :]
    rkv = rkv + (rkv_prev - rkv) * mu_ref[...]
    r = rkv[:, 0:A_WIDTH]
    k = rkv[:, A_WIDTH:2 * A_WIDTH]
    v = rkv[:, 2 * A_WIDTH:]
    k_k, k_a, r_k, ln_g, ln_b = (par_ref[j:j + 1, :] for j in range(5))
    a = a_ref[...]
    if has_vres:
        v = v + (vfirst_ref[...] - v) * vg_ref[...]
    else:
        vfirst_out_ref[...] = v

    lw_all = lw_ref[...]
    kk_all = k * k_k
    kk_all = kk_all * lax.rsqrt(_seg_sum(kk_all * kk_all) + L2_EPS)
    k2_all = k * (1.0 + (a - 1.0) * k_a)
    b_all = kk_all * a
    bonus = _seg_sum(r * k2_all * r_k) * v
    v_bf_all = v.astype(BF16)

    heads = A_GROUP // A_HEAD
    n_groups = A_WIDTH // A_GROUP
    n_rows = heads * CHUNK
    same_head, incl, strict, same_block = _block_masks(heads)
    incl2 = jnp.concatenate([incl, incl], axis=1)
    n_chunks = rkv.shape[0] // CHUNK
    group_sl = [slice(g * A_GROUP, (g + 1) * A_GROUP) for g in range(n_groups)]

    states = [state_ref[g] for g in range(n_groups)]

    def front(unit):
        probs = []
        for cc in unit:
            rows = slice(cc * CHUNK, (cc + 1) * CHUNK)
            lw, kk, k2, b = lw_all[rows], kk_all[rows], k2_all[rows], b_all[rows]
            cw = _cumsum_rows(lw)
            cw_last = cw[CHUNK - 1:CHUNK, :]
            e_neg = jnp.exp(-cw)
            e_last = jnp.exp(cw_last - cw)
            r_t = (r[rows] * jnp.exp(cw)).astype(BF16)
            a_t = (-kk * jnp.exp(cw - lw)).astype(BF16)
            b_t = (b * e_neg).astype(BF16)
            k_t = (k2 * e_neg).astype(BF16)
            b_hat = (b * e_last).astype(BF16)
            k_hat = (k2 * e_last).astype(BF16)
            yield
            for sl in group_sl:
                lhs = jnp.concatenate([_stack_heads(a_t[:, sl], heads), _stack_heads(r_t[:, sl], heads)], axis=0)
                rhs = jnp.concatenate([b_t[:, sl]] * heads + [k_t[:, sl]] * heads, axis=0)
                probs.append(dict(
                    rows=rows, sl=sl, gram=_dot_nt(lhs, rhs),
                    ar=jnp.concatenate([a_t[:, sl], r_t[:, sl]], axis=0),
                    bk_hat=jnp.concatenate([b_hat[:, sl], k_hat[:, sl]], axis=0),
                    v=v_bf_all[rows, sl], w_last=jnp.exp(cw_last[:, sl])))
            yield
        minvs = yield from _unit_lower_inverses(
            [jnp.where(strict, pr["gram"][0:n_rows, 0:n_rows], 0.0) for pr in probs], same_block)
        return probs, minvs

    def tail(probs, minvs):
        for first in range(0, len(probs), n_groups):
            prs, mis = probs[first:first + n_groups], minvs[first:first + n_groups]
            from_state = [_dot_nt(pr["ar"], st) for pr, st in zip(prs, states)]
            v_s = [_stack_heads(pr["v"], heads) for pr in prs]
            yield
            xs = [fs[0:CHUNK]
                  + _unstack_heads(_dot(jnp.where(strict, pr["gram"][0:n_rows, n_rows:], 0.0), vs), heads)
                  for fs, pr, vs in zip(from_state, prs, v_s)]
            yield
            us = [x + _unstack_heads(_dot(mi, _stack_heads(x.astype(BF16), heads)), heads) for x, mi in zip(xs, mis)]
            u_bf = [u.astype(BF16) for u in us]
            yield
            ys = [fs[CHUNK:] + _unstack_heads(_dot(jnp.where(incl2, pr["gram"][n_rows:, :], 0.0),
                                                   jnp.concatenate([_stack_heads(ub, heads), vs], axis=0)), heads)
                  for fs, pr, ub, vs in zip(from_state, prs, u_bf, v_s)]
            upds = [_dot_tn(jnp.concatenate([ub, pr["v"]], axis=0), pr["bk_hat"]) for ub, pr in zip(u_bf, prs)]
            yield
            states[:] = [st * pr["w_last"] + jnp.where(same_head, upd, 0.0)
                         for st, pr, upd in zip(states, prs, upds)]
            for pr, y in zip(prs, ys):
                rows, sl = pr["rows"], pr["sl"]
                mean = _seg_sum(y) * (1.0 / A_HEAD)
                yc = y - mean
                var = _seg_sum(yc * yc) * (1.0 / A_HEAD)
                yn = yc * lax.rsqrt(var + A_GN_EPS) * ln_g[:, sl] + ln_b[:, sl]
                y_ref[rows, sl] = ((yn + bonus[rows, sl]) * gg_ref[rows, sl]).astype(y_ref.dtype)
            yield

    _pipeline([range(c0, c0 + A_UNIT) for c0 in range(0, n_chunks, A_UNIT)], front, tail)
    for g in range(n_groups):
        state_ref[g] = states[g]


def _rwkv(rkv, lw, a, gg, vg, vfirst, par, mu, batch, seq):
    m = rkv.shape[0]
    nc = seq // A_SCAN_ROWS
    has_vres = vg is not None
    row = lambda w: pl.BlockSpec((A_SCAN_ROWS, w), lambda b, c: (b * nc + c, 0))
    const = lambda arr: pl.BlockSpec(arr.shape, lambda b, c: (0,) * arr.ndim)
    ins = [rkv, lw, a, gg] + ([vg, vfirst] if has_vres else []) + [par, mu]
    in_specs = [row(3 * A_WIDTH)] + [row(A_WIDTH)] * (5 if has_vres else 3) + [const(par), const(mu)]
    n_out = 1 if has_vres else 2
    outs = pl.pallas_call(
        functools.partial(_rwkv_kernel, has_vres=has_vres),
        grid=(batch, nc),
        in_specs=in_specs,
        out_specs=[row(A_WIDTH)] * n_out,
        out_shape=[jax.ShapeDtypeStruct((m, A_WIDTH), dt) for dt in (BF16, F32)[:n_out]],
        scratch_shapes=[pltpu.VMEM((A_WIDTH // A_GROUP, A_GROUP, A_GROUP), F32),
                        pltpu.VMEM((SUBLANES, 3 * A_WIDTH), F32)],
        compiler_params=pltpu.CompilerParams(dimension_semantics=("arbitrary", "arbitrary"),
                                             vmem_limit_bytes=VMEM_LIMIT),
        name="rwkv7_scan",
    )(*ins)
    return outs if not has_vres else (outs[0], vfirst)


def _gdn_kernel(qkv_ref, gate_ref, ba_ref, conv_ref, hp_ref, ng_ref, y_ref, state_ref, zz_ref):
    c = pl.program_id(1)

    @pl.when(c == 0)
    def _():
        state_ref[...] = jnp.zeros_like(state_ref)
        zz_ref[0:SUBLANES, :] = jnp.zeros((SUBLANES, 3 * B_WIDTH), F32)

    z = qkv_ref[...]
    zz_ref[SUBLANES:, :] = z
    zz = zz_ref[...]
    zz_ref[0:SUBLANES, :] = z[z.shape[0] - SUBLANES:, :]
    conv = zz[SUBLANES:, :] * conv_ref[B_CONV - 1:B_CONV, :]
    for j in range(B_CONV - 1):
        conv = conv + pltpu.roll(zz, B_CONV - 1 - j, 0)[SUBLANES:, :] * conv_ref[j:j + 1, :]
    qkv = conv * _sigmoid(conv)

    def per_head(tile, lane0):
        return jnp.concatenate([jnp.broadcast_to(tile[:, lane0 + h:lane0 + h + 1], (CHUNK, B_HEAD))
                                for h in range(B_HEADS)], axis=1)

    def l2n(t):
        return jnp.concatenate(
            [t[:, h * B_HEAD:(h + 1) * B_HEAD]
             * lax.rsqrt(jnp.sum(jnp.square(t[:, h * B_HEAD:(h + 1) * B_HEAD]), axis=-1, keepdims=True) + L2_EPS)
             for h in range(B_HEADS)], axis=1)

    q_all = l2n(qkv[:, 0:B_WIDTH]) * (B_HEAD ** -0.5)
    k_all = l2n(qkv[:, B_WIDTH:2 * B_WIDTH])
    v_all = qkv[:, 2 * B_WIDTH:]
    ba_all = ba_ref[...]
    beta_all = _sigmoid(ba_all)
    g_step_all = -jnp.exp(hp_ref[0:1, :]) * _softplus(ba_all + hp_ref[1:2, :])

    same_head, incl, strict, same_block = _block_masks(B_HEADS)
    n_rows = B_HEADS * CHUNK
    def in_block(data, hd):
        blocks = [jnp.zeros_like(data)] * B_HEADS
        blocks[hd] = data
        return jnp.concatenate(blocks, axis=0)

    head_sl = [slice(hd * B_HEAD, (hd + 1) * B_HEAD) for hd in range(B_HEADS)]
    head_rows = [slice(hd * CHUNK, (hd + 1) * CHUNK) for hd in range(B_HEADS)]
    states = [state_ref[hd] for hd in range(B_HEADS)]

    def front(unit):
        probs = []
        for cc in unit:
            rows = slice(cc * CHUNK, (cc + 1) * CHUNK)
            q, k, v = q_all[rows], k_all[rows], v_all[rows]
            gc = _cumsum_rows(g_step_all[rows])
            gc_t = gc.T
            beta_f = per_head(beta_all[rows], 0)
            g_col = per_head(gc, B_HEADS)
            g_last = g_col[CHUNK - 1:CHUNK, :]
            e_g = jnp.exp(g_col)
            kb = k * beta_f
            g_col_s = jnp.concatenate([gc[:, B_HEADS + h:B_HEADS + h + 1] for h in range(B_HEADS)], axis=0)
            g_row_s = jnp.concatenate([gc_t[B_HEADS + h:B_HEADS + h + 1, :] for h in range(B_HEADS)], axis=1)
            decay = jnp.where(incl, jnp.exp(jnp.where(incl, g_col_s - g_row_s, 0.0)), 0.0)
            yield
            lhs = jnp.concatenate([_stack_heads(kb.astype(BF16), B_HEADS), _stack_heads(q.astype(BF16), B_HEADS)],
                                  axis=0)
            gram = _dot_nt(lhs, jnp.concatenate([k.astype(BF16)] * B_HEADS, axis=0))
            probs.append(dict(rows=rows, amat=jnp.where(strict, gram[0:n_rows] * decay, 0.0),
                              qk=(gram[n_rows:] * decay).astype(BF16), vb=v * beta_f, kbg=kb * e_g, qe=q * e_g,
                              kd=(k * jnp.exp(g_last - g_col)).astype(BF16), s_decay=jnp.exp(g_last)))
            yield
        tinvs = yield from _unit_lower_inverses([-pr["amat"] for pr in probs], same_block)
        solved = [[_dot(tinv[head_rows[hd]],
                        in_block(jnp.concatenate([pr["vb"][:, head_sl[hd]], pr["kbg"][:, head_sl[hd]]],
                                                 axis=1).astype(BF16), hd))
                   for hd in range(B_HEADS)] for tinv, pr in zip(tinvs, probs)]
        return probs, solved

    def tail(probs, solved):
        for pr, sol in zip(probs, solved):
            rows, qk, kd, g_last_e = pr["rows"], pr["qk"], pr["kd"], pr["s_decay"]
            us = [pr["vb"][:, sl] + s[:, 0:B_HEAD] for sl, s in zip(head_sl, sol)]
            wqs = [jnp.concatenate([pr["kbg"][:, sl] + s[:, B_HEAD:], pr["qe"][:, sl]], axis=0)
                   for sl, s in zip(head_sl, sol)]
            from_state = [_dot(wq, st) for wq, st in zip(wqs, states)]
            yield
            v_new = [u - fs[0:CHUNK] for u, fs in zip(us, from_state)]
            states[:] = [st * g_last_e[:, sl] + _dot_tn(kd[:, sl], vn) for st, sl, vn in zip(states, head_sl, v_new)]
            yield
            outs = [fs[CHUNK:] + _dot(qk[hr], in_block(vn.astype(BF16), hd))
                    for hd, (fs, hr, vn) in enumerate(zip(from_state, head_rows, v_new))]
            yield
            for sl, o in zip(head_sl, outs):
                gate = gate_ref[rows, sl]
                y_ref[rows, sl] = (_rms(o, ng_ref[...]) * (gate * _sigmoid(gate))).astype(y_ref.dtype)
            yield

    _pipeline([range(c0, c0 + B_UNIT) for c0 in range(0, z.shape[0] // CHUNK, B_UNIT)], front, tail)
    for hd in range(B_HEADS):
        state_ref[hd] = states[hd]


def _gdn(qkvb, gate, ba, conv_w, hp, norm_g, batch, seq):
    m = qkvb.shape[0]
    nc = seq // B_SCAN_ROWS
    row = lambda w: pl.BlockSpec((B_SCAN_ROWS, w), lambda b, c: (b * nc + c, 0))
    const = lambda arr: pl.BlockSpec(arr.shape, lambda b, c: (0,) * arr.ndim)
    return pl.pallas_call(
        _gdn_kernel,
        grid=(batch, nc),
        in_specs=[row(3 * B_WIDTH), row(B_WIDTH), row(LANES), const(conv_w), const(hp), const(norm_g)],
        out_specs=row(B_WIDTH),
        out_shape=jax.ShapeDtypeStruct((m, B_WIDTH), BF16),
        scratch_shapes=[pltpu.VMEM((B_HEADS, B_HEAD, B_HEAD), F32),
                        pltpu.VMEM((B_SCAN_ROWS + SUBLANES, 3 * B_WIDTH), F32)],
        compiler_params=pltpu.CompilerParams(dimension_semantics=("arbitrary", "arbitrary"),
                                             vmem_limit_bytes=VMEM_LIMIT),
        name="gdn_scan",
    )(qkvb, gate, ba, conv_w, hp, norm_g)


def _head_rms_many(xs, ones_bd, gs):
    sqs = [x * x for x in xs]
    his = [sq.astype(BF16) for sq in sqs]
    los = [(sq - hi.astype(F32)).astype(BF16) for sq, hi in zip(sqs, his)]
    sums = [jnp.dot(hi, ones_bd, preferred_element_type=F32) for hi in his]
    sums = [s + jnp.dot(lo, ones_bd, preferred_element_type=F32) for s, lo in zip(sums, los)]
    return [x * lax.rsqrt(s * (1.0 / C_HEAD) + NORM_EPS) * g for x, s, g in zip(xs, sums, gs)]


def _attn_qkv_kernel(x_ref, g_ref, w_ref, qg_ref, kg_ref, ones_ref, o_ref):
    t = pl.program_id(1)

    @pl.when(t == 0)
    def _():
        o_ref[...] = jnp.zeros_like(o_ref)

    @pl.when(t > 0)
    def _():
        hb = _rms(x_ref[...], g_ref[...]).astype(BF16)
        ones_bd = ones_ref[...]
        blk = ones_bd.shape[0]
        o_ref[:, 2 * D_MODEL:] = _dot(hb, w_ref[:, 2 * D_MODEL:]).astype(o_ref.dtype)
        q_gain = qg_ref[...] * (C_HEAD ** -0.5)
        per_pass = D_MODEL // blk // 2
        for half in range(2):
            cols = [slice(c * blk, (c + 1) * blk) for c in range(half * per_pass, (half + 1) * per_pass)]
            cols = cols + [slice(D_MODEL + sl.start, D_MODEL + sl.stop) for sl in cols]
            gains = [q_gain] * (len(cols) // 2) + [kg_ref[...]] * (len(cols) // 2)
            normed = _head_rms_many([_dot(hb, w_ref[:, sl]) for sl in cols], ones_bd, gains)
            for sl, val in zip(cols, normed):
                o_ref[:, sl] = val.astype(o_ref.dtype)


def _attn_qkv(x2d, g, w, qg, kg, ones_bd, batch, seq, tm):
    tiles = seq // tm
    pad_tiles = C_WINDOW // tm
    const = lambda arr: pl.BlockSpec(arr.shape, lambda b, t: (0,) * arr.ndim)
    return pl.pallas_call(
        _attn_qkv_kernel,
        grid=(batch, tiles + pad_tiles),
        in_specs=[pl.BlockSpec((tm, D_MODEL), lambda b, t: (b * tiles + jnp.maximum(t - pad_tiles, 0), 0)),
                  const(g), const(w), const(qg), const(kg), const(ones_bd)],
        out_specs=pl.BlockSpec((tm, 3 * D_MODEL), lambda b, t: (b * (tiles + pad_tiles) + t, 0)),
        out_shape=jax.ShapeDtypeStruct((batch * (seq + C_WINDOW), 3 * D_MODEL), BF16),
        compiler_params=pltpu.CompilerParams(dimension_semantics=("parallel", "arbitrary"),
                                             vmem_limit_bytes=VMEM_LIMIT),
        name="attn_qkv",
    )(x2d, g, w, qg, kg, ones_bd)


def _attn_kernel(q_ref, k_ref, v_ref, bias_ref, o_ref):
    t = pl.program_id(2)
    lane_lo = lax.broadcasted_iota(jnp.int32, (CHUNK, LANES), 1) < C_HEAD
    zero = jnp.zeros((CHUNK, LANES), q_ref.dtype)

    def tile(mask_start):
        bias = bias_ref[...]
        key_idx = lax.broadcasted_iota(jnp.int32, (2 * CHUNK, C_BAND), 1)
        groups = [range(c0, c0 + C_IN_FLIGHT) for c0 in range(0, C_QTILE // CHUNK, C_IN_FLIGHT)]

        def first_key(cc):
            return pl.multiple_of(t * C_QTILE + cc * CHUNK, CHUNK)

        def qk(chunks):
            scores = []
            for cc in chunks:
                q = q_ref[cc * CHUNK:(cc + 1) * CHUNK, :]
                q2 = jnp.concatenate([jnp.where(lane_lo, q, zero), jnp.where(lane_lo, zero, q)], axis=0)
                s = _dot_nt(q2, k_ref[pl.ds(first_key(cc), C_BAND), :]) + bias
                if mask_start:
                    s = jnp.where(key_idx >= C_WINDOW - first_key(cc), s, MASK_VALUE)
                scores.append(s)
            return scores

        def softmax(scores):
            probs = [jnp.exp(s - jnp.max(s, axis=-1, keepdims=True)) for s in scores]
            return probs, [jnp.sum(p, axis=-1, keepdims=True) for p in probs]

        def pv(chunks, probs, denoms):
            outs = [_dot(p, v_ref[pl.ds(first_key(cc), C_BAND), :]) / d for cc, p, d in zip(chunks, probs, denoms)]
            for cc, o2 in zip(chunks, outs):
                o_ref[cc * CHUNK:(cc + 1) * CHUNK, :] = (
                    jnp.where(lane_lo, o2[0:CHUNK], o2[CHUNK:]).astype(o_ref.dtype))

        scores, normed = {}, {}
        for step in range(len(groups) + 2):
            if 0 <= step - 2:
                pv(groups[step - 2], *normed.pop(step - 2))
            if step < len(groups):
                scores[step] = qk(groups[step])
            if 0 <= step - 1 < len(groups):
                normed[step - 1] = softmax(scores.pop(step - 1))

    first_tiles = -(-C_WINDOW // C_QTILE)
    pl.when(t < first_tiles)(functools.partial(tile, True))
    pl.when(t >= first_tiles)(functools.partial(tile, False))


def _attn(qkv_pad, bias, batch, seq):
    nt = seq // C_QTILE
    pad_rows = seq + C_WINDOW
    n_pairs = D_MODEL // LANES
    q_off = C_WINDOW // C_QTILE
    return pl.pallas_call(
        _attn_kernel,
        grid=(batch, n_pairs, nt),
        in_specs=[pl.BlockSpec((C_QTILE, LANES), lambda b, h, t: (b * (pad_rows // C_QTILE) + q_off + t, h)),
                  pl.BlockSpec((pad_rows, LANES), lambda b, h, t: (b, n_pairs + h)),
                  pl.BlockSpec((pad_rows, LANES), lambda b, h, t: (b, 2 * n_pairs + h)),
                  pl.BlockSpec((2 * CHUNK, C_BAND), lambda b, h, t: (h, 0))],
        out_specs=pl.BlockSpec((C_QTILE, LANES), lambda b, h, t: (b * nt + t, h)),
        out_shape=jax.ShapeDtypeStruct((batch * seq, D_MODEL), BF16),
        compiler_params=pltpu.CompilerParams(dimension_semantics=("parallel", "parallel", "arbitrary"),
                                             vmem_limit_bytes=VMEM_LIMIT),
        name="band_attn",
    )(qkv_pad, qkv_pad, qkv_pad, bias.reshape(-1, C_BAND))


def _post_kernel(x_ref, p_ref, *rest, n_mix):
    mix_refs = rest[:n_mix]
    wo_ref, g_ref, w1_ref, w2_ref, wp_ref, pg_ref, wg_ref, o_ref = rest[n_mix:]
    mix = jnp.concatenate([mref[...] for mref in mix_refs], axis=1)
    x = x_ref[...] + _dot(mix, wo_ref[...])
    hb = _rms(x, g_ref[...]).astype(BF16)
    acc = None
    for j in range(D_FF // FF_CHUNK):
        sl = slice(j * FF_CHUNK, (j + 1) * FF_CHUNK)
        hid = jnp.maximum(_dot(hb, w1_ref[:, sl]), 0.0)
        part = _dot(hid * hid, w2_ref[sl, :])
        acc = part if acc is None else acc + part
    x = x + acc
    emb = _rms(_dot(p_ref[...], wp_ref[...]), pg_ref[...])
    o_ref[...] = x + emb * _sigmoid(_dot(x, wg_ref[...]))


def _post(x2d, p3d, layer, mixes, wo, g, w1, w2, wp, pg, wg, tm):
    m = x2d.shape[0]
    const = lambda arr: pl.BlockSpec(arr.shape, lambda i: (0,) * arr.ndim, pipeline_mode=pl.Buffered(1))
    row = lambda w: pl.BlockSpec((tm, w), lambda i: (i, 0))
    weights = [wo, g, w1, w2, wp, pg, wg]
    return pl.pallas_call(
        functools.partial(_post_kernel, n_mix=len(mixes)),
        grid=(m // tm,),
        in_specs=[row(D_MODEL), pl.BlockSpec((None, tm, p3d.shape[2]), lambda i: (layer, i, 0))]
                 + [row(mx.shape[1]) for mx in mixes] + [const(w) for w in weights],
        out_specs=row(D_MODEL),
        out_shape=jax.ShapeDtypeStruct((m, D_MODEL), F32),
        compiler_params=pltpu.CompilerParams(dimension_semantics=("parallel",), vmem_limit_bytes=VMEM_LIMIT),
        name="mix_out_mlp_ple",
    )(x2d, p3d, *mixes, *weights)


def _pad_to(w, rows=None, cols=None):
    r = (rows or w.shape[0]) - w.shape[0]
    c = (cols or w.shape[1]) - w.shape[1]
    return jnp.pad(w, ((0, r), (0, c)))


def _rel_bias_table(rel_bias):
    n_heads = rel_bias.shape[0]
    span = CHUNK + C_BAND - 1
    rel = (C_BAND - 1) - jnp.arange(span)
    f = rel_bias[:, jnp.clip(rel, -C_MAX_REL, C_MAX_REL) + C_MAX_REL]
    g = jnp.tile(jnp.pad(f, ((0, 0), (0, 1))), (1, CHUNK))[:, :CHUNK * span].reshape(n_heads, CHUNK, span)
    return g[:, :, CHUNK - 1:CHUNK - 1 + C_BAND]


def kernel(x, p, norm_mix_g, norm_ffn_g, even_w_in, rwkv_mu_proj, rwkv_mu_lora, rwkv_w0, rwkv_w1, rwkv_w2, rwkv_a0, rwkv_a1, rwkv_a2, rwkv_g1, rwkv_g2, rwkv_k_k, rwkv_k_a, rwkv_r_k, rwkv_ln_g, rwkv_ln_b, rwkv_v_mu, rwkv_v0, rwkv_v1, rwkv_v2, gdn_conv_w, gdn_a_log, gdn_dt_bias, gdn_norm_g, even_w_out, attn_w_qkv, attn_q_g, attn_k_g, attn_rel_bias, attn_w_out, mlp_w1, mlp_w2, ple_w_proj, ple_norm_g, ple_w_gate):
    batch, seq, _ = x.shape
    depth = p.shape[0]
    assert seq % 512 == 0 and x.shape[2] == D_MODEL
    tm_in, tm_post, tm_qkv = 512, 512, 512
    xs = x.reshape(batch * seq, D_MODEL)
    row1 = lambda vec: vec.reshape(1, -1)
    main_cols = 3 * A_WIDTH + 4 * B_WIDTH

    blk = 4 * C_HEAD
    ones_bd = (jnp.arange(blk)[:, None] // C_HEAD == jnp.arange(blk)[None, :] // C_HEAD).astype(BF16)

    v_first = None
    for i in range(depth):
        if i % 2 == 0:
            e = i // 2
            has_vres = e > 0
            win = even_w_in[e]
            mus = [rwkv_mu_lora[e, 0], rwkv_mu_lora[e, 1], rwkv_mu_lora[e, 2]]
            l1s = [rwkv_w1[e], rwkv_a1[e], rwkv_g1[e]]
            l2s = [rwkv_w2[e], rwkv_a2[e], rwkv_g2[e]]
            lbs = [rwkv_w0[e], rwkv_a0[e]]
            if has_vres:
                mus.append(rwkv_v_mu[e - 1])
                l1s.append(rwkv_v1[e - 1])
                l2s.append(rwkv_v2[e - 1])
                lbs.append(rwkv_v0[e - 1])
            l1 = jnp.stack([_pad_to(w, cols=A_LORA_PAD) for w in l1s]).astype(BF16)
            l2 = jnp.stack([_pad_to(w, rows=A_LORA_PAD) for w in l2s]).astype(BF16)
            outs = _even_in(xs, seq, row1(norm_mix_g[i]), win[:, :main_cols].astype(BF16),
                            _pad_to(win[:, main_cols:], cols=LANES).astype(BF16),
                            jnp.stack(mus), l1, l2, jnp.stack(lbs), has_vres, tm_in)
            rkv, qkvb, gate, ba, lw, a_lr, gg = outs[:7]
            par = jnp.stack([rwkv_k_k[e], rwkv_k_a[e], rwkv_r_k[e].reshape(-1), rwkv_ln_g[e], rwkv_ln_b[e]])
            y_a, v_first = _rwkv(rkv, lw, a_lr, gg, outs[7] if has_vres else None, v_first, par,
                                 rwkv_mu_proj[e].reshape(1, -1), batch, seq)
            hp = jnp.stack([_pad_to(jnp.pad(row1(gdn_a_log[e]), ((0, 0), (B_HEADS, 0))), cols=LANES)[0],
                            _pad_to(jnp.pad(row1(gdn_dt_bias[e]), ((0, 0), (B_HEADS, 0))), cols=LANES)[0]])
            y_b = _gdn(qkvb, gate, ba, gdn_conv_w[e], hp, row1(gdn_norm_g[e]), batch, seq)
            mixes, wo = [y_a, y_b], even_w_out[e]
        else:
            o = i // 2
            tile4 = lambda gvec: jnp.tile(gvec, blk // C_HEAD).reshape(1, blk)
            qkv_pad = _attn_qkv(xs, row1(norm_mix_g[i]), attn_w_qkv[o].astype(BF16), tile4(attn_q_g[o]),
                                tile4(attn_k_g[o]), ones_bd, batch, seq, tm_qkv)
            mixes, wo = [_attn(qkv_pad, _rel_bias_table(attn_rel_bias[o]), batch, seq)], attn_w_out[o]
        xs = _post(xs, p.reshape(depth, batch * seq, -1), i, mixes, wo.astype(BF16), row1(norm_ffn_g[i]),
                   mlp_w1[i].astype(BF16), mlp_w2[i].astype(BF16), ple_w_proj[i].astype(BF16),
                   row1(ple_norm_g[i]), ple_w_gate[i].astype(BF16), tm_post)
    return xs.reshape(batch, seq, D_MODEL)
```

```python
import functools

import jax
import jax.numpy as jnp
from jax import lax
from jax.experimental import pallas as pl
from jax.experimental.pallas import tpu as pltpu

F32 = jnp.float32
BF16 = jnp.bfloat16

D_MODEL = 1024
CHUNK = 64
NORM_EPS = 1e-6
L2_EPS = 1e-6
A_WIDTH = 512
A_HEAD = 64
A_GN_EPS = 64e-5
A_LORA_PAD = 128
A_GROUP = 256
A_SCAN_ROWS = 8 * CHUNK
B_SCAN_ROWS = 8 * CHUNK
A_UNIT = 4
B_UNIT = 4
B_WIDTH = 512
B_HEADS = 4
B_HEAD = 128
B_CONV = 4
C_HEADS = 16
C_HEAD = 64
C_WINDOW = 8 * CHUNK
C_BAND = C_WINDOW + CHUNK
C_QTILE = 8 * CHUNK
C_IN_FLIGHT = 2
C_MAX_REL = 256
D_FF = 4096
FF_CHUNK = 1024
LANES = 128
SUBLANES = 8
ROW_TILE = 512
INV_BLOCK = 16
MASK_VALUE = -1e30
V7X_VMEM_BYTES = 64 * 1024 * 1024
VMEM_LIMIT = V7X_VMEM_BYTES * 7 // 8


def _dot(a, b):
    return jnp.dot(a.astype(BF16), b.astype(BF16), preferred_element_type=F32)


def _dot_nt(a, b):
    return lax.dot_general(a.astype(BF16), b.astype(BF16), (((1,), (1,)), ((), ())),
                           preferred_element_type=F32)


def _dot_tn(a, b):
    return jnp.dot(a.astype(BF16).T, b.astype(BF16), preferred_element_type=F32)


def _rms(x, g, eps=NORM_EPS):
    return x * lax.rsqrt(jnp.mean(x * x, axis=-1, keepdims=True) + eps) * g


def _sigmoid(z):
    return 1.0 / (1.0 + jnp.exp(-z))


def _softplus(z):
    return jnp.maximum(z, 0.0) + jnp.log(1.0 + jnp.exp(-jnp.abs(z)))


def _shift_rows(x, prev_row):
    row = lax.broadcasted_iota(jnp.int32, x.shape, 0)
    return jnp.where(row == 0, prev_row, pltpu.roll(x, 1, 0))


def _cumsum_rows(x):
    row = lax.broadcasted_iota(jnp.int32, x.shape, 0)
    step = 1
    while step < x.shape[0]:
        x = x + jnp.where(row >= step, pltpu.roll(x, step, 0), 0.0)
        step *= 2
    return x


def _stack_heads(x, n_heads):
    head_w = x.shape[1] // n_heads
    lane_head = lax.broadcasted_iota(jnp.int32, x.shape, 1) >> (head_w.bit_length() - 1)
    zero = jnp.zeros_like(x)
    return jnp.concatenate([jnp.where(lane_head == h, x, zero) for h in range(n_heads)], axis=0)


def _unstack_heads(xs, n_heads):
    out = xs[0:CHUNK]
    for h in range(1, n_heads):
        out = out + xs[h * CHUNK:(h + 1) * CHUNK]
    return out


def _block_masks(n_heads):
    n = n_heads * CHUNK
    r = lax.broadcasted_iota(jnp.int32, (n, n), 0)
    c = lax.broadcasted_iota(jnp.int32, (n, n), 1)
    same_head = (r ^ c) < CHUNK
    delta = jnp.where(same_head, r - c, -1)
    return same_head, delta >= 0, delta > 0, (r ^ c) < INV_BLOCK


def _unit_lower_inverses(n_mats, same_block):
    nds = [jnp.where(same_block, n, 0.0) for n in n_mats]
    nos = [n - nd for n, nd in zip(n_mats, nds)]
    tds, pws = nds, nds
    for _ in range(3):
        pws = [_dot(pw, pw) for pw in pws]
        yield
        tds = [td + pw + _dot(td, pw) for td, pw in zip(tds, pws)]
        yield
    es = [no + _dot(td, no) for td, no in zip(tds, nos)]
    yield
    e2s = [_dot(e, e) for e in es]
    yield
    fs = [e + e2 + _dot(e, e2) for e, e2 in zip(es, e2s)]
    yield
    return [f + td + _dot(f, td) for f, td in zip(fs, tds)]


def _pipeline(units, front, tail):
    ready = None
    for step in range(len(units) + 1):
        gens = ([front(units[step])] if step < len(units) else []) + ([tail(*ready)] if ready is not None else [])
        results = [None] * len(gens)
        live = list(range(len(gens)))
        while live:
            for i in list(live):
                try:
                    next(gens[i])
                except StopIteration as done:
                    results[i] = done.value
                    live.remove(i)
        ready = results[0] if step < len(units) else None


def _seg_sum(x):
    seg_lo = lax.broadcasted_iota(jnp.int32, (x.shape[0], LANES), 1) < LANES // 2
    outs = []
    for j in range(x.shape[1] // LANES):
        blk = x[:, j * LANES:(j + 1) * LANES]
        s_lo = jnp.sum(jnp.where(seg_lo, blk, 0.0), axis=-1, keepdims=True)
        s_hi = jnp.sum(jnp.where(seg_lo, 0.0, blk), axis=-1, keepdims=True)
        outs.append(jnp.where(seg_lo, s_lo, s_hi))
    return jnp.concatenate(outs, axis=1)


def _even_in_kernel(x_ref, halo_ref, g_ref, win_ref, wtail_ref, mu_ref, l1_ref, l2_ref, lb_ref,
                    rkv_ref, qkvb_ref, gate_ref, ba_ref, lw_ref, a_ref, gg_ref, *vg_ref,
                    tiles_per_seq):
    i = pl.program_id(0)
    g = g_ref[...]
    h = _rms(x_ref[...], g)
    prev = _rms(halo_ref[...], g)[SUBLANES - 1:SUBLANES, :]
    prev = jnp.where(i % tiles_per_seq == 0, 0.0, prev)
    dh = _shift_rows(h, prev) - h
    hb = h.astype(BF16)
    rkv_ref[...] = _dot(hb, win_ref[:, 0:3 * A_WIDTH])
    qkvb_ref[...] = _dot(hb, win_ref[:, 3 * A_WIDTH:3 * A_WIDTH + 3 * B_WIDTH])
    gate_ref[...] = _dot(hb, win_ref[:, 3 * A_WIDTH + 3 * B_WIDTH:])
    ba_ref[...] = _dot(hb, wtail_ref[...])

    n_lora = 4 if vg_ref else 3
    mid = [_dot(h + dh * mu_ref[j:j + 1, :], l1_ref[j]) for j in range(n_lora)]
    mid[0] = jnp.tanh(mid[0])
    mid[2] = _sigmoid(mid[2])
    up = [_dot(mid[j], l2_ref[j]) for j in range(n_lora)]
    w_log = -_softplus(-(lb_ref[0:1, :] + up[0])) - 0.5
    lw_ref[...] = -jnp.exp(w_log)
    a_ref[...] = _sigmoid(lb_ref[1:2, :] + up[1])
    gg_ref[...] = up[2]
    if vg_ref:
        vg_ref[0][...] = _sigmoid(lb_ref[2:3, :] + up[3])


def _even_in(x2d, seq, g, win, wtail, mu, l1, l2, lb, has_vres, tm):
    m = x2d.shape[0]
    n_out = 8 if has_vres else 7
    widths = [3 * A_WIDTH, 3 * B_WIDTH, B_WIDTH, LANES, A_WIDTH, A_WIDTH, A_WIDTH, A_WIDTH][:n_out]
    const = lambda a: pl.BlockSpec(a.shape, lambda i: (0,) * a.ndim, pipeline_mode=pl.Buffered(1))
    return pl.pallas_call(
        functools.partial(_even_in_kernel, tiles_per_seq=seq // tm),
        grid=(m // tm,),
        in_specs=[pl.BlockSpec((tm, D_MODEL), lambda i: (i, 0)),
                  pl.BlockSpec((SUBLANES, D_MODEL), lambda i: (jnp.maximum(i * (tm // SUBLANES) - 1, 0), 0)),
                  const(g), const(win), const(wtail), const(mu), const(l1), const(l2), const(lb)],
        out_specs=[pl.BlockSpec((tm, w), lambda i: (i, 0)) for w in widths],
        out_shape=[jax.ShapeDtypeStruct((m, w), F32) for w in widths],
        compiler_params=pltpu.CompilerParams(dimension_semantics=("parallel",), vmem_limit_bytes=VMEM_LIMIT),
        name="even_in",
    )(x2d, x2d, g, win, wtail, mu, l1, l2, lb)


def _rwkv_kernel(rkv_ref, lw_ref, a_ref, gg_ref, *rest, has_vres):
    if has_vres:
        vg_ref, vfirst_ref, par_ref, mu_ref, y_ref, state_ref, tail_ref = rest
    else:
        par_ref, mu_ref, y_ref, vfirst_out_ref, state_ref, tail_ref = rest
    c = pl.program_id(1)

    @pl.when(c == 0)
    def _():
        state_ref[...] = jnp.zeros_like(state_ref)
        tail_ref[...] = jnp.zeros_like(tail_ref)

    rkv = rkv_ref[...]
    rkv_prev = _shift_rows(rkv, tail_ref[SUBLANES - 1:SUBLANES, :])
    tail_ref[...] = rkv[rkv.shape[0] - SUBLANES:, :]
    rkv = rkv + (rkv_prev - rkv) * mu_ref[...]
    r = rkv[:, 0:A_WIDTH]
    k = rkv[:, A_WIDTH:2 * A_WIDTH]
    v = rkv[:, 2 * A_WIDTH:]
    k_k, k_a, r_k, ln_g, ln_b = (par_ref[j:j + 1, :] for j in range(5))
    a = a_ref[...]
    if has_vres:
        v = v + (vfirst_ref[...] - v) * vg_ref[...]
    else:
        vfirst_out_ref[...] = v

    lw_all = lw_ref[...]
    kk_all = k * k_k
    kk_all = kk_all * lax.rsqrt(_seg_sum(kk_all * kk_all) + L2_EPS)
    k2_all = k * (1.0 + (a - 1.0) * k_a)
    b_all = kk_all * a
    bonus = _seg_sum(r * k2_all * r_k) * v
    v_bf_all = v.astype(BF16)

    heads = A_GROUP // A_HEAD
    n_groups = A_WIDTH // A_GROUP
    n_rows = heads * CHUNK
    same_head, incl, strict, same_block = _block_masks(heads)
    incl2 = jnp.concatenate([incl, incl], axis=1)
    n_chunks = rkv.shape[0] // CHUNK
    group_sl = [slice(g * A_GROUP, (g + 1) * A_GROUP) for g in range(n_groups)]

    states = [state_ref[g] for g in range(n_groups)]

    def front(unit):
        probs = []
        for cc in unit:
            rows = slice(cc * CHUNK, (cc + 1) * CHUNK)
            lw, kk, k2, b = lw_all[rows], kk_all[rows], k2_all[rows], b_all[rows]
            cw = _cumsum_rows(lw)
            cw_last = cw[CHUNK - 1:CHUNK, :]
            e_neg = jnp.exp(-cw)
            e_last = jnp.exp(cw_last - cw)
            r_t = (r[rows] * jnp.exp(cw)).astype(BF16)
            a_t = (-kk * jnp.exp(cw - lw)).astype(BF16)
            b_t = (b * e_neg).astype(BF16)
            k_t = (k2 * e_neg).astype(BF16)
            b_hat = (b * e_last).astype(BF16)
            k_hat = (k2 * e_last).astype(BF16)
            yield
            for sl in group_sl:
                lhs = jnp.concatenate([_stack_heads(a_t[:, sl], heads), _stack_heads(r_t[:, sl], heads)], axis=0)
                rhs = jnp.concatenate([b_t[:, sl]] * heads + [k_t[:, sl]] * heads, axis=0)
                probs.append(dict(
                    rows=rows, sl=sl, gram=_dot_nt(lhs, rhs),
                    ar=jnp.concatenate([a_t[:, sl], r_t[:, sl]], axis=0),
                    bk_hat=jnp.concatenate([b_hat[:, sl], k_hat[:, sl]], axis=0),
                    v=v_bf_all[rows, sl], w_last=jnp.exp(cw_last[:, sl])))
            yield
        minvs = yield from _unit_lower_inverses(
            [jnp.where(strict, pr["gram"][0:n_rows, 0:n_rows], 0.0) for pr in probs], same_block)
        return probs, minvs

    def tail(probs, minvs):
        for first in range(0, len(probs), n_groups):
            prs, mis = probs[first:first + n_groups], minvs[first:first + n_groups]
            from_state = [_dot_nt(pr["ar"], st) for pr, st in zip(prs, states)]
            v_s = [_stack_heads(pr["v"], heads) for pr in prs]
            yield
            xs = [fs[0:CHUNK]
                  + _unstack_heads(_dot(jnp.where(strict, pr["gram"][0:n_rows, n_rows:], 0.0), vs), heads)
                  for fs, pr, vs in zip(from_state, prs, v_s)]
            yield
            us = [x + _unstack_heads(_dot(mi, _stack_heads(x.astype(BF16), heads)), heads) for x, mi in zip(xs, mis)]
            u_bf = [u.astype(BF16) for u in us]
            yield
            ys = [fs[CHUNK:] + _unstack_heads(_dot(jnp.where(incl2, pr["gram"][n_rows:, :], 0.0),
                                                   jnp.concatenate([_stack_heads(ub, heads), vs], axis=0)), heads)
                  for fs, pr, ub, vs in zip(from_state, prs, u_bf, v_s)]
            upds = [_dot_tn(jnp.concatenate([ub, pr["v"]], axis=0), pr["bk_hat"]) for ub, pr in zip(u_bf, prs)]
            yield
            states[:] = [st * pr["w_last"] + jnp.where(same_head, upd, 0.0)
                         for st, pr, upd in zip(states, prs, upds)]
            for pr, y in zip(prs, ys):
                rows, sl = pr["rows"], pr["sl"]
                mean = _seg_sum(y) * (1.0 / A_HEAD)
                yc = y - mean
                var = _seg_sum(yc * yc) * (1.0 / A_HEAD)
                yn = yc * lax.rsqrt(var + A_GN_EPS) * ln_g[:, sl] + ln_b[:, sl]
                y_ref[rows, sl] = ((yn + bonus[rows, sl]) * gg_ref[rows, sl]).astype(y_ref.dtype)
            yield

    _pipeline([range(c0, c0 + A_UNIT) for c0 in range(0, n_chunks, A_UNIT)], front, tail)
    for g in range(n_groups):
        state_ref[g] = states[g]


def _rwkv(rkv, lw, a, gg, vg, vfirst, par, mu, batch, seq):
    m = rkv.shape[0]
    nc = seq // A_SCAN_ROWS
    has_vres = vg is not None
    row = lambda w: pl.BlockSpec((A_SCAN_ROWS, w), lambda b, c: (b * nc + c, 0))
    const = lambda arr: pl.BlockSpec(arr.shape, lambda b, c: (0,) * arr.ndim)
    ins = [rkv, lw, a, gg] + ([vg, vfirst] if has_vres else []) + [par, mu]
    in_specs = [row(3 * A_WIDTH)] + [row(A_WIDTH)] * (5 if has_vres else 3) + [const(par), const(mu)]
    n_out = 1 if has_vres else 2
    outs = pl.pallas_call(
        functools.partial(_rwkv_kernel, has_vres=has_vres),
        grid=(batch, nc),
        in_specs=in_specs,
        out_specs=[row(A_WIDTH)] * n_out,
        out_shape=[jax.ShapeDtypeStruct((m, A_WIDTH), dt) for dt in (BF16, F32)[:n_out]],
        scratch_shapes=[pltpu.VMEM((A_WIDTH // A_GROUP, A_GROUP, A_GROUP), F32),
                        pltpu.VMEM((SUBLANES, 3 * A_WIDTH), F32)],
        compiler_params=pltpu.CompilerParams(dimension_semantics=("arbitrary", "arbitrary"),
                                             vmem_limit_bytes=VMEM_LIMIT),
        name="rwkv7_scan",
    )(*ins)
    return outs if not has_vres else (outs[0], vfirst)


def _gdn_kernel(qkv_ref, gate_ref, ba_ref, conv_ref, hp_ref, ng_ref, y_ref, state_ref, zz_ref):
    c = pl.program_id(1)

    @pl.when(c == 0)
    def _():
        state_ref[...] = jnp.zeros_like(state_ref)
        zz_ref[0:SUBLANES, :] = jnp.zeros((SUBLANES, 3 * B_WIDTH), F32)

    z = qkv_ref[...]
    zz_ref[SUBLANES:, :] = z
    zz = zz_ref[...]
    zz_ref[0:SUBLANES, :] = z[z.shape[0] - SUBLANES:, :]
    conv = zz[SUBLANES:, :] * conv_ref[B_CONV - 1:B_CONV, :]
    for j in range(B_CONV - 1):
        conv = conv + pltpu.roll(zz, B_CONV - 1 - j, 0)[SUBLANES:, :] * conv_ref[j:j + 1, :]
    qkv = conv * _sigmoid(conv)

    def per_head(tile, lane0):
        return jnp.concatenate([jnp.broadcast_to(tile[:, lane0 + h:lane0 + h + 1], (CHUNK, B_HEAD))
                                for h in range(B_HEADS)], axis=1)

    def l2n(t):
        return jnp.concatenate(
            [t[:, h * B_HEAD:(h + 1) * B_HEAD]
             * lax.rsqrt(jnp.sum(jnp.square(t[:, h * B_HEAD:(h + 1) * B_HEAD]), axis=-1, keepdims=True) + L2_EPS)
             for h in range(B_HEADS)], axis=1)

    q_all = l2n(qkv[:, 0:B_WIDTH]) * (B_HEAD ** -0.5)
    k_all = l2n(qkv[:, B_WIDTH:2 * B_WIDTH])
    v_all = qkv[:, 2 * B_WIDTH:]
    ba_all = ba_ref[...]
    beta_all = _sigmoid(ba_all)
    g_step_all = -jnp.exp(hp_ref[0:1, :]) * _softplus(ba_all + hp_ref[1:2, :])

    same_head, incl, strict, same_block = _block_masks(B_HEADS)
    n_rows = B_HEADS * CHUNK
    def in_block(data, hd):
        blocks = [jnp.zeros_like(data)] * B_HEADS
        blocks[hd] = data
        return jnp.concatenate(blocks, axis=0)

    head_sl = [slice(hd * B_HEAD, (hd + 1) * B_HEAD) for hd in range(B_HEADS)]
    head_rows = [slice(hd * CHUNK, (hd + 1) * CHUNK) for hd in range(B_HEADS)]
    states = [state_ref[hd] for hd in range(B_HEADS)]

    def front(unit):
        probs = []
        for cc in unit:
            rows = slice(cc * CHUNK, (cc + 1) * CHUNK)
            q, k, v = q_all[rows], k_all[rows], v_all[rows]
            gc = _cumsum_rows(g_step_all[rows])
            gc_t = gc.T
            beta_f = per_head(beta_all[rows], 0)
            g_col = per_head(gc, B_HEADS)
            g_last = g_col[CHUNK - 1:CHUNK, :]
            e_g = jnp.exp(g_col)
            kb = k * beta_f
            g_col_s = jnp.concatenate([gc[:, B_HEADS + h:B_HEADS + h + 1] for h in range(B_HEADS)], axis=0)
            g_row_s = jnp.concatenate([gc_t[B_HEADS + h:B_HEADS + h + 1, :] for h in range(B_HEADS)], axis=1)
            decay = jnp.where(incl, jnp.exp(jnp.where(incl, g_col_s - g_row_s, 0.0)), 0.0)
            yield
            lhs = jnp.concatenate([_stack_heads(kb.astype(BF16), B_HEADS), _stack_heads(q.astype(BF16), B_HEADS)],
                                  axis=0)
            gram = _dot_nt(lhs, jnp.concatenate([k.astype(BF16)] * B_HEADS, axis=0))
            probs.append(dict(rows=rows, amat=jnp.where(strict, gram[0:n_rows] * decay, 0.0),
                              qk=(gram[n_rows:] * decay).astype(BF16), vb=v * beta_f, kbg=kb * e_g, qe=q * e_g,
                              kd=(k * jnp.exp(g_last - g_col)).astype(BF16), s_decay=jnp.exp(g_last)))
            yield
        tinvs = yield from _unit_lower_inverses([-pr["amat"] for pr in probs], same_block)
        solved = [[_dot(tinv[head_rows[hd]],
                        in_block(jnp.concatenate([pr["vb"][:, head_sl[hd]], pr["kbg"][:, head_sl[hd]]],
                                                 axis=1).astype(BF16), hd))
                   for hd in range(B_HEADS)] for tinv, pr in zip(tinvs, probs)]
        return probs, solved

    def tail(probs, solved):
        for pr, sol in zip(probs, solved):
            rows, qk, kd, g_last_e = pr["rows"], pr["qk"], pr["kd"], pr["s_decay"]
            us = [pr["vb"][:, sl] + s[:, 0:B_HEAD] for sl, s in zip(head_sl, sol)]
            wqs = [jnp.concatenate([pr["kbg"][:, sl] + s[:, B_HEAD:], pr["qe"][:, sl]], axis=0)
                   for sl, s in zip(head_sl, sol)]
            from_state = [_dot(wq, st) for wq, st in zip(wqs, states)]
            yield
            v_new = [u - fs[0:CHUNK] for u, fs in zip(us, from_state)]
            states[:] = [st * g_last_e[:, sl] + _dot_tn(kd[:, sl], vn) for st, sl, vn in zip(states, head_sl, v_new)]
            yield
            outs = [fs[CHUNK:] + _dot(qk[hr], in_block(vn.astype(BF16), hd))
                    for hd, (fs, hr, vn) in enumerate(zip(from_state, head_rows, v_new))]
            yield
            for sl, o in zip(head_sl, outs):
                gate = gate_ref[rows, sl]
                y_ref[rows, sl] = (_rms(o, ng_ref[...]) * (gate * _sigmoid(gate))).astype(y_ref.dtype)
            yield

    _pipeline([range(c0, c0 + B_UNIT) for c0 in range(0, z.shape[0] // CHUNK, B_UNIT)], front, tail)
    for hd in range(B_HEADS):
        state_ref[hd] = states[hd]


def _gdn(qkvb, gate, ba, conv_w, hp, norm_g, batch, seq):
    m = qkvb.shape[0]
    nc = seq // B_SCAN_ROWS
    row = lambda w: pl.BlockSpec((B_SCAN_ROWS, w), lambda b, c: (b * nc + c, 0))
    const = lambda arr: pl.BlockSpec(arr.shape, lambda b, c: (0,) * arr.ndim)
    return pl.pallas_call(
        _gdn_kernel,
        grid=(batch, nc),
        in_specs=[row(3 * B_WIDTH), row(B_WIDTH), row(LANES), const(conv_w), const(hp), const(norm_g)],
        out_specs=row(B_WIDTH),
        out_shape=jax.ShapeDtypeStruct((m, B_WIDTH), BF16),
        scratch_shapes=[pltpu.VMEM((B_HEADS, B_HEAD, B_HEAD), F32),
                        pltpu.VMEM((B_SCAN_ROWS + SUBLANES, 3 * B_WIDTH), F32)],
        compiler_params=pltpu.CompilerParams(dimension_semantics=("arbitrary", "arbitrary"),
                                             vmem_limit_bytes=VMEM_LIMIT),
        name="gdn_scan",
    )(qkvb, gate, ba, conv_w, hp, norm_g)


def _head_rms_many(xs, ones_bd, gs):
    sqs = [x * x for x in xs]
    his = [sq.astype(BF16) for sq in sqs]
    los = [(sq - hi.astype(F32)).astype(BF16) for sq, hi in zip(sqs, his)]
    sums = [jnp.dot(hi, ones_bd, preferred_element_type=F32) for hi in his]
    sums = [s + jnp.dot(lo, ones_bd, preferred_element_type=F32) for s, lo in zip(sums, los)]
    return [x * lax.rsqrt(s * (1.0 / C_HEAD) + NORM_EPS) * g for x, s, g in zip(xs, sums, gs)]


def _attn_qkv_kernel(x_ref, g_ref, w_ref, qg_ref, kg_ref, ones_ref, o_ref):
    t = pl.program_id(1)

    @pl.when(t == 0)
    def _():
        o_ref[...] = jnp.zeros_like(o_ref)

    @pl.when(t > 0)
    def _():
        hb = _rms(x_ref[...], g_ref[...]).astype(BF16)
        ones_bd = ones_ref[...]
        blk = ones_bd.shape[0]
        o_ref[:, 2 * D_MODEL:] = _dot(hb, w_ref[:, 2 * D_MODEL:]).astype(o_ref.dtype)
        q_gain = qg_ref[...] * (C_HEAD ** -0.5)
        per_pass = D_MODEL // blk // 2
        for half in range(2):
            cols = [slice(c * blk, (c + 1) * blk) for c in range(half * per_pass, (half + 1) * per_pass)]
            cols = cols + [slice(D_MODEL + sl.start, D_MODEL + sl.stop) for sl in cols]
            gains = [q_gain] * (len(cols) // 2) + [kg_ref[...]] * (len(cols) // 2)
            normed = _head_rms_many([_dot(hb, w_ref[:, sl]) for sl in cols], ones_bd, gains)
            for sl, val in zip(cols, normed):
                o_ref[:, sl] = val.astype(o_ref.dtype)


def _attn_qkv(x2d, g, w, qg, kg, ones_bd, batch, seq, tm):
    tiles = seq // tm
    pad_tiles = C_WINDOW // tm
    const = lambda arr: pl.BlockSpec(arr.shape, lambda b, t: (0,) * arr.ndim)
    return pl.pallas_call(
        _attn_qkv_kernel,
        grid=(batch, tiles + pad_tiles),
        in_specs=[pl.BlockSpec((tm, D_MODEL), lambda b, t: (b * tiles + jnp.maximum(t - pad_tiles, 0), 0)),
                  const(g), const(w), const(qg), const(kg), const(ones_bd)],
        out_specs=pl.BlockSpec((tm, 3 * D_MODEL), lambda b, t: (b * (tiles + pad_tiles) + t, 0)),
        out_shape=jax.ShapeDtypeStruct((batch * (seq + C_WINDOW), 3 * D_MODEL), BF16),
        compiler_params=pltpu.CompilerParams(dimension_semantics=("parallel", "arbitrary"),
                                             vmem_limit_bytes=VMEM_LIMIT),
        name="attn_qkv",
    )(x2d, g, w, qg, kg, ones_bd)


def _attn_kernel(q_ref, k_ref, v_ref, bias_ref, o_ref):
    t = pl.program_id(2)
    lane_lo = lax.broadcasted_iota(jnp.int32, (CHUNK, LANES), 1) < C_HEAD
    zero = jnp.zeros((CHUNK, LANES), q_ref.dtype)

    def tile(mask_start):
        bias = bias_ref[...]
        key_idx = lax.broadcasted_iota(jnp.int32, (2 * CHUNK, C_BAND), 1)
        groups = [range(c0, c0 + C_IN_FLIGHT) for c0 in range(0, C_QTILE // CHUNK, C_IN_FLIGHT)]

        def first_key(cc):
            return pl.multiple_of(t * C_QTILE + cc * CHUNK, CHUNK)

        def qk(chunks):
            scores = []
            for cc in chunks:
                q = q_ref[cc * CHUNK:(cc + 1) * CHUNK, :]
                q2 = jnp.concatenate([jnp.where(lane_lo, q, zero), jnp.where(lane_lo, zero, q)], axis=0)
                s = _dot_nt(q2, k_ref[pl.ds(first_key(cc), C_BAND), :]) + bias
                if mask_start:
                    s = jnp.where(key_idx >= C_WINDOW - first_key(cc), s, MASK_VALUE)
                scores.append(s)
            return scores

        def softmax(scores):
            probs = [jnp.exp(s - jnp.max(s, axis=-1, keepdims=True)) for s in scores]
            return probs, [jnp.sum(p, axis=-1, keepdims=True) for p in probs]

        def pv(chunks, probs, denoms):
            outs = [_dot(p, v_ref[pl.ds(first_key(cc), C_BAND), :]) / d for cc, p, d in zip(chunks, probs, denoms)]
            for cc, o2 in zip(chunks, outs):
                o_ref[cc * CHUNK:(cc + 1) * CHUNK, :] = (
                    jnp.where(lane_lo, o2[0:CHUNK], o2[CHUNK:]).astype(o_ref.dtype))

        scores, normed = {}, {}
        for step in range(len(groups) + 2):
            if 0 <= step - 2:
                pv(groups[step - 2], *normed.pop(step - 2))
            if step < len(groups):
                scores[step] = qk(groups[step])
            if 0 <= step - 1 < len(groups):
                normed[step - 1] = softmax(scores.pop(step - 1))

    first_tiles = -(-C_WINDOW // C_QTILE)
    pl.when(t < first_tiles)(functools.partial(tile, True))
    pl.when(t >= first_tiles)(functools.partial(tile, False))


def _attn(qkv_pad, bias, batch, seq):
    nt = seq // C_QTILE
    pad_rows = seq + C_WINDOW
    n_pairs = D_MODEL // LANES
    q_off = C_WINDOW // C_QTILE
    return pl.pallas_call(
        _attn_kernel,
        grid=(batch, n_pairs, nt),
        in_specs=[pl.BlockSpec((C_QTILE, LANES), lambda b, h, t: (b * (pad_rows // C_QTILE) + q_off + t, h)),
                  pl.BlockSpec((pad_rows, LANES), lambda b, h, t: (b, n_pairs + h)),
                  pl.BlockSpec((pad_rows, LANES), lambda b, h, t: (b, 2 * n_pairs + h)),
                  pl.BlockSpec((2 * CHUNK, C_BAND), lambda b, h, t: (h, 0))],
        out_specs=pl.BlockSpec((C_QTILE, LANES), lambda b, h, t: (b * nt + t, h)),
        out_shape=jax.ShapeDtypeStruct((batch * seq, D_MODEL), BF16),
        compiler_params=pltpu.CompilerParams(dimension_semantics=("parallel", "parallel", "arbitrary"),
                                             vmem_limit_bytes=VMEM_LIMIT),
        name="band_attn",
    )(qkv_pad, qkv_pad, qkv_pad, bias.reshape(-1, C_BAND))


def _post_kernel(x_ref, p_ref, *rest, n_mix):
    mix_refs = rest[:n_mix]
    wo_ref, g_ref, w1_ref, w2_ref, wp_ref, pg_ref, wg_ref, o_ref = rest[n_mix:]
    mix = jnp.concatenate([mref[...] for mref in mix_refs], axis=1)
    x = x_ref[...] + _dot(mix, wo_ref[...])
    hb = _rms(x, g_ref[...]).astype(BF16)
    acc = None
    for j in range(D_FF // FF_CHUNK):
        sl = slice(j * FF_CHUNK, (j + 1) * FF_CHUNK)
        hid = jnp.maximum(_dot(hb, w1_ref[:, sl]), 0.0)
        part = _dot(hid * hid, w2_ref[sl, :])
        acc = part if acc is None else acc + part
    x = x + acc
    emb = _rms(_dot(p_ref[...], wp_ref[...]), pg_ref[...])
    o_ref[...] = x + emb * _sigmoid(_dot(x, wg_ref[...]))


def _post(x2d, p3d, layer, mixes, wo, g, w1, w2, wp, pg, wg, tm):
    m = x2d.shape[0]
    const = lambda arr: pl.BlockSpec(arr.shape, lambda i: (0,) * arr.ndim, pipeline_mode=pl.Buffered(1))
    row = lambda w: pl.BlockSpec((tm, w), lambda i: (i, 0))
    weights = [wo, g, w1, w2, wp, pg, wg]
    return pl.pallas_call(
        functools.partial(_post_kernel, n_mix=len(mixes)),
        grid=(m // tm,),
        in_specs=[row(D_MODEL), pl.BlockSpec((None, tm, p3d.shape[2]), lambda i: (layer, i, 0))]
                 + [row(mx.shape[1]) for mx in mixes] + [const(w) for w in weights],
        out_specs=row(D_MODEL),
        out_shape=jax.ShapeDtypeStruct((m, D_MODEL), F32),
        compiler_params=pltpu.CompilerParams(dimension_semantics=("parallel",), vmem_limit_bytes=VMEM_LIMIT),
        name="mix_out_mlp_ple",
    )(x2d, p3d, *mixes, *weights)


def _pad_to(w, rows=None, cols=None):
    r = (rows or w.shape[0]) - w.shape[0]
    c = (cols or w.shape[1]) - w.shape[1]
    return jnp.pad(w, ((0, r), (0, c)))


def _rel_bias_table(rel_bias):
    n_heads = rel_bias.shape[0]
    span = CHUNK + C_BAND - 1
    rel = (C_BAND - 1) - jnp.arange(span)
    f = rel_bias[:, jnp.clip(rel, -C_MAX_REL, C_MAX_REL) + C_MAX_REL]
    g = jnp.tile(jnp.pad(f, ((0, 0), (0, 1))), (1, CHUNK))[:, :CHUNK * span].reshape(n_heads, CHUNK, span)
    return g[:, :, CHUNK - 1:CHUNK - 1 + C_BAND]


def kernel(x, p, norm_mix_g, norm_ffn_g, even_w_in, rwkv_mu_proj, rwkv_mu_lora, rwkv_w0, rwkv_w1, rwkv_w2, rwkv_a0, rwkv_a1, rwkv_a2, rwkv_g1, rwkv_g2, rwkv_k_k, rwkv_k_a, rwkv_r_k, rwkv_ln_g, rwkv_ln_b, rwkv_v_mu, rwkv_v0, rwkv_v1, rwkv_v2, gdn_conv_w, gdn_a_log, gdn_dt_bias, gdn_norm_g, even_w_out, attn_w_qkv, attn_q_g, attn_k_g, attn_rel_bias, attn_w_out, mlp_w1, mlp_w2, ple_w_proj, ple_norm_g, ple_w_gate):
    batch, seq, _ = x.shape
    depth = p.shape[0]
    assert x.shape[2] == D_MODEL
    assert all(seq % rows == 0 for rows in (ROW_TILE, A_SCAN_ROWS, B_SCAN_ROWS, C_QTILE))
    assert C_WINDOW % ROW_TILE == 0 and C_WINDOW % C_QTILE == 0
    tm_in = tm_post = tm_qkv = ROW_TILE
    xs = x.reshape(batch * seq, D_MODEL)
    row1 = lambda vec: vec.reshape(1, -1)
    main_cols = 3 * A_WIDTH + 4 * B_WIDTH

    blk = 4 * C_HEAD
    ones_bd = (jnp.arange(blk)[:, None] // C_HEAD == jnp.arange(blk)[None, :] // C_HEAD).astype(BF16)

    v_first = None
    for i in range(depth):
        if i % 2 == 0:
            e = i // 2
            has_vres = e > 0
            win = even_w_in[e]
            mus = [rwkv_mu_lora[e, 0], rwkv_mu_lora[e, 1], rwkv_mu_lora[e, 2]]
            l1s = [rwkv_w1[e], rwkv_a1[e], rwkv_g1[e]]
            l2s = [rwkv_w2[e], rwkv_a2[e], rwkv_g2[e]]
            lbs = [rwkv_w0[e], rwkv_a0[e]]
            if has_vres:
                mus.append(rwkv_v_mu[e - 1])
                l1s.append(rwkv_v1[e - 1])
                l2s.append(rwkv_v2[e - 1])
                lbs.append(rwkv_v0[e - 1])
            l1 = jnp.stack([_pad_to(w, cols=A_LORA_PAD) for w in l1s]).astype(BF16)
            l2 = jnp.stack([_pad_to(w, rows=A_LORA_PAD) for w in l2s]).astype(BF16)
            outs = _even_in(xs, seq, row1(norm_mix_g[i]), win[:, :main_cols].astype(BF16),
                            _pad_to(win[:, main_cols:], cols=LANES).astype(BF16),
                            jnp.stack(mus), l1, l2, jnp.stack(lbs), has_vres, tm_in)
            rkv, qkvb, gate, ba, lw, a_lr, gg = outs[:7]
            par = jnp.stack([rwkv_k_k[e], rwkv_k_a[e], rwkv_r_k[e].reshape(-1), rwkv_ln_g[e], rwkv_ln_b[e]])
            y_a, v_first = _rwkv(rkv, lw, a_lr, gg, outs[7] if has_vres else None, v_first, par,
                                 rwkv_mu_proj[e].reshape(1, -1), batch, seq)
            hp = jnp.stack([_pad_to(jnp.pad(row1(gdn_a_log[e]), ((0, 0), (B_HEADS, 0))), cols=LANES)[0],
                            _pad_to(jnp.pad(row1(gdn_dt_bias[e]), ((0, 0), (B_HEADS, 0))), cols=LANES)[0]])
            y_b = _gdn(qkvb, gate, ba, gdn_conv_w[e], hp, row1(gdn_norm_g[e]), batch, seq)
            mixes, wo = [y_a, y_b], even_w_out[e]
        else:
            o = i // 2
            tile4 = lambda gvec: jnp.tile(gvec, blk // C_HEAD).reshape(1, blk)
            qkv_pad = _attn_qkv(xs, row1(norm_mix_g[i]), attn_w_qkv[o].astype(BF16), tile4(attn_q_g[o]),
                                tile4(attn_k_g[o]), ones_bd, batch, seq, tm_qkv)
            mixes, wo = [_attn(qkv_pad, _rel_bias_table(attn_rel_bias[o]), batch, seq)], attn_w_out[o]
        xs = _post(xs, p.reshape(depth, batch * seq, -1), i, mixes, wo.astype(BF16), row1(norm_ffn_g[i]),
                   mlp_w1[i].astype(BF16), mlp_w2[i].astype(BF16), ple_w_proj[i].astype(BF16),
                   row1(ple_norm_g[i]), ple_w_gate[i].astype(BF16), tm_post)
    return xs.reshape(batch, seq, D_MODEL)
```

```python
import functools

import jax
import jax.numpy as jnp
from jax import lax
from jax.experimental import pallas as pl
from jax.experimental.pallas import tpu as pltpu

F32 = jnp.float32
BF16 = jnp.bfloat16

D_MODEL = 1024
CHUNK = 64
NORM_EPS = 1e-6
L2_EPS = 1e-6
A_WIDTH = 512
A_HEAD = 64
A_GN_EPS = 64e-5
A_LORA_PAD = 128
A_GROUP = 256
A_SCAN_ROWS = 8 * CHUNK
B_SCAN_ROWS = 8 * CHUNK
A_UNIT = 4
B_UNIT = 4
B_WIDTH = 512
B_HEADS = 4
B_HEAD = 128
B_CONV = 4
C_HEADS = 16
C_HEAD = 64
C_WINDOW = 8 * CHUNK
C_BAND = C_WINDOW + CHUNK
C_QTILE = 8 * CHUNK
C_IN_FLIGHT = 2
C_MAX_REL = 256
D_FF = 4096
FF_CHUNK = 1024
LANES = 128
SUBLANES = 8
ROW_TILE = 512
MASK_VALUE = -1e30
V7X_VMEM_BYTES = 64 * 1024 * 1024
VMEM_LIMIT = V7X_VMEM_BYTES * 7 // 8


def _dot(a, b):
    return jnp.dot(a.astype(BF16), b.astype(BF16), preferred_element_type=F32)


def _dot_nt(a, b):
    return lax.dot_general(a.astype(BF16), b.astype(BF16), (((1,), (1,)), ((), ())),
                           preferred_element_type=F32)


def _dot_tn(a, b):
    return jnp.dot(a.astype(BF16).T, b.astype(BF16), preferred_element_type=F32)


def _rms(x, g, eps=NORM_EPS):
    return x * lax.rsqrt(jnp.mean(x * x, axis=-1, keepdims=True) + eps) * g


def _sigmoid(z):
    return 1.0 / (1.0 + jnp.exp(-z))


def _softplus(z):
    return jnp.maximum(z, 0.0) + jnp.log(1.0 + jnp.exp(-jnp.abs(z)))


def _shift_rows(x, prev_row):
    row = lax.broadcasted_iota(jnp.int32, x.shape, 0)
    return jnp.where(row == 0, prev_row, pltpu.roll(x, 1, 0))


def _cumsum_rows(x):
    row = lax.broadcasted_iota(jnp.int32, x.shape, 0)
    step = 1
    while step < x.shape[0]:
        x = x + jnp.where(row >= step, pltpu.roll(x, step, 0), 0.0)
        step *= 2
    return x


def _stack_heads(x, n_heads):
    head_w = x.shape[1] // n_heads
    lane_head = lax.broadcasted_iota(jnp.int32, x.shape, 1) >> (head_w.bit_length() - 1)
    zero = jnp.zeros_like(x)
    return jnp.concatenate([jnp.where(lane_head == h, x, zero) for h in range(n_heads)], axis=0)


def _unstack_heads(xs, n_heads):
    out = xs[0:CHUNK]
    for h in range(1, n_heads):
        out = out + xs[h * CHUNK:(h + 1) * CHUNK]
    return out


def _block_masks(n_heads):
    n = n_heads * CHUNK
    r = lax.broadcasted_iota(jnp.int32, (n, n), 0)
    c = lax.broadcasted_iota(jnp.int32, (n, n), 1)
    same_head = (r ^ c) < CHUNK
    delta = jnp.where(same_head, r - c, -1)
    return same_head, delta >= 0, delta > 0, r ^ c


def _unit_lower_inverses(n_mats, block_xor):
    qs = [jnp.where(block_xor < 2, n, 0.0) for n in n_mats]
    size = 2
    while size < CHUNK:
        corner = (block_xor >> (size.bit_length() - 1)) == 1
        ncs = [jnp.where(corner, n, 0.0) for n in n_mats]
        ws = [nc + _dot(q, nc) for q, nc in zip(qs, ncs)]
        yield
        qs = [q + w + _dot(w, q) for q, w in zip(qs, ws)]
        yield
        size *= 2
    return qs


def _pipeline(units, front, tail):
    ready = None
    for step in range(len(units) + 1):
        gens = ([front(units[step])] if step < len(units) else []) + ([tail(*ready)] if ready is not None else [])
        results = [None] * len(gens)
        live = list(range(len(gens)))
        while live:
            for i in list(live):
                try:
                    next(gens[i])
                except StopIteration as done:
                    results[i] = done.value
                    live.remove(i)
        ready = results[0] if step < len(units) else None


def _seg_sum(x):
    seg_lo = lax.broadcasted_iota(jnp.int32, (x.shape[0], LANES), 1) < LANES // 2
    outs = []
    for j in range(x.shape[1] // LANES):
        blk = x[:, j * LANES:(j + 1) * LANES]
        s_lo = jnp.sum(jnp.where(seg_lo, blk, 0.0), axis=-1, keepdims=True)
        s_hi = jnp.sum(jnp.where(seg_lo, 0.0, blk), axis=-1, keepdims=True)
        outs.append(jnp.where(seg_lo, s_lo, s_hi))
    return jnp.concatenate(outs, axis=1)


def _even_in_kernel(x_ref, halo_ref, g_ref, win_ref, wtail_ref, mu_ref, l1_ref, l2_ref, lb_ref,
                    rkv_ref, qkvb_ref, gate_ref, ba_ref, lw_ref, a_ref, gg_ref, *vg_ref,
                    tiles_per_seq):
    i = pl.program_id(0)
    g = g_ref[...]
    h = _rms(x_ref[...], g)
    prev = _rms(halo_ref[...], g)[SUBLANES - 1:SUBLANES, :]
    prev = jnp.where(i % tiles_per_seq == 0, 0.0, prev)
    dh = _shift_rows(h, prev) - h
    hb = h.astype(BF16)
    rkv_ref[...] = _dot(hb, win_ref[:, 0:3 * A_WIDTH])
    qkvb_ref[...] = _dot(hb, win_ref[:, 3 * A_WIDTH:3 * A_WIDTH + 3 * B_WIDTH])
    gate_ref[...] = _dot(hb, win_ref[:, 3 * A_WIDTH + 3 * B_WIDTH:])
    ba_ref[...] = _dot(hb, wtail_ref[...])

    n_lora = 4 if vg_ref else 3
    mid = [_dot(h + dh * mu_ref[j:j + 1, :], l1_ref[j]) for j in range(n_lora)]
    mid[0] = jnp.tanh(mid[0])
    mid[2] = _sigmoid(mid[2])
    up = [_dot(mid[j], l2_ref[j]) for j in range(n_lora)]
    w_log = -_softplus(-(lb_ref[0:1, :] + up[0])) - 0.5
    lw_ref[...] = -jnp.exp(w_log)
    a_ref[...] = _sigmoid(lb_ref[1:2, :] + up[1])
    gg_ref[...] = up[2]
    if vg_ref:
        vg_ref[0][...] = _sigmoid(lb_ref[2:3, :] + up[3])


def _even_in(x2d, seq, g, win, wtail, mu, l1, l2, lb, has_vres, tm):
    m = x2d.shape[0]
    n_out = 8 if has_vres else 7
    widths = [3 * A_WIDTH, 3 * B_WIDTH, B_WIDTH, LANES, A_WIDTH, A_WIDTH, A_WIDTH, A_WIDTH][:n_out]
    const = lambda a: pl.BlockSpec(a.shape, lambda i: (0,) * a.ndim, pipeline_mode=pl.Buffered(1))
    return pl.pallas_call(
        functools.partial(_even_in_kernel, tiles_per_seq=seq // tm),
        grid=(m // tm,),
        in_specs=[pl.BlockSpec((tm, D_MODEL), lambda i: (i, 0)),
                  pl.BlockSpec((SUBLANES, D_MODEL), lambda i: (jnp.maximum(i * (tm // SUBLANES) - 1, 0), 0)),
                  const(g), const(win), const(wtail), const(mu), const(l1), const(l2), const(lb)],
        out_specs=[pl.BlockSpec((tm, w), lambda i: (i, 0)) for w in widths],
        out_shape=[jax.ShapeDtypeStruct((m, w), F32) for w in widths],
        compiler_params=pltpu.CompilerParams(dimension_semantics=("parallel",), vmem_limit_bytes=VMEM_LIMIT),
        name="even_in",
    )(x2d, x2d, g, win, wtail, mu, l1, l2, lb)


def _rwkv_kernel(rkv_ref, lw_ref, a_ref, gg_ref, *rest, has_vres):
    if has_vres:
        vg_ref, vfirst_ref, par_ref, mu_ref, y_ref, state_ref, tail_ref = rest
    else:
        par_ref, mu_ref, y_ref, vfirst_out_ref, state_ref, tail_ref = rest
    c = pl.program_id(1)

    @pl.when(c == 0)
    def _():
        state_ref[...] = jnp.zeros_like(state_ref)
        tail_ref[...] = jnp.zeros_like(tail_ref)

    rkv = rkv_ref[...]
    rkv_prev = _shift_rows(rkv, tail_ref[SUBLANES - 1:SUBLANES, :])
    tail_ref[...] = rkv[rkv.shape[0] - SUBLANES:, :]
    rkv = rkv + (rkv_prev - rkv) * mu_ref[...]
    r = rkv[:, 0:A_WIDTH]
    k = rkv[:, A_WIDTH:2 * A_WIDTH]
    v = rkv[:, 2 * A_WIDTH:]
    k_k, k_a, r_k, ln_g, ln_b = (par_ref[j:j + 1, :] for j in range(5))
    a = a_ref[...]
    if has_vres:
        v = v + (vfirst_ref[...] - v) * vg_ref[...]
    else:
        vfirst_out_ref[...] = v

    lw_all = lw_ref[...]
    kk_all = k * k_k
    kk_all = kk_all * lax.rsqrt(_seg_sum(kk_all * kk_all) + L2_EPS)
    k2_all = k * (1.0 + (a - 1.0) * k_a)
    b_all = kk_all * a
    bonus = _seg_sum(r * k2_all * r_k) * v
    v_bf_all = v.astype(BF16)

    heads = A_GROUP // A_HEAD
    n_groups = A_WIDTH // A_GROUP
    n_rows = heads * CHUNK
    same_head, incl, strict, block_xor = _block_masks(heads)
    incl2 = jnp.concatenate([incl, incl], axis=1)
    n_chunks = rkv.shape[0] // CHUNK
    group_sl = [slice(g * A_GROUP, (g + 1) * A_GROUP) for g in range(n_groups)]

    states = [state_ref[g] for g in range(n_groups)]

    def front(unit):
        probs = []
        for cc in unit:
            rows = slice(cc * CHUNK, (cc + 1) * CHUNK)
            lw, kk, k2, b = lw_all[rows], kk_all[rows], k2_all[rows], b_all[rows]
            cw = _cumsum_rows(lw)
            cw_last = cw[CHUNK - 1:CHUNK, :]
            e_neg = jnp.exp(-cw)
            e_last = jnp.exp(cw_last - cw)
            r_t = (r[rows] * jnp.exp(cw)).astype(BF16)
            a_t = (-kk * jnp.exp(cw - lw)).astype(BF16)
            b_t = (b * e_neg).astype(BF16)
            k_t = (k2 * e_neg).astype(BF16)
            b_hat = (b * e_last).astype(BF16)
            k_hat = (k2 * e_last).astype(BF16)
            yield
            for sl in group_sl:
                lhs = jnp.concatenate([_stack_heads(a_t[:, sl], heads), _stack_heads(r_t[:, sl], heads)], axis=0)
                rhs = jnp.concatenate([b_t[:, sl]] * heads + [k_t[:, sl]] * heads, axis=0)
                probs.append(dict(
                    rows=rows, sl=sl, gram=_dot_nt(lhs, rhs),
                    ar=jnp.concatenate([a_t[:, sl], r_t[:, sl]], axis=0),
                    bk_hat=jnp.concatenate([b_hat[:, sl], k_hat[:, sl]], axis=0),
                    v=v_bf_all[rows, sl], w_last=jnp.exp(cw_last[:, sl])))
            yield
        minvs = yield from _unit_lower_inverses(
            [jnp.where(strict, pr["gram"][0:n_rows, 0:n_rows], 0.0) for pr in probs], block_xor)
        return probs, minvs

    def tail(probs, minvs):
        for first in range(0, len(probs), n_groups):
            prs, mis = probs[first:first + n_groups], minvs[first:first + n_groups]
            from_state = [_dot_nt(pr["ar"], st) for pr, st in zip(prs, states)]
            v_s = [_stack_heads(pr["v"], heads) for pr in prs]
            yield
            xs = [fs[0:CHUNK]
                  + _unstack_heads(_dot(jnp.where(strict, pr["gram"][0:n_rows, n_rows:], 0.0), vs), heads)
                  for fs, pr, vs in zip(from_state, prs, v_s)]
            yield
            us = [x + _unstack_heads(_dot(mi, _stack_heads(x.astype(BF16), heads)), heads) for x, mi in zip(xs, mis)]
            u_bf = [u.astype(BF16) for u in us]
            yield
            ys = [fs[CHUNK:] + _unstack_heads(_dot(jnp.where(incl2, pr["gram"][n_rows:, :], 0.0),
                                                   jnp.concatenate([_stack_heads(ub, heads), vs], axis=0)), heads)
                  for fs, pr, ub, vs in zip(from_state, prs, u_bf, v_s)]
            upds = [_dot_tn(jnp.concatenate([ub, pr["v"]], axis=0), pr["bk_hat"]) for ub, pr in zip(u_bf, prs)]
            yield
            states[:] = [st * pr["w_last"] + jnp.where(same_head, upd, 0.0)
                         for st, pr, upd in zip(states, prs, upds)]
            for pr, y in zip(prs, ys):
                rows, sl = pr["rows"], pr["sl"]
                mean = _seg_sum(y) * (1.0 / A_HEAD)
                yc = y - mean
                var = _seg_sum(yc * yc) * (1.0 / A_HEAD)
                yn = yc * lax.rsqrt(var + A_GN_EPS) * ln_g[:, sl] + ln_b[:, sl]
                y_ref[rows, sl] = ((yn + bonus[rows, sl]) * gg_ref[rows, sl]).astype(y_ref.dtype)
            yield

    _pipeline([range(c0, c0 + A_UNIT) for c0 in range(0, n_chunks, A_UNIT)], front, tail)
    for g in range(n_groups):
        state_ref[g] = states[g]


def _rwkv(rkv, lw, a, gg, vg, vfirst, par, mu, batch, seq):
    m = rkv.shape[0]
    nc = seq // A_SCAN_ROWS
    has_vres = vg is not None
    row = lambda w: pl.BlockSpec((A_SCAN_ROWS, w), lambda b, c: (b * nc + c, 0))
    const = lambda arr: pl.BlockSpec(arr.shape, lambda b, c: (0,) * arr.ndim)
    ins = [rkv, lw, a, gg] + ([vg, vfirst] if has_vres else []) + [par, mu]
    in_specs = [row(3 * A_WIDTH)] + [row(A_WIDTH)] * (5 if has_vres else 3) + [const(par), const(mu)]
    n_out = 1 if has_vres else 2
    outs = pl.pallas_call(
        functools.partial(_rwkv_kernel, has_vres=has_vres),
        grid=(batch, nc),
        in_specs=in_specs,
        out_specs=[row(A_WIDTH)] * n_out,
        out_shape=[jax.ShapeDtypeStruct((m, A_WIDTH), dt) for dt in (BF16, F32)[:n_out]],
        scratch_shapes=[pltpu.VMEM((A_WIDTH // A_GROUP, A_GROUP, A_GROUP), F32),
                        pltpu.VMEM((SUBLANES, 3 * A_WIDTH), F32)],
        compiler_params=pltpu.CompilerParams(dimension_semantics=("arbitrary", "arbitrary"),
                                             vmem_limit_bytes=VMEM_LIMIT),
        name="rwkv7_scan",
    )(*ins)
    return outs if not has_vres else (outs[0], vfirst)


def _gdn_kernel(qkv_ref, gate_ref, ba_ref, conv_ref, hp_ref, ng_ref, y_ref, state_ref, zz_ref):
    c = pl.program_id(1)

    @pl.when(c == 0)
    def _():
        state_ref[...] = jnp.zeros_like(state_ref)
        zz_ref[0:SUBLANES, :] = jnp.zeros((SUBLANES, 3 * B_WIDTH), F32)

    z = qkv_ref[...]
    zz_ref[SUBLANES:, :] = z
    zz = zz_ref[...]
    zz_ref[0:SUBLANES, :] = z[z.shape[0] - SUBLANES:, :]
    conv = zz[SUBLANES:, :] * conv_ref[B_CONV - 1:B_CONV, :]
    for j in range(B_CONV - 1):
        conv = conv + pltpu.roll(zz, B_CONV - 1 - j, 0)[SUBLANES:, :] * conv_ref[j:j + 1, :]
    qkv = conv * _sigmoid(conv)

    def per_head(tile, lane0):
        return jnp.concatenate([jnp.broadcast_to(tile[:, lane0 + h:lane0 + h + 1], (CHUNK, B_HEAD))
                                for h in range(B_HEADS)], axis=1)

    def l2n(t):
        return jnp.concatenate(
            [t[:, h * B_HEAD:(h + 1) * B_HEAD]
             * lax.rsqrt(jnp.sum(jnp.square(t[:, h * B_HEAD:(h + 1) * B_HEAD]), axis=-1, keepdims=True) + L2_EPS)
             for h in range(B_HEADS)], axis=1)

    q_all = l2n(qkv[:, 0:B_WIDTH]) * (B_HEAD ** -0.5)
    k_all = l2n(qkv[:, B_WIDTH:2 * B_WIDTH])
    v_all = qkv[:, 2 * B_WIDTH:]
    ba_all = ba_ref[...]
    beta_all = _sigmoid(ba_all)
    g_step_all = -jnp.exp(hp_ref[0:1, :]) * _softplus(ba_all + hp_ref[1:2, :])

    same_head, incl, strict, block_xor = _block_masks(B_HEADS)
    n_rows = B_HEADS * CHUNK
    def in_block(data, hd):
        blocks = [jnp.zeros_like(data)] * B_HEADS
        blocks[hd] = data
        return jnp.concatenate(blocks, axis=0)

    head_sl = [slice(hd * B_HEAD, (hd + 1) * B_HEAD) for hd in range(B_HEADS)]
    head_rows = [slice(hd * CHUNK, (hd + 1) * CHUNK) for hd in range(B_HEADS)]
    states = [state_ref[hd] for hd in range(B_HEADS)]

    def front(unit):
        probs = []
        for cc in unit:
            rows = slice(cc * CHUNK, (cc + 1) * CHUNK)
            q, k, v = q_all[rows], k_all[rows], v_all[rows]
            gc = _cumsum_rows(g_step_all[rows])
            gc_t = gc.T
            beta_f = per_head(beta_all[rows], 0)
            g_col = per_head(gc, B_HEADS)
            g_last = g_col[CHUNK - 1:CHUNK, :]
            e_g = jnp.exp(g_col)
            kb = k * beta_f
            g_col_s = jnp.concatenate([gc[:, B_HEADS + h:B_HEADS + h + 1] for h in range(B_HEADS)], axis=0)
            g_row_s = jnp.concatenate([gc_t[B_HEADS + h:B_HEADS + h + 1, :] for h in range(B_HEADS)], axis=1)
            decay = jnp.where(incl, jnp.exp(jnp.where(incl, g_col_s - g_row_s, 0.0)), 0.0)
            yield
            lhs = jnp.concatenate([_stack_heads(kb.astype(BF16), B_HEADS), _stack_heads(q.astype(BF16), B_HEADS)],
                                  axis=0)
            gram = _dot_nt(lhs, jnp.concatenate([k.astype(BF16)] * B_HEADS, axis=0))
            probs.append(dict(rows=rows, amat=jnp.where(strict, gram[0:n_rows] * decay, 0.0),
                              qk=(gram[n_rows:] * decay).astype(BF16), vb=v * beta_f, kbg=kb * e_g, qe=q * e_g,
                              kd=(k * jnp.exp(g_last - g_col)).astype(BF16), s_decay=jnp.exp(g_last)))
            yield
        tinvs = yield from _unit_lower_inverses([-pr["amat"] for pr in probs], block_xor)
        solved = [[_dot(tinv[head_rows[hd]],
                        in_block(jnp.concatenate([pr["vb"][:, head_sl[hd]], pr["kbg"][:, head_sl[hd]]],
                                                 axis=1).astype(BF16), hd))
                   for hd in range(B_HEADS)] for tinv, pr in zip(tinvs, probs)]
        return probs, solved

    def tail(probs, solved):
        for pr, sol in zip(probs, solved):
            rows, qk, kd, g_last_e = pr["rows"], pr["qk"], pr["kd"], pr["s_decay"]
            us = [pr["vb"][:, sl] + s[:, 0:B_HEAD] for sl, s in zip(head_sl, sol)]
            wqs = [jnp.concatenate([pr["kbg"][:, sl] + s[:, B_HEAD:], pr["qe"][:, sl]], axis=0)
                   for sl, s in zip(head_sl, sol)]
            from_state = [_dot(wq, st) for wq, st in zip(wqs, states)]
            yield
            v_new = [u - fs[0:CHUNK] for u, fs in zip(us, from_state)]
            states[:] = [st * g_last_e[:, sl] + _dot_tn(kd[:, sl], vn) for st, sl, vn in zip(states, head_sl, v_new)]
            yield
            outs = [fs[CHUNK:] + _dot(qk[hr], in_block(vn.astype(BF16), hd))
                    for hd, (fs, hr, vn) in enumerate(zip(from_state, head_rows, v_new))]
            yield
            for sl, o in zip(head_sl, outs):
                gate = gate_ref[rows, sl]
                y_ref[rows, sl] = (_rms(o, ng_ref[...]) * (gate * _sigmoid(gate))).astype(y_ref.dtype)
            yield

    _pipeline([range(c0, c0 + B_UNIT) for c0 in range(0, z.shape[0] // CHUNK, B_UNIT)], front, tail)
    for hd in range(B_HEADS):
        state_ref[hd] = states[hd]


def _gdn(qkvb, gate, ba, conv_w, hp, norm_g, batch, seq):
    m = qkvb.shape[0]
    nc = seq // B_SCAN_ROWS
    row = lambda w: pl.BlockSpec((B_SCAN_ROWS, w), lambda b, c: (b * nc + c, 0))
    const = lambda arr: pl.BlockSpec(arr.shape, lambda b, c: (0,) * arr.ndim)
    return pl.pallas_call(
        _gdn_kernel,
        grid=(batch, nc),
        in_specs=[row(3 * B_WIDTH), row(B_WIDTH), row(LANES), const(conv_w), const(hp), const(norm_g)],
        out_specs=row(B_WIDTH),
        out_shape=jax.ShapeDtypeStruct((m, B_WIDTH), BF16),
        scratch_shapes=[pltpu.VMEM((B_HEADS, B_HEAD, B_HEAD), F32),
                        pltpu.VMEM((B_SCAN_ROWS + SUBLANES, 3 * B_WIDTH), F32)],
        compiler_params=pltpu.CompilerParams(dimension_semantics=("arbitrary", "arbitrary"),
                                             vmem_limit_bytes=VMEM_LIMIT),
        name="gdn_scan",
    )(qkvb, gate, ba, conv_w, hp, norm_g)


def _head_rms_many(xs, ones_bd, gs):
    sqs = [x * x for x in xs]
    his = [sq.astype(BF16) for sq in sqs]
    los = [(sq - hi.astype(F32)).astype(BF16) for sq, hi in zip(sqs, his)]
    sums = [jnp.dot(hi, ones_bd, preferred_element_type=F32) for hi in his]
    sums = [s + jnp.dot(lo, ones_bd, preferred_element_type=F32) for s, lo in zip(sums, los)]
    return [x * lax.rsqrt(s * (1.0 / C_HEAD) + NORM_EPS) * g for x, s, g in zip(xs, sums, gs)]


def _attn_qkv_kernel(x_ref, g_ref, w_ref, qg_ref, kg_ref, ones_ref, o_ref):
    t = pl.program_id(1)

    @pl.when(t == 0)
    def _():
        o_ref[...] = jnp.zeros_like(o_ref)

    @pl.when(t > 0)
    def _():
        hb = _rms(x_ref[...], g_ref[...]).astype(BF16)
        ones_bd = ones_ref[...]
        blk = ones_bd.shape[0]
        o_ref[:, 2 * D_MODEL:] = _dot(hb, w_ref[:, 2 * D_MODEL:]).astype(o_ref.dtype)
        q_gain = qg_ref[...] * (C_HEAD ** -0.5)
        per_pass = D_MODEL // blk // 2
        for half in range(2):
            cols = [slice(c * blk, (c + 1) * blk) for c in range(half * per_pass, (half + 1) * per_pass)]
            cols = cols + [slice(D_MODEL + sl.start, D_MODEL + sl.stop) for sl in cols]
            gains = [q_gain] * (len(cols) // 2) + [kg_ref[...]] * (len(cols) // 2)
            normed = _head_rms_many([_dot(hb, w_ref[:, sl]) for sl in cols], ones_bd, gains)
            for sl, val in zip(cols, normed):
                o_ref[:, sl] = val.astype(o_ref.dtype)


def _attn_qkv(x2d, g, w, qg, kg, ones_bd, batch, seq, tm):
    tiles = seq // tm
    pad_tiles = C_WINDOW // tm
    const = lambda arr: pl.BlockSpec(arr.shape, lambda b, t: (0,) * arr.ndim)
    return pl.pallas_call(
        _attn_qkv_kernel,
        grid=(batch, tiles + pad_tiles),
        in_specs=[pl.BlockSpec((tm, D_MODEL), lambda b, t: (b * tiles + jnp.maximum(t - pad_tiles, 0), 0)),
                  const(g), const(w), const(qg), const(kg), const(ones_bd)],
        out_specs=pl.BlockSpec((tm, 3 * D_MODEL), lambda b, t: (b * (tiles + pad_tiles) + t, 0)),
        out_shape=jax.ShapeDtypeStruct((batch * (seq + C_WINDOW), 3 * D_MODEL), BF16),
        compiler_params=pltpu.CompilerParams(dimension_semantics=("parallel", "arbitrary"),
                                             vmem_limit_bytes=VMEM_LIMIT),
        name="attn_qkv",
    )(x2d, g, w, qg, kg, ones_bd)


def _attn_kernel(q_ref, k_ref, v_ref, bias_ref, o_ref):
    t = pl.program_id(2)
    lane_lo = lax.broadcasted_iota(jnp.int32, (CHUNK, LANES), 1) < C_HEAD
    zero = jnp.zeros((CHUNK, LANES), q_ref.dtype)

    def tile(mask_start):
        bias = bias_ref[...]
        key_idx = lax.broadcasted_iota(jnp.int32, (2 * CHUNK, C_BAND), 1)
        groups = [range(c0, c0 + C_IN_FLIGHT) for c0 in range(0, C_QTILE // CHUNK, C_IN_FLIGHT)]

        def first_key(cc):
            return pl.multiple_of(t * C_QTILE + cc * CHUNK, CHUNK)

        def qk(chunks):
            scores = []
            for cc in chunks:
                q = q_ref[cc * CHUNK:(cc + 1) * CHUNK, :]
                q2 = jnp.concatenate([jnp.where(lane_lo, q, zero), jnp.where(lane_lo, zero, q)], axis=0)
                s = _dot_nt(q2, k_ref[pl.ds(first_key(cc), C_BAND), :]) + bias
                if mask_start:
                    s = jnp.where(key_idx >= C_WINDOW - first_key(cc), s, MASK_VALUE)
                scores.append(s)
            return scores

        def softmax(scores):
            probs = [jnp.exp(s - jnp.max(s, axis=-1, keepdims=True)) for s in scores]
            return probs, [jnp.sum(p, axis=-1, keepdims=True) for p in probs]

        def pv(chunks, probs, denoms):
            outs = [_dot(p, v_ref[pl.ds(first_key(cc), C_BAND), :]) / d for cc, p, d in zip(chunks, probs, denoms)]
            for cc, o2 in zip(chunks, outs):
                o_ref[cc * CHUNK:(cc + 1) * CHUNK, :] = (
                    jnp.where(lane_lo, o2[0:CHUNK], o2[CHUNK:]).astype(o_ref.dtype))

        scores, normed = {}, {}
        for step in range(len(groups) + 2):
            if 0 <= step - 2:
                pv(groups[step - 2], *normed.pop(step - 2))
            if step < len(groups):
                scores[step] = qk(groups[step])
            if 0 <= step - 1 < len(groups):
                normed[step - 1] = softmax(scores.pop(step - 1))

    first_tiles = -(-C_WINDOW // C_QTILE)
    pl.when(t < first_tiles)(functools.partial(tile, True))
    pl.when(t >= first_tiles)(functools.partial(tile, False))


def _attn(qkv_pad, bias, batch, seq):
    nt = seq // C_QTILE
    pad_rows = seq + C_WINDOW
    n_pairs = D_MODEL // LANES
    q_off = C_WINDOW // C_QTILE
    return pl.pallas_call(
        _attn_kernel,
        grid=(batch, n_pairs, nt),
        in_specs=[pl.BlockSpec((C_QTILE, LANES), lambda b, h, t: (b * (pad_rows // C_QTILE) + q_off + t, h)),
                  pl.BlockSpec((pad_rows, LANES), lambda b, h, t: (b, n_pairs + h)),
                  pl.BlockSpec((pad_rows, LANES), lambda b, h, t: (b, 2 * n_pairs + h)),
                  pl.BlockSpec((2 * CHUNK, C_BAND), lambda b, h, t: (h, 0))],
        out_specs=pl.BlockSpec((C_QTILE, LANES), lambda b, h, t: (b * nt + t, h)),
        out_shape=jax.ShapeDtypeStruct((batch * seq, D_MODEL), BF16),
        compiler_params=pltpu.CompilerParams(dimension_semantics=("parallel", "parallel", "arbitrary"),
                                             vmem_limit_bytes=VMEM_LIMIT),
        name="band_attn",
    )(qkv_pad, qkv_pad, qkv_pad, bias.reshape(-1, C_BAND))


def _post_kernel(x_ref, p_ref, *rest, n_mix):
    mix_refs = rest[:n_mix]
    wo_ref, g_ref, w1_ref, w2_ref, wp_ref, pg_ref, wg_ref, o_ref = rest[n_mix:]
    mix = jnp.concatenate([mref[...] for mref in mix_refs], axis=1)
    x = x_ref[...] + _dot(mix, wo_ref[...])
    hb = _rms(x, g_ref[...]).astype(BF16)
    acc = None
    for j in range(D_FF // FF_CHUNK):
        sl = slice(j * FF_CHUNK, (j + 1) * FF_CHUNK)
        hid = jnp.maximum(_dot(hb, w1_ref[:, sl]), 0.0)
        part = _dot(hid * hid, w2_ref[sl, :])
        acc = part if acc is None else acc + part
    x = x + acc
    emb = _rms(_dot(p_ref[...], wp_ref[...]), pg_ref[...])
    o_ref[...] = x + emb * _sigmoid(_dot(x, wg_ref[...]))


def _post(x2d, p3d, layer, mixes, wo, g, w1, w2, wp, pg, wg, tm):
    m = x2d.shape[0]
    const = lambda arr: pl.BlockSpec(arr.shape, lambda i: (0,) * arr.ndim, pipeline_mode=pl.Buffered(1))
    row = lambda w: pl.BlockSpec((tm, w), lambda i: (i, 0))
    weights = [wo, g, w1, w2, wp, pg, wg]
    return pl.pallas_call(
        functools.partial(_post_kernel, n_mix=len(mixes)),
        grid=(m // tm,),
        in_specs=[row(D_MODEL), pl.BlockSpec((None, tm, p3d.shape[2]), lambda i: (layer, i, 0))]
                 + [row(mx.shape[1]) for mx in mixes] + [const(w) for w in weights],
        out_specs=row(D_MODEL),
        out_shape=jax.ShapeDtypeStruct((m, D_MODEL), F32),
        compiler_params=pltpu.CompilerParams(dimension_semantics=("parallel",), vmem_limit_bytes=VMEM_LIMIT),
        name="mix_out_mlp_ple",
    )(x2d, p3d, *mixes, *weights)


def _pad_to(w, rows=None, cols=None):
    r = (rows or w.shape[0]) - w.shape[0]
    c = (cols or w.shape[1]) - w.shape[1]
    return jnp.pad(w, ((0, r), (0, c)))


def _rel_bias_table(rel_bias):
    n_heads = rel_bias.shape[0]
    span = CHUNK + C_BAND - 1
    rel = (C_BAND - 1) - jnp.arange(span)
    f = rel_bias[:, jnp.clip(rel, -C_MAX_REL, C_MAX_REL) + C_MAX_REL]
    g = jnp.tile(jnp.pad(f, ((0, 0), (0, 1))), (1, CHUNK))[:, :CHUNK * span].reshape(n_heads, CHUNK, span)
    return g[:, :, CHUNK - 1:CHUNK - 1 + C_BAND]


def kernel(x, p, norm_mix_g, norm_ffn_g, even_w_in, rwkv_mu_proj, rwkv_mu_lora, rwkv_w0, rwkv_w1, rwkv_w2, rwkv_a0, rwkv_a1, rwkv_a2, rwkv_g1, rwkv_g2, rwkv_k_k, rwkv_k_a, rwkv_r_k, rwkv_ln_g, rwkv_ln_b, rwkv_v_mu, rwkv_v0, rwkv_v1, rwkv_v2, gdn_conv_w, gdn_a_log, gdn_dt_bias, gdn_norm_g, even_w_out, attn_w_qkv, attn_q_g, attn_k_g, attn_rel_bias, attn_w_out, mlp_w1, mlp_w2, ple_w_proj, ple_norm_g, ple_w_gate):
    batch, seq, _ = x.shape
    depth = p.shape[0]
    assert x.shape[2] == D_MODEL
    assert all(seq % rows == 0 for rows in (ROW_TILE, A_SCAN_ROWS, B_SCAN_ROWS, C_QTILE))
    assert C_WINDOW % ROW_TILE == 0 and C_WINDOW % C_QTILE == 0
    tm_in = tm_post = tm_qkv = ROW_TILE
    xs = x.reshape(batch * seq, D_MODEL)
    row1 = lambda vec: vec.reshape(1, -1)
    main_cols = 3 * A_WIDTH + 4 * B_WIDTH

    blk = 4 * C_HEAD
    ones_bd = (jnp.arange(blk)[:, None] // C_HEAD == jnp.arange(blk)[None, :] // C_HEAD).astype(BF16)

    v_first = None
    for i in range(depth):
        if i % 2 == 0:
            e = i // 2
            has_vres = e > 0
            win = even_w_in[e]
            mus = [rwkv_mu_lora[e, 0], rwkv_mu_lora[e, 1], rwkv_mu_lora[e, 2]]
            l1s = [rwkv_w1[e], rwkv_a1[e], rwkv_g1[e]]
            l2s = [rwkv_w2[e], rwkv_a2[e], rwkv_g2[e]]
            lbs = [rwkv_w0[e], rwkv_a0[e]]
            if has_vres:
                mus.append(rwkv_v_mu[e - 1])
                l1s.append(rwkv_v1[e - 1])
                l2s.append(rwkv_v2[e - 1])
                lbs.append(rwkv_v0[e - 1])
            l1 = jnp.stack([_pad_to(w, cols=A_LORA_PAD) for w in l1s]).astype(BF16)
            l2 = jnp.stack([_pad_to(w, rows=A_LORA_PAD) for w in l2s]).astype(BF16)
            outs = _even_in(xs, seq, row1(norm_mix_g[i]), win[:, :main_cols].astype(BF16),
                            _pad_to(win[:, main_cols:], cols=LANES).astype(BF16),
                            jnp.stack(mus), l1, l2, jnp.stack(lbs), has_vres, tm_in)
            rkv, qkvb, gate, ba, lw, a_lr, gg = outs[:7]
            par = jnp.stack([rwkv_k_k[e], rwkv_k_a[e], rwkv_r_k[e].reshape(-1), rwkv_ln_g[e], rwkv_ln_b[e]])
            y_a, v_first = _rwkv(rkv, lw, a_lr, gg, outs[7] if has_vres else None, v_first, par,
                                 rwkv_mu_proj[e].reshape(1, -1), batch, seq)
            hp = jnp.stack([_pad_to(jnp.pad(row1(gdn_a_log[e]), ((0, 0), (B_HEADS, 0))), cols=LANES)[0],
                            _pad_to(jnp.pad(row1(gdn_dt_bias[e]), ((0, 0), (B_HEADS, 0))), cols=LANES)[0]])
            y_b = _gdn(qkvb, gate, ba, gdn_conv_w[e], hp, row1(gdn_norm_g[e]), batch, seq)
            mixes, wo = [y_a, y_b], even_w_out[e]
        else:
            o = i // 2
            tile4 = lambda gvec: jnp.tile(gvec, blk // C_HEAD).reshape(1, blk)
            qkv_pad = _attn_qkv(xs, row1(norm_mix_g[i]), attn_w_qkv[o].astype(BF16), tile4(attn_q_g[o]),
                                tile4(attn_k_g[o]), ones_bd, batch, seq, tm_qkv)
            mixes, wo = [_attn(qkv_pad, _rel_bias_table(attn_rel_bias[o]), batch, seq)], attn_w_out[o]
        xs = _post(xs, p.reshape(depth, batch * seq, -1), i, mixes, wo.astype(BF16), row1(norm_ffn_g[i]),
                   mlp_w1[i].astype(BF16), mlp_w2[i].astype(BF16), ple_w_proj[i].astype(BF16),
                   row1(ple_norm_g[i]), ple_w_gate[i].astype(BF16), tm_post)
    return xs.reshape(batch, seq, D_MODEL)
```

```python
import functools

import jax
import jax.numpy as jnp
from jax import lax
from jax.experimental import pallas as pl
from jax.experimental.pallas import tpu as pltpu

F32 = jnp.float32
BF16 = jnp.bfloat16

D_MODEL = 1024
CHUNK = 64
NORM_EPS = 1e-6
L2_EPS = 1e-6
A_WIDTH = 512
A_HEAD = 64
A_GN_EPS = 64e-5
A_LORA_PAD = 128
A_GROUP = 256
A_SCAN_ROWS = 8 * CHUNK
B_SCAN_ROWS = 8 * CHUNK
A_UNIT = 4
B_UNIT = 4
B_WIDTH = 512
B_HEADS = 4
B_HEAD = 128
B_CONV = 4
C_HEADS = 16
C_HEAD = 64
C_WINDOW = 8 * CHUNK
C_BAND = C_WINDOW + CHUNK
C_QTILE = 8 * CHUNK
C_IN_FLIGHT = 2
C_MAX_REL = 256
D_FF = 4096
FF_CHUNK = 1024
LANES = 128
SUBLANES = 8
ROW_TILE = 512
MASK_VALUE = -1e30
V7X_VMEM_BYTES = 64 * 1024 * 1024
VMEM_LIMIT = V7X_VMEM_BYTES * 7 // 8


def _dot(a, b):
    return jnp.dot(a.astype(BF16), b.astype(BF16), preferred_element_type=F32)


def _dot_nt(a, b):
    return lax.dot_general(a.astype(BF16), b.astype(BF16), (((1,), (1,)), ((), ())),
                           preferred_element_type=F32)


def _dot_tn(a, b):
    return jnp.dot(a.astype(BF16).T, b.astype(BF16), preferred_element_type=F32)


def _rms(x, g, eps=NORM_EPS):
    return x * lax.rsqrt(jnp.mean(x * x, axis=-1, keepdims=True) + eps) * g


def _sigmoid(z):
    return 1.0 / (1.0 + jnp.exp(-z))


def _softplus(z):
    return jnp.maximum(z, 0.0) + jnp.log(1.0 + jnp.exp(-jnp.abs(z)))


def _shift_rows(x, prev_row):
    row = lax.broadcasted_iota(jnp.int32, x.shape, 0)
    return jnp.where(row == 0, prev_row, pltpu.roll(x, 1, 0))


def _cumsum_rows(x):
    row = lax.broadcasted_iota(jnp.int32, x.shape, 0)
    step = 1
    while step < x.shape[0]:
        x = x + jnp.where(row >= step, pltpu.roll(x, step, 0), 0.0)
        step *= 2
    return x


def _stack_heads(x, n_heads):
    head_w = x.shape[1] // n_heads
    lane_head = lax.broadcasted_iota(jnp.int32, x.shape, 1) >> (head_w.bit_length() - 1)
    zero = jnp.zeros_like(x)
    return jnp.concatenate([jnp.where(lane_head == h, x, zero) for h in range(n_heads)], axis=0)


def _unstack_heads(xs, n_heads):
    out = xs[0:CHUNK]
    for h in range(1, n_heads):
        out = out + xs[h * CHUNK:(h + 1) * CHUNK]
    return out


def _block_masks(n_heads):
    n = n_heads * CHUNK
    r = lax.broadcasted_iota(jnp.int32, (n, n), 0)
    c = lax.broadcasted_iota(jnp.int32, (n, n), 1)
    same_head = (r ^ c) < CHUNK
    delta = jnp.where(same_head, r - c, -1)
    return same_head, delta >= 0, delta > 0, r ^ c


def _unit_lower_inverses(n_mats, block_xor):
    eye = block_xor == 0
    n_bf = [n.astype(BF16) for n in n_mats]
    zero, one = jnp.zeros_like(n_bf[0]), jnp.ones_like(n_bf[0])
    ts = [jnp.where(eye, one, jnp.where(block_xor < 2, n, zero)) for n in n_bf]
    size = 2
    while size < CHUNK:
        corner = (block_xor >> (size.bit_length() - 1)) == 1
        ws = [_dot(t, jnp.where(corner, n, zero)) for t, n in zip(ts, n_bf)]
        yield
        ts = [_dot(jnp.where(eye, one, w.astype(BF16)), t).astype(BF16) for w, t in zip(ws, ts)]
        yield
        size *= 2
    return ts


def _pipeline(units, front, tail):
    ready = None
    for step in range(len(units) + 1):
        gens = ([front(units[step])] if step < len(units) else []) + ([tail(*ready)] if ready is not None else [])
        results = [None] * len(gens)
        live = list(range(len(gens)))
        while live:
            for i in list(live):
                try:
                    next(gens[i])
                except StopIteration as done:
                    results[i] = done.value
                    live.remove(i)
        ready = results[0] if step < len(units) else None


def _seg_sum(x):
    seg_lo = lax.broadcasted_iota(jnp.int32, (x.shape[0], LANES), 1) < LANES // 2
    outs = []
    for j in range(x.shape[1] // LANES):
        blk = x[:, j * LANES:(j + 1) * LANES]
        s_lo = jnp.sum(jnp.where(seg_lo, blk, 0.0), axis=-1, keepdims=True)
        s_hi = jnp.sum(jnp.where(seg_lo, 0.0, blk), axis=-1, keepdims=True)
        outs.append(jnp.where(seg_lo, s_lo, s_hi))
    return jnp.concatenate(outs, axis=1)


def _even_in_kernel(x_ref, halo_ref, g_ref, win_ref, wtail_ref, mu_ref, l1_ref, l2_ref, lb_ref,
                    rkv_ref, qkvb_ref, gate_ref, ba_ref, lw_ref, a_ref, gg_ref, *vg_ref,
                    tiles_per_seq):
    i = pl.program_id(0)
    g = g_ref[...]
    h = _rms(x_ref[...], g)
    prev = _rms(halo_ref[...], g)[SUBLANES - 1:SUBLANES, :]
    prev = jnp.where(i % tiles_per_seq == 0, 0.0, prev)
    dh = _shift_rows(h, prev) - h
    hb = h.astype(BF16)
    rkv_ref[...] = _dot(hb, win_ref[:, 0:3 * A_WIDTH])
    qkvb_ref[...] = _dot(hb, win_ref[:, 3 * A_WIDTH:3 * A_WIDTH + 3 * B_WIDTH])
    gate_ref[...] = _dot(hb, win_ref[:, 3 * A_WIDTH + 3 * B_WIDTH:])
    ba_ref[...] = _dot(hb, wtail_ref[...])

    n_lora = 4 if vg_ref else 3
    mid = [_dot(h + dh * mu_ref[j:j + 1, :], l1_ref[j]) for j in range(n_lora)]
    mid[0] = jnp.tanh(mid[0])
    mid[2] = _sigmoid(mid[2])
    up = [_dot(mid[j], l2_ref[j]) for j in range(n_lora)]
    w_log = -_softplus(-(lb_ref[0:1, :] + up[0])) - 0.5
    lw_ref[...] = -jnp.exp(w_log)
    a_ref[...] = _sigmoid(lb_ref[1:2, :] + up[1])
    gg_ref[...] = up[2]
    if vg_ref:
        vg_ref[0][...] = _sigmoid(lb_ref[2:3, :] + up[3])


def _even_in(x2d, seq, g, win, wtail, mu, l1, l2, lb, has_vres, tm):
    m = x2d.shape[0]
    n_out = 8 if has_vres else 7
    widths = [3 * A_WIDTH, 3 * B_WIDTH, B_WIDTH, LANES, A_WIDTH, A_WIDTH, A_WIDTH, A_WIDTH][:n_out]
    const = lambda a: pl.BlockSpec(a.shape, lambda i: (0,) * a.ndim, pipeline_mode=pl.Buffered(1))
    return pl.pallas_call(
        functools.partial(_even_in_kernel, tiles_per_seq=seq // tm),
        grid=(m // tm,),
        in_specs=[pl.BlockSpec((tm, D_MODEL), lambda i: (i, 0)),
                  pl.BlockSpec((SUBLANES, D_MODEL), lambda i: (jnp.maximum(i * (tm // SUBLANES) - 1, 0), 0)),
                  const(g), const(win), const(wtail), const(mu), const(l1), const(l2), const(lb)],
        out_specs=[pl.BlockSpec((tm, w), lambda i: (i, 0)) for w in widths],
        out_shape=[jax.ShapeDtypeStruct((m, w), F32) for w in widths],
        compiler_params=pltpu.CompilerParams(dimension_semantics=("parallel",), vmem_limit_bytes=VMEM_LIMIT),
        name="even_in",
    )(x2d, x2d, g, win, wtail, mu, l1, l2, lb)


def _rwkv_kernel(rkv_ref, lw_ref, a_ref, gg_ref, *rest, has_vres):
    if has_vres:
        vg_ref, vfirst_ref, par_ref, mu_ref, y_ref, state_ref, tail_ref = rest
    else:
        par_ref, mu_ref, y_ref, vfirst_out_ref, state_ref, tail_ref = rest
    c = pl.program_id(1)

    @pl.when(c == 0)
    def _():
        state_ref[...] = jnp.zeros_like(state_ref)
        tail_ref[...] = jnp.zeros_like(tail_ref)

    rkv = rkv_ref[...]
    rkv_prev = _shift_rows(rkv, tail_ref[SUBLANES - 1:SUBLANES, :])
    tail_ref[...] = rkv[rkv.shape[0] - SUBLANES:, :]
    rkv = rkv + (rkv_prev - rkv) * mu_ref[...]
    r = rkv[:, 0:A_WIDTH]
    k = rkv[:, A_WIDTH:2 * A_WIDTH]
    v = rkv[:, 2 * A_WIDTH:]
    k_k, k_a, r_k, ln_g, ln_b = (par_ref[j:j + 1, :] for j in range(5))
    a = a_ref[...]
    if has_vres:
        v = v + (vfirst_ref[...] - v) * vg_ref[...]
    else:
        vfirst_out_ref[...] = v

    lw_all = lw_ref[...]
    kk_all = k * k_k
    kk_all = kk_all * lax.rsqrt(_seg_sum(kk_all * kk_all) + L2_EPS)
    k2_all = k * (1.0 + (a - 1.0) * k_a)
    b_all = kk_all * a
    bonus = _seg_sum(r * k2_all * r_k) * v
    v_bf_all = v.astype(BF16)

    heads = A_GROUP // A_HEAD
    n_groups = A_WIDTH // A_GROUP
    n_rows = heads * CHUNK
    same_head, incl, strict, block_xor = _block_masks(heads)
    incl2 = jnp.concatenate([incl, incl], axis=1)
    n_chunks = rkv.shape[0] // CHUNK
    group_sl = [slice(g * A_GROUP, (g + 1) * A_GROUP) for g in range(n_groups)]

    states = [state_ref[g] for g in range(n_groups)]

    def front(unit):
        probs = []
        for cc in unit:
            rows = slice(cc * CHUNK, (cc + 1) * CHUNK)
            lw, kk, k2, b = lw_all[rows], kk_all[rows], k2_all[rows], b_all[rows]
            cw = _cumsum_rows(lw)
            cw_last = cw[CHUNK - 1:CHUNK, :]
            e_neg = jnp.exp(-cw)
            e_last = jnp.exp(cw_last - cw)
            r_t = (r[rows] * jnp.exp(cw)).astype(BF16)
            a_t = (-kk * jnp.exp(cw - lw)).astype(BF16)
            b_t = (b * e_neg).astype(BF16)
            k_t = (k2 * e_neg).astype(BF16)
            b_hat = (b * e_last).astype(BF16)
            k_hat = (k2 * e_last).astype(BF16)
            yield
            for sl in group_sl:
                lhs = jnp.concatenate([_stack_heads(a_t[:, sl], heads), _stack_heads(r_t[:, sl], heads)], axis=0)
                rhs = jnp.concatenate([b_t[:, sl]] * heads + [k_t[:, sl]] * heads, axis=0)
                probs.append(dict(
                    rows=rows, sl=sl, gram=_dot_nt(lhs, rhs),
                    ar=jnp.concatenate([a_t[:, sl], r_t[:, sl]], axis=0),
                    bk_hat=jnp.concatenate([b_hat[:, sl], k_hat[:, sl]], axis=0),
                    v=v_bf_all[rows, sl], w_last=jnp.exp(cw_last[:, sl])))
            yield
        minvs = yield from _unit_lower_inverses(
            [jnp.where(strict, pr["gram"][0:n_rows, 0:n_rows], 0.0) for pr in probs], block_xor)
        return probs, minvs

    def tail(probs, minvs):
        for first in range(0, len(probs), n_groups):
            prs, mis = probs[first:first + n_groups], minvs[first:first + n_groups]
            from_state = [_dot_nt(pr["ar"], st) for pr, st in zip(prs, states)]
            v_s = [_stack_heads(pr["v"], heads) for pr in prs]
            yield
            xs = [fs[0:CHUNK]
                  + _unstack_heads(_dot(jnp.where(strict, pr["gram"][0:n_rows, n_rows:], 0.0), vs), heads)
                  for fs, pr, vs in zip(from_state, prs, v_s)]
            yield
            us = [_unstack_heads(_dot(mi, _stack_heads(x.astype(BF16), heads)), heads) for x, mi in zip(xs, mis)]
            u_bf = [u.astype(BF16) for u in us]
            yield
            ys = [fs[CHUNK:] + _unstack_heads(_dot(jnp.where(incl2, pr["gram"][n_rows:, :], 0.0),
                                                   jnp.concatenate([_stack_heads(ub, heads), vs], axis=0)), heads)
                  for fs, pr, ub, vs in zip(from_state, prs, u_bf, v_s)]
            upds = [_dot_tn(jnp.concatenate([ub, pr["v"]], axis=0), pr["bk_hat"]) for ub, pr in zip(u_bf, prs)]
            yield
            states[:] = [st * pr["w_last"] + jnp.where(same_head, upd, 0.0)
                         for st, pr, upd in zip(states, prs, upds)]
            for pr, y in zip(prs, ys):
                rows, sl = pr["rows"], pr["sl"]
                mean = _seg_sum(y) * (1.0 / A_HEAD)
                yc = y - mean
                var = _seg_sum(yc * yc) * (1.0 / A_HEAD)
                yn = yc * lax.rsqrt(var + A_GN_EPS) * ln_g[:, sl] + ln_b[:, sl]
                y_ref[rows, sl] = ((yn + bonus[rows, sl]) * gg_ref[rows, sl]).astype(y_ref.dtype)
            yield

    _pipeline([range(c0, c0 + A_UNIT) for c0 in range(0, n_chunks, A_UNIT)], front, tail)
    for g in range(n_groups):
        state_ref[g] = states[g]


def _rwkv(rkv, lw, a, gg, vg, vfirst, par, mu, batch, seq):
    m = rkv.shape[0]
    nc = seq // A_SCAN_ROWS
    has_vres = vg is not None
    row = lambda w: pl.BlockSpec((A_SCAN_ROWS, w), lambda b, c: (b * nc + c, 0))
    const = lambda arr: pl.BlockSpec(arr.shape, lambda b, c: (0,) * arr.ndim)
    ins = [rkv, lw, a, gg] + ([vg, vfirst] if has_vres else []) + [par, mu]
    in_specs = [row(3 * A_WIDTH)] + [row(A_WIDTH)] * (5 if has_vres else 3) + [const(par), const(mu)]
    n_out = 1 if has_vres else 2
    outs = pl.pallas_call(
        functools.partial(_rwkv_kernel, has_vres=has_vres),
        grid=(batch, nc),
        in_specs=in_specs,
        out_specs=[row(A_WIDTH)] * n_out,
        out_shape=[jax.ShapeDtypeStruct((m, A_WIDTH), dt) for dt in (BF16, F32)[:n_out]],
        scratch_shapes=[pltpu.VMEM((A_WIDTH // A_GROUP, A_GROUP, A_GROUP), F32),
                        pltpu.VMEM((SUBLANES, 3 * A_WIDTH), F32)],
        compiler_params=pltpu.CompilerParams(dimension_semantics=("arbitrary", "arbitrary"),
                                             vmem_limit_bytes=VMEM_LIMIT),
        name="rwkv7_scan",
    )(*ins)
    return outs if not has_vres else (outs[0], vfirst)


def _gdn_kernel(qkv_ref, gate_ref, ba_ref, conv_ref, hp_ref, ng_ref, y_ref, state_ref, zz_ref):
    c = pl.program_id(1)

    @pl.when(c == 0)
    def _():
        state_ref[...] = jnp.zeros_like(state_ref)
        zz_ref[0:SUBLANES, :] = jnp.zeros((SUBLANES, 3 * B_WIDTH), F32)

    z = qkv_ref[...]
    zz_ref[SUBLANES:, :] = z
    zz = zz_ref[...]
    zz_ref[0:SUBLANES, :] = z[z.shape[0] - SUBLANES:, :]
    conv = zz[SUBLANES:, :] * conv_ref[B_CONV - 1:B_CONV, :]
    for j in range(B_CONV - 1):
        conv = conv + pltpu.roll(zz, B_CONV - 1 - j, 0)[SUBLANES:, :] * conv_ref[j:j + 1, :]
    qkv = conv * _sigmoid(conv)

    def per_head(tile, lane0):
        return jnp.concatenate([jnp.broadcast_to(tile[:, lane0 + h:lane0 + h + 1], (CHUNK, B_HEAD))
                                for h in range(B_HEADS)], axis=1)

    def l2n(t):
        return jnp.concatenate(
            [t[:, h * B_HEAD:(h + 1) * B_HEAD]
             * lax.rsqrt(jnp.sum(jnp.square(t[:, h * B_HEAD:(h + 1) * B_HEAD]), axis=-1, keepdims=True) + L2_EPS)
             for h in range(B_HEADS)], axis=1)

    q_all = l2n(qkv[:, 0:B_WIDTH]) * (B_HEAD ** -0.5)
    k_all = l2n(qkv[:, B_WIDTH:2 * B_WIDTH])
    v_all = qkv[:, 2 * B_WIDTH:]
    ba_all = ba_ref[...]
    beta_all = _sigmoid(ba_all)
    g_step_all = -jnp.exp(hp_ref[0:1, :]) * _softplus(ba_all + hp_ref[1:2, :])

    same_head, incl, strict, block_xor = _block_masks(B_HEADS)
    n_rows = B_HEADS * CHUNK
    def in_block(data, hd):
        blocks = [jnp.zeros_like(data)] * B_HEADS
        blocks[hd] = data
        return jnp.concatenate(blocks, axis=0)

    head_sl = [slice(hd * B_HEAD, (hd + 1) * B_HEAD) for hd in range(B_HEADS)]
    head_rows = [slice(hd * CHUNK, (hd + 1) * CHUNK) for hd in range(B_HEADS)]
    states = [state_ref[hd] for hd in range(B_HEADS)]

    def front(unit):
        probs = []
        for cc in unit:
            rows = slice(cc * CHUNK, (cc + 1) * CHUNK)
            q, k, v = q_all[rows], k_all[rows], v_all[rows]
            gc = _cumsum_rows(g_step_all[rows])
            gc_t = gc.T
            beta_f = per_head(beta_all[rows], 0)
            g_col = per_head(gc, B_HEADS)
            g_last = g_col[CHUNK - 1:CHUNK, :]
            e_g = jnp.exp(g_col)
            kb = k * beta_f
            g_col_s = jnp.concatenate([gc[:, B_HEADS + h:B_HEADS + h + 1] for h in range(B_HEADS)], axis=0)
            g_row_s = jnp.concatenate([gc_t[B_HEADS + h:B_HEADS + h + 1, :] for h in range(B_HEADS)], axis=1)
            decay = jnp.where(incl, jnp.exp(jnp.where(incl, g_col_s - g_row_s, 0.0)), 0.0)
            yield
            lhs = jnp.concatenate([_stack_heads(kb.astype(BF16), B_HEADS), _stack_heads(q.astype(BF16), B_HEADS)],
                                  axis=0)
            gram = _dot_nt(lhs, jnp.concatenate([k.astype(BF16)] * B_HEADS, axis=0))
            probs.append(dict(rows=rows, amat=jnp.where(strict, gram[0:n_rows] * decay, 0.0),
                              qk=(gram[n_rows:] * decay).astype(BF16), vb=v * beta_f, kbg=kb * e_g, qe=q * e_g,
                              kd=(k * jnp.exp(g_last - g_col)).astype(BF16), s_decay=jnp.exp(g_last)))
            yield
        tinvs = yield from _unit_lower_inverses([-pr["amat"] for pr in probs], block_xor)
        solved = [[_dot(tinv[head_rows[hd]],
                        in_block(jnp.concatenate([pr["vb"][:, head_sl[hd]], pr["kbg"][:, head_sl[hd]]],
                                                 axis=1).astype(BF16), hd))
                   for hd in range(B_HEADS)] for tinv, pr in zip(tinvs, probs)]
        return probs, solved

    def tail(probs, solved):
        for pr, sol in zip(probs, solved):
            rows, qk, kd, g_last_e = pr["rows"], pr["qk"], pr["kd"], pr["s_decay"]
            us = [s[:, 0:B_HEAD] for s in sol]
            wqs = [jnp.concatenate([s[:, B_HEAD:], pr["qe"][:, sl]], axis=0)
                   for sl, s in zip(head_sl, sol)]
            from_state = [_dot(wq, st) for wq, st in zip(wqs, states)]
            yield
            v_new = [u - fs[0:CHUNK] for u, fs in zip(us, from_state)]
            states[:] = [st * g_last_e[:, sl] + _dot_tn(kd[:, sl], vn) for st, sl, vn in zip(states, head_sl, v_new)]
            yield
            outs = [fs[CHUNK:] + _dot(qk[hr], in_block(vn.astype(BF16), hd))
                    for hd, (fs, hr, vn) in enumerate(zip(from_state, head_rows, v_new))]
            yield
            for sl, o in zip(head_sl, outs):
                gate = gate_ref[rows, sl]
                y_ref[rows, sl] = (_rms(o, ng_ref[...]) * (gate * _sigmoid(gate))).astype(y_ref.dtype)
            yield

    _pipeline([range(c0, c0 + B_UNIT) for c0 in range(0, z.shape[0] // CHUNK, B_UNIT)], front, tail)
    for hd in range(B_HEADS):
        state_ref[hd] = states[hd]


def _gdn(qkvb, gate, ba, conv_w, hp, norm_g, batch, seq):
    m = qkvb.shape[0]
    nc = seq // B_SCAN_ROWS
    row = lambda w: pl.BlockSpec((B_SCAN_ROWS, w), lambda b, c: (b * nc + c, 0))
    const = lambda arr: pl.BlockSpec(arr.shape, lambda b, c: (0,) * arr.ndim)
    return pl.pallas_call(
        _gdn_kernel,
        grid=(batch, nc),
        in_specs=[row(3 * B_WIDTH), row(B_WIDTH), row(LANES), const(conv_w), const(hp), const(norm_g)],
        out_specs=row(B_WIDTH),
        out_shape=jax.ShapeDtypeStruct((m, B_WIDTH), BF16),
        scratch_shapes=[pltpu.VMEM((B_HEADS, B_HEAD, B_HEAD), F32),
                        pltpu.VMEM((B_SCAN_ROWS + SUBLANES, 3 * B_WIDTH), F32)],
        compiler_params=pltpu.CompilerParams(dimension_semantics=("arbitrary", "arbitrary"),
                                             vmem_limit_bytes=VMEM_LIMIT),
        name="gdn_scan",
    )(qkvb, gate, ba, conv_w, hp, norm_g)


def _head_rms_many(xs, ones_bd, gs):
    sqs = [x * x for x in xs]
    his = [sq.astype(BF16) for sq in sqs]
    los = [(sq - hi.astype(F32)).astype(BF16) for sq, hi in zip(sqs, his)]
    sums = [jnp.dot(hi, ones_bd, preferred_element_type=F32) for hi in his]
    sums = [s + jnp.dot(lo, ones_bd, preferred_element_type=F32) for s, lo in zip(sums, los)]
    return [x * lax.rsqrt(s * (1.0 / C_HEAD) + NORM_EPS) * g for x, s, g in zip(xs, sums, gs)]


def _attn_qkv_kernel(x_ref, g_ref, w_ref, qg_ref, kg_ref, ones_ref, o_ref):
    t = pl.program_id(1)

    @pl.when(t == 0)
    def _():
        o_ref[...] = jnp.zeros_like(o_ref)

    @pl.when(t > 0)
    def _():
        hb = _rms(x_ref[...], g_ref[...]).astype(BF16)
        ones_bd = ones_ref[...]
        blk = ones_bd.shape[0]
        o_ref[:, 2 * D_MODEL:] = _dot(hb, w_ref[:, 2 * D_MODEL:]).astype(o_ref.dtype)
        q_gain = qg_ref[...] * (C_HEAD ** -0.5)
        per_pass = D_MODEL // blk // 2
        for half in range(2):
            cols = [slice(c * blk, (c + 1) * blk) for c in range(half * per_pass, (half + 1) * per_pass)]
            cols = cols + [slice(D_MODEL + sl.start, D_MODEL + sl.stop) for sl in cols]
            gains = [q_gain] * (len(cols) // 2) + [kg_ref[...]] * (len(cols) // 2)
            normed = _head_rms_many([_dot(hb, w_ref[:, sl]) for sl in cols], ones_bd, gains)
            for sl, val in zip(cols, normed):
                o_ref[:, sl] = val.astype(o_ref.dtype)


def _attn_qkv(x2d, g, w, qg, kg, ones_bd, batch, seq, tm):
    tiles = seq // tm
    pad_tiles = C_WINDOW // tm
    const = lambda arr: pl.BlockSpec(arr.shape, lambda b, t: (0,) * arr.ndim)
    return pl.pallas_call(
        _attn_qkv_kernel,
        grid=(batch, tiles + pad_tiles),
        in_specs=[pl.BlockSpec((tm, D_MODEL), lambda b, t: (b * tiles + jnp.maximum(t - pad_tiles, 0), 0)),
                  const(g), const(w), const(qg), const(kg), const(ones_bd)],
        out_specs=pl.BlockSpec((tm, 3 * D_MODEL), lambda b, t: (b * (tiles + pad_tiles) + t, 0)),
        out_shape=jax.ShapeDtypeStruct((batch * (seq + C_WINDOW), 3 * D_MODEL), BF16),
        compiler_params=pltpu.CompilerParams(dimension_semantics=("parallel", "arbitrary"),
                                             vmem_limit_bytes=VMEM_LIMIT),
        name="attn_qkv",
    )(x2d, g, w, qg, kg, ones_bd)


def _attn_kernel(q_ref, k_ref, v_ref, bias_ref, o_ref):
    t = pl.program_id(2)
    lane_lo = lax.broadcasted_iota(jnp.int32, (CHUNK, LANES), 1) < C_HEAD
    zero = jnp.zeros((CHUNK, LANES), q_ref.dtype)

    def tile(mask_start):
        bias = bias_ref[...]
        key_idx = lax.broadcasted_iota(jnp.int32, (2 * CHUNK, C_BAND), 1)
        groups = [range(c0, c0 + C_IN_FLIGHT) for c0 in range(0, C_QTILE // CHUNK, C_IN_FLIGHT)]

        def first_key(cc):
            return pl.multiple_of(t * C_QTILE + cc * CHUNK, CHUNK)

        def qk(chunks):
            scores = []
            for cc in chunks:
                q = q_ref[cc * CHUNK:(cc + 1) * CHUNK, :]
                q2 = jnp.concatenate([jnp.where(lane_lo, q, zero), jnp.where(lane_lo, zero, q)], axis=0)
                s = _dot_nt(q2, k_ref[pl.ds(first_key(cc), C_BAND), :]) + bias
                if mask_start:
                    s = jnp.where(key_idx >= C_WINDOW - first_key(cc), s, MASK_VALUE)
                scores.append(s)
            return scores

        def softmax(scores):
            probs = [jnp.exp(s - jnp.max(s, axis=-1, keepdims=True)) for s in scores]
            return probs, [jnp.sum(p, axis=-1, keepdims=True) for p in probs]

        def pv(chunks, probs, denoms):
            outs = [_dot(p, v_ref[pl.ds(first_key(cc), C_BAND), :]) / d for cc, p, d in zip(chunks, probs, denoms)]
            for cc, o2 in zip(chunks, outs):
                o_ref[cc * CHUNK:(cc + 1) * CHUNK, :] = (
                    jnp.where(lane_lo, o2[0:CHUNK], o2[CHUNK:]).astype(o_ref.dtype))

        scores, normed = {}, {}
        for step in range(len(groups) + 2):
            if 0 <= step - 2:
                pv(groups[step - 2], *normed.pop(step - 2))
            if step < len(groups):
                scores[step] = qk(groups[step])
            if 0 <= step - 1 < len(groups):
                normed[step - 1] = softmax(scores.pop(step - 1))

    first_tiles = -(-C_WINDOW // C_QTILE)
    pl.when(t < first_tiles)(functools.partial(tile, True))
    pl.when(t >= first_tiles)(functools.partial(tile, False))


def _attn(qkv_pad, bias, batch, seq):
    nt = seq // C_QTILE
    pad_rows = seq + C_WINDOW
    n_pairs = D_MODEL // LANES
    q_off = C_WINDOW // C_QTILE
    return pl.pallas_call(
        _attn_kernel,
        grid=(batch, n_pairs, nt),
        in_specs=[pl.BlockSpec((C_QTILE, LANES), lambda b, h, t: (b * (pad_rows // C_QTILE) + q_off + t, h)),
                  pl.BlockSpec((pad_rows, LANES), lambda b, h, t: (b, n_pairs + h)),
                  pl.BlockSpec((pad_rows, LANES), lambda b, h, t: (b, 2 * n_pairs + h)),
                  pl.BlockSpec((2 * CHUNK, C_BAND), lambda b, h, t: (h, 0))],
        out_specs=pl.BlockSpec((C_QTILE, LANES), lambda b, h, t: (b * nt + t, h)),
        out_shape=jax.ShapeDtypeStruct((batch * seq, D_MODEL), BF16),
        compiler_params=pltpu.CompilerParams(dimension_semantics=("parallel", "parallel", "arbitrary"),
                                             vmem_limit_bytes=VMEM_LIMIT),
        name="band_attn",
    )(qkv_pad, qkv_pad, qkv_pad, bias.reshape(-1, C_BAND))


def _post_kernel(x_ref, p_ref, *rest, n_mix):
    mix_refs = rest[:n_mix]
    wo_ref, g_ref, w1_ref, w2_ref, wp_ref, pg_ref, wg_ref, o_ref = rest[n_mix:]
    mix = jnp.concatenate([mref[...] for mref in mix_refs], axis=1)
    x = x_ref[...] + _dot(mix, wo_ref[...])
    hb = _rms(x, g_ref[...]).astype(BF16)
    acc = None
    for j in range(D_FF // FF_CHUNK):
        sl = slice(j * FF_CHUNK, (j + 1) * FF_CHUNK)
        hid = jnp.maximum(_dot(hb, w1_ref[:, sl]), 0.0)
        part = _dot(hid * hid, w2_ref[sl, :])
        acc = part if acc is None else acc + part
    x = x + acc
    emb = _rms(_dot(p_ref[...], wp_ref[...]), pg_ref[...])
    o_ref[...] = x + emb * _sigmoid(_dot(x, wg_ref[...]))


def _post(x2d, p3d, layer, mixes, wo, g, w1, w2, wp, pg, wg, tm):
    m = x2d.shape[0]
    const = lambda arr: pl.BlockSpec(arr.shape, lambda i: (0,) * arr.ndim, pipeline_mode=pl.Buffered(1))
    row = lambda w: pl.BlockSpec((tm, w), lambda i: (i, 0))
    weights = [wo, g, w1, w2, wp, pg, wg]
    return pl.pallas_call(
        functools.partial(_post_kernel, n_mix=len(mixes)),
        grid=(m // tm,),
        in_specs=[row(D_MODEL), pl.BlockSpec((None, tm, p3d.shape[2]), lambda i: (layer, i, 0))]
                 + [row(mx.shape[1]) for mx in mixes] + [const(w) for w in weights],
        out_specs=row(D_MODEL),
        out_shape=jax.ShapeDtypeStruct((m, D_MODEL), F32),
        compiler_params=pltpu.CompilerParams(dimension_semantics=("parallel",), vmem_limit_bytes=VMEM_LIMIT),
        name="mix_out_mlp_ple",
    )(x2d, p3d, *mixes, *weights)


def _pad_to(w, rows=None, cols=None):
    r = (rows or w.shape[0]) - w.shape[0]
    c = (cols or w.shape[1]) - w.shape[1]
    return jnp.pad(w, ((0, r), (0, c)))


def _rel_bias_table(rel_bias):
    n_heads = rel_bias.shape[0]
    span = CHUNK + C_BAND - 1
    rel = (C_BAND - 1) - jnp.arange(span)
    f = rel_bias[:, jnp.clip(rel, -C_MAX_REL, C_MAX_REL) + C_MAX_REL]
    g = jnp.tile(jnp.pad(f, ((0, 0), (0, 1))), (1, CHUNK))[:, :CHUNK * span].reshape(n_heads, CHUNK, span)
    return g[:, :, CHUNK - 1:CHUNK - 1 + C_BAND]


def kernel(x, p, norm_mix_g, norm_ffn_g, even_w_in, rwkv_mu_proj, rwkv_mu_lora, rwkv_w0, rwkv_w1, rwkv_w2, rwkv_a0, rwkv_a1, rwkv_a2, rwkv_g1, rwkv_g2, rwkv_k_k, rwkv_k_a, rwkv_r_k, rwkv_ln_g, rwkv_ln_b, rwkv_v_mu, rwkv_v0, rwkv_v1, rwkv_v2, gdn_conv_w, gdn_a_log, gdn_dt_bias, gdn_norm_g, even_w_out, attn_w_qkv, attn_q_g, attn_k_g, attn_rel_bias, attn_w_out, mlp_w1, mlp_w2, ple_w_proj, ple_norm_g, ple_w_gate):
    batch, seq, _ = x.shape
    depth = p.shape[0]
    assert x.shape[2] == D_MODEL
    assert all(seq % rows == 0 for rows in (ROW_TILE, A_SCAN_ROWS, B_SCAN_ROWS, C_QTILE))
    assert C_WINDOW % ROW_TILE == 0 and C_WINDOW % C_QTILE == 0
    tm_in = tm_post = tm_qkv = ROW_TILE
    xs = x.reshape(batch * seq, D_MODEL)
    row1 = lambda vec: vec.reshape(1, -1)
    main_cols = 3 * A_WIDTH + 4 * B_WIDTH

    blk = 4 * C_HEAD
    ones_bd = (jnp.arange(blk)[:, None] // C_HEAD == jnp.arange(blk)[None, :] // C_HEAD).astype(BF16)

    v_first = None
    for i in range(depth):
        if i % 2 == 0:
            e = i // 2
            has_vres = e > 0
            win = even_w_in[e]
            mus = [rwkv_mu_lora[e, 0], rwkv_mu_lora[e, 1], rwkv_mu_lora[e, 2]]
            l1s = [rwkv_w1[e], rwkv_a1[e], rwkv_g1[e]]
            l2s = [rwkv_w2[e], rwkv_a2[e], rwkv_g2[e]]
            lbs = [rwkv_w0[e], rwkv_a0[e]]
            if has_vres:
                mus.append(rwkv_v_mu[e - 1])
                l1s.append(rwkv_v1[e - 1])
                l2s.append(rwkv_v2[e - 1])
                lbs.append(rwkv_v0[e - 1])
            l1 = jnp.stack([_pad_to(w, cols=A_LORA_PAD) for w in l1s]).astype(BF16)
            l2 = jnp.stack([_pad_to(w, rows=A_LORA_PAD) for w in l2s]).astype(BF16)
            outs = _even_in(xs, seq, row1(norm_mix_g[i]), win[:, :main_cols].astype(BF16),
                            _pad_to(win[:, main_cols:], cols=LANES).astype(BF16),
                            jnp.stack(mus), l1, l2, jnp.stack(lbs), has_vres, tm_in)
            rkv, qkvb, gate, ba, lw, a_lr, gg = outs[:7]
            par = jnp.stack([rwkv_k_k[e], rwkv_k_a[e], rwkv_r_k[e].reshape(-1), rwkv_ln_g[e], rwkv_ln_b[e]])
            y_a, v_first = _rwkv(rkv, lw, a_lr, gg, outs[7] if has_vres else None, v_first, par,
                                 rwkv_mu_proj[e].reshape(1, -1), batch, seq)
            hp = jnp.stack([_pad_to(jnp.pad(row1(gdn_a_log[e]), ((0, 0), (B_HEADS, 0))), cols=LANES)[0],
                            _pad_to(jnp.pad(row1(gdn_dt_bias[e]), ((0, 0), (B_HEADS, 0))), cols=LANES)[0]])
            y_b = _gdn(qkvb, gate, ba, gdn_conv_w[e], hp, row1(gdn_norm_g[e]), batch, seq)
            mixes, wo = [y_a, y_b], even_w_out[e]
        else:
            o = i // 2
            tile4 = lambda gvec: jnp.tile(gvec, blk // C_HEAD).reshape(1, blk)
            qkv_pad = _attn_qkv(xs, row1(norm_mix_g[i]), attn_w_qkv[o].astype(BF16), tile4(attn_q_g[o]),
                                tile4(attn_k_g[o]), ones_bd, batch, seq, tm_qkv)
            mixes, wo = [_attn(qkv_pad, _rel_bias_table(attn_rel_bias[o]), batch, seq)], attn_w_out[o]
        xs = _post(xs, p.reshape(depth, batch * seq, -1), i, mixes, wo.astype(BF16), row1(norm_ffn_g[i]),
                   mlp_w1[i].astype(BF16), mlp_w2[i].astype(BF16), ple_w_proj[i].astype(BF16),
                   row1(ple_norm_g[i]), ple_w_gate[i].astype(BF16), tm_post)
    return xs.reshape(batch, seq, D_MODEL)
```
